```python
import math, functools
import jax, jax.numpy as jnp
from jax import lax
import numpy as np

D_MODEL = 1024
BATCH = 8
SEQ = 2048
DEPTH = 4
DEC_BATCH = 32
DEC_SEQ = 4
PAST_LEN = 8192
PAGE_SIZE = 128

F32 = jnp.float32
BRANCH_WIDTH = D_MODEL // 2
N_BRANCH = 4
A_HEAD_DIM = 64
A_HEADS = BRANCH_WIDTH // A_HEAD_DIM
A_KV_HEADS = A_HEADS // 2
IDX_HEADS = 8
IDX_DIM = 64
TOPK_MAX = 256
Q_BLOCK = 128
R_HEADS = 4
R_DK = BRANCH_WIDTH // R_HEADS
R_DV = BRANCH_WIDTH // R_HEADS
R_CHUNK = 128
ROPE_BASE = 10000.0
H_HEADS = 4
H_DK = 128
H_DV = BRANCH_WIDTH // H_HEADS
H_CHUNK = 64
MIN_FORGET = 1e-30
M_HEAD_DIM = 64
M_HEADS = BRANCH_WIDTH // M_HEAD_DIM
M_INNER = M_HEADS * M_HEAD_DIM
M_GROUPS = 2
M_STATE = 128
M_CONV = 4
M_CHUNK = 128
M_CONV_DIM = M_INNER + 2 * M_GROUPS * M_STATE
D_FF = -(-8 * D_MODEL // (3 * 256)) * 256
RMS_EPS = 1e-6
GN_EPS = 1e-6
MASK_VALUE = -1e30

SPLIT_SIZES = (
    A_HEADS * A_HEAD_DIM, A_KV_HEADS * A_HEAD_DIM, A_KV_HEADS * A_HEAD_DIM,
    IDX_HEADS * IDX_DIM, IDX_DIM, IDX_HEADS,
    R_HEADS * R_DK, R_HEADS * R_DK, R_HEADS * R_DV, R_HEADS * R_DV,
    H_HEADS * H_DK, H_HEADS * H_DK, H_HEADS * H_DV, H_HEADS * H_DV,
    M_INNER, M_CONV_DIM, M_HEADS,
    N_BRANCH * D_MODEL,
)
SPLIT_POINTS = tuple(sum(SPLIT_SIZES[:i + 1]) for i in range(len(SPLIT_SIZES) - 1))
D_IN = sum(SPLIT_SIZES)

kernel_name = 'hybrid_dsa_retnet_hgrn2_mamba2_step'


def rmsnorm(x, g=None, eps=RMS_EPS):
    xf = x.astype(F32)
    y = xf * lax.rsqrt(jnp.mean(xf * xf, axis=-1, keepdims=True) + eps)
    if g is not None:
        y = y * g.astype(F32)
    return y.astype(x.dtype)


def pick_chunk(T, chunk):
    return chunk if T % chunk == 0 else T


def chunked_scan(step, S0, xs, chunk):
    B, T = xs[0].shape[:2]
    n = T // chunk
    xs_c = tuple(jnp.swapaxes(a.reshape(B, n, chunk, *a.shape[2:]), 0, 1) for a in xs)
    S, ys = lax.scan(lambda s, inp: step(s, *inp), S0, xs_c)
    ys = jnp.swapaxes(ys, 0, 1)
    return S, ys.reshape(B, T, *ys.shape[3:])


def rotary(x, pos):
    half = x.shape[-1] // 2
    inv_freq = ROPE_BASE ** (-jnp.arange(half, dtype=F32) / half)
    ang = pos.astype(F32)[:, None] * inv_freq[None, :]
    c = jnp.cos(ang)[None, :, None, :]
    s = jnp.sin(ang)[None, :, None, :]
    x1, x2 = x[..., :half], x[..., half:]
    return jnp.concatenate([x1 * c - x2 * s, x1 * s + x2 * c], axis=-1)


def sparse_attention_block(q, iq, iw, keys_k, keys_v, keys_i, q_pos, k_pos, topk):
    B, Tq = q.shape[:2]
    logits = jnp.einsum('bqhd,bld->bqhl', iq.astype(F32), keys_i.astype(F32)) * IDX_DIM ** -0.5
    score = jnp.einsum('bqh,bqhl->bql', iw.astype(F32), jax.nn.relu(logits))
    admissible = k_pos[None, :] <= q_pos[:, None]
    score = jnp.where(admissible[None], score, MASK_VALUE)
    top_score, idx = lax.top_k(score, topk)
    valid = top_score > 0.5 * MASK_VALUE
    gather = jax.vmap(lambda kv, ix: kv[ix])
    k_sel = gather(keys_k, idx).astype(F32)
    v_sel = gather(keys_v, idx).astype(F32)
    qg = q.astype(F32).reshape(B, Tq, A_KV_HEADS, A_HEADS // A_KV_HEADS, A_HEAD_DIM)
    s = jnp.einsum('bqgrd,bqkgd->bqgrk', qg, k_sel) * A_HEAD_DIM ** -0.5
    s = jnp.where(valid[:, :, None, None, :], s, MASK_VALUE)
    p = jax.nn.softmax(s, axis=-1)
    o = jnp.einsum('bqgrk,bqkgd->bqgrd', p, v_sel)
    return o.reshape(B, Tq, A_HEADS * A_HEAD_DIM)


def attention_prompt(q, iq, iw, k, v, ik):
    B, T = q.shape[:2]
    topk = min(TOPK_MAX, T // 4)
    pos = jnp.arange(T, dtype=jnp.int32)

    def block(i):
        s0 = i * Q_BLOCK
        sl = lambda a: lax.dynamic_slice_in_dim(a, s0, Q_BLOCK, axis=1)
        return sparse_attention_block(sl(q), sl(iq), sl(iw), k, v, ik,
                                      lax.dynamic_slice_in_dim(pos, s0, Q_BLOCK, axis=0), pos, topk)

    out = lax.map(block, jnp.arange(T // Q_BLOCK))
    return jnp.swapaxes(out, 0, 1).reshape(B, T, A_HEADS * A_HEAD_DIM)


def gather_pages(cache_l, page_table):
    pages = cache_l[page_table]
    return pages.reshape(page_table.shape[0], page_table.shape[1] * PAGE_SIZE, *cache_l.shape[2:])


def retention_chunk(S, q, k, v, log_gamma):
    C = q.shape[1]
    n = jnp.arange(C, dtype=F32)
    diff = n[:, None] - n[None, :]
    decay = jnp.where(diff >= 0, jnp.exp(log_gamma[:, None, None] * jnp.maximum(diff, 0.0)), 0.0)
    scores = jnp.einsum('bnhd,bmhd->bhnm', q, k) * decay[None]
    intra = jnp.einsum('bhnm,bmhe->bnhe', scores, v)
    q_decay = jnp.exp(log_gamma[None, :] * (n[:, None] + 1.0))
    cross = jnp.einsum('bnhd,bhde->bnhe', q * q_decay[None, :, :, None], S)
    k_decay = jnp.exp(log_gamma[None, :] * (C - 1.0 - n)[:, None])
    S_new = jnp.exp(log_gamma * C)[None, :, None, None] * S + \
        jnp.einsum('bmhd,bmhe->bhde', k * k_decay[None, :, :, None], v)
    return S_new, intra + cross


def retention_mixer(q, k, v, g, pos, S0):
    B, T, _ = q.shape
    q = rotary(q.astype(F32).reshape(B, T, R_HEADS, R_DK), pos)
    k = rotary(k.astype(F32).reshape(B, T, R_HEADS, R_DK), pos) * R_DK ** -0.5
    v = v.astype(F32).reshape(B, T, R_HEADS, R_DV)
    log_gamma = jnp.log1p(-jnp.exp2(-5.0 - jnp.arange(R_HEADS, dtype=F32)))
    step = functools.partial(retention_chunk, log_gamma=log_gamma)
    S, o = chunked_scan(step, S0.astype(F32), (q, k, v), pick_chunk(T, R_CHUNK))
    mu = jnp.mean(o, axis=-1, keepdims=True)
    var = jnp.mean(jnp.square(o - mu), axis=-1, keepdims=True)
    o = (o - mu) * lax.rsqrt(var + GN_EPS)
    return jax.nn.silu(g.astype(F32)) * o.reshape(B, T, R_HEADS * R_DV), S


def hgrn_lower_bounds(lb_param):
    p = jax.nn.softmax(lb_param.astype(F32), axis=0)
    return jnp.cumsum(p, axis=0) - p[0]


def gla_chunk(S, q, k, v, logf):
    C = q.shape[1]
    G = jnp.cumsum(logf, axis=1)
    mask = jnp.tril(jnp.ones((C, C), dtype=bool))
    diff = G[:, :, None] - G[:, None, :]
    decay = jnp.exp(jnp.where(mask[None, :, :, None, None], diff, MASK_VALUE))
    A = jnp.einsum('bnmhd,bmhd->bhnm', q[:, :, None] * decay, k)
    intra = jnp.einsum('bhnm,bmhe->bnhe', A, v)
    cross = jnp.einsum('bnhd,bhde->bnhe', q * jnp.exp(G), S)
    G_last = G[:, -1]
    S_new = jnp.exp(G_last)[..., None] * S + \
        jnp.einsum('bmhd,bmhe->bhde', k * jnp.exp(G[:, -1:] - G), v)
    return S_new, intra + cross


def hgrn2_mixer(q, f, i, g, lb, norm_g, S0):
    B, T, _ = q.shape
    q = jax.nn.silu(q.astype(F32)).reshape(B, T, H_HEADS, H_DK) * H_DK ** -0.5
    f = f.astype(F32)
    lb = lb.astype(F32)
    forget = lb + (1.0 - lb) * jax.nn.sigmoid(f)
    logf = jnp.log(jnp.maximum(forget, MIN_FORGET))
    k = (1.0 - lb) * jax.nn.sigmoid(-f)
    rs = lambda a, d: a.reshape(B, T, H_HEADS, d)
    S, o = chunked_scan(gla_chunk, S0.astype(F32),
                        (q, rs(k, H_DK), rs(i.astype(F32), H_DV), rs(logf, H_DK)), pick_chunk(T, H_CHUNK))
    o = rmsnorm(o, norm_g)
    return jax.nn.silu(g.astype(F32)) * o.reshape(B, T, H_HEADS * H_DV), S


def causal_conv(u, buf, w, b):
    T = u.shape[1]
    up = jnp.concatenate([buf, u], axis=1)
    y = b
    for j in range(M_CONV):
        y = y + up[:, j:j + T] * w[j]
    return jax.nn.silu(y), up[:, T:]


def ssd_chunk(h, x, dt, a, Bm, Cm):
    C = x.shape[1]
    cum = jnp.cumsum(a, axis=1)
    mask = jnp.tril(jnp.ones((C, C), dtype=bool))
    seg = jnp.exp(jnp.where(mask[None, :, :, None], cum[:, :, None, :] - cum[:, None, :, :], MASK_VALUE))
    CB = jnp.einsum('bnhs,bmhs->bnmh', Cm, Bm)
    intra = jnp.einsum('bnmh,bmhp->bnhp', CB * seg * dt[:, None, :, :], x)
    cross = jnp.einsum('bnhs,bhps->bnhp', Cm, h) * jnp.exp(cum)[..., None]
    w = jnp.exp(cum[:, -1:] - cum) * dt
    h_new = jnp.exp(cum[:, -1])[:, :, None, None] * h + jnp.einsum('bmhp,bmhs->bhps', x * w[..., None], Bm)
    return h_new, intra + cross


def mamba2_mixer(z, xbc, dt_raw, A_log, dt_bias, D_skip, conv_w, conv_b, norm_g, h0, conv_buf):
    B, T, _ = z.shape
    xbc, new_buf = causal_conv(xbc.astype(F32), conv_buf.astype(F32), conv_w.astype(F32), conv_b.astype(F32))
    x, Bm, Cm = jnp.split(xbc, [M_INNER, M_INNER + M_GROUPS * M_STATE], axis=-1)
    x = x.reshape(B, T, M_HEADS, M_HEAD_DIM)
    rep = M_HEADS // M_GROUPS
    Bm = jnp.repeat(Bm.reshape(B, T, M_GROUPS, M_STATE), rep, axis=2)
    Cm = jnp.repeat(Cm.reshape(B, T, M_GROUPS, M_STATE), rep, axis=2)
    dt = jax.nn.softplus(dt_raw.astype(F32) + dt_bias.astype(F32))
    a = -jnp.exp(A_log.astype(F32)) * dt
    h, y = chunked_scan(ssd_chunk, h0.astype(F32), (x, dt, a, Bm, Cm), pick_chunk(T, M_CHUNK))
    y = y + D_skip.astype(F32)[:, None] * x
    y = y.reshape(B, T, M_INNER) * jax.nn.silu(z.astype(F32))
    y = rmsnorm(y.reshape(B, T, M_GROUPS, M_INNER // M_GROUPS)).reshape(B, T, M_INNER) * norm_g.astype(F32)
    return y, h, new_buf


def trunk_layer(x, pos, lp, states, past):
    B, T, _ = x.shape
    dtype = x.dtype
    h = rmsnorm(x, lp['norm1'])
    (a_q, a_k, a_v, a_iq, a_ik, a_iw, r_q, r_k, r_v, r_g,
     c_q, c_f, c_i, c_g, m_z, m_xbc, m_dt, gates) = jnp.split(h @ lp['w_in'], SPLIT_POINTS, axis=-1)
    q = rmsnorm(a_q.reshape(B, T, A_HEADS, A_HEAD_DIM), lp['q_norm'])
    k = rmsnorm(a_k.reshape(B, T, A_KV_HEADS, A_HEAD_DIM), lp['k_norm'])
    v = a_v.reshape(B, T, A_KV_HEADS, A_HEAD_DIM)
    iq = a_iq.reshape(B, T, IDX_HEADS, IDX_DIM)
    iw = a_iw * IDX_HEADS ** -0.5
    if past is None:
        o_a = attention_prompt(q, iq, iw, k, v, a_ik)
    else:
        keys_k = jnp.concatenate([past[0].astype(dtype), k], axis=1)
        keys_v = jnp.concatenate([past[1].astype(dtype), v], axis=1)
        keys_i = jnp.concatenate([past[2].astype(dtype), a_ik], axis=1)
        L = keys_k.shape[1]
        o_a = sparse_attention_block(q, iq, iw, keys_k, keys_v, keys_i, pos,
                                     jnp.arange(L, dtype=jnp.int32), min(TOPK_MAX, L // 4))
    o_b, ret_new = retention_mixer(r_q, r_k, r_v, r_g, pos, states[0])
    o_c, hgrn_new = hgrn2_mixer(c_q, c_f, c_i, c_g, lp['lb'], lp['hgrn_norm'], states[1])
    o_d, ssm_new, conv_new = mamba2_mixer(m_z, m_xbc, m_dt, lp['A_log'], lp['dt_bias'], lp['D_skip'],
                                          lp['conv_w'], lp['conv_b'], lp['ssm_norm'], states[2], states[3])
    branches = jnp.stack([o_a, o_b, o_c, o_d], axis=2).astype(dtype)
    up = jnp.einsum('btnw,nwd->btnd', branches, lp['w_branch'])
    g = jax.nn.sigmoid(gates.reshape(B, T, N_BRANCH, D_MODEL).astype(F32))
    merged = jnp.sum(g * up.astype(F32), axis=2).astype(dtype)
    x = x + merged @ lp['w_out']
    ff_gate, ff_up = jnp.split(rmsnorm(x, lp['norm2']) @ lp['w_gu'], 2, axis=-1)
    x = x + (jax.nn.silu(ff_gate) * ff_up) @ lp['w_down']
    return x, (k, v, a_ik, ret_new, hgrn_new, ssm_new, conv_new)


def setup_inputs(seed: int = 0) -> dict:
    key = jax.random.key(seed)
    k = jax.random.split(key, 32)
    n_pages = PAST_LEN // PAGE_SIZE
    n_used = DEC_BATCH * n_pages
    n_pool = n_used + -(-n_used // 4)

    def nrm(kk, shape, scale=1.0):
        return scale * jax.random.normal(kk, shape, F32)

    def gain(kk, shape, scale=0.02):
        return 1.0 + scale * jax.random.normal(kk, shape, F32)

    x_prompt = nrm(k[0], (BATCH, SEQ, D_MODEL))
    x_sample = nrm(k[1], (DEC_BATCH, DEC_SEQ, D_MODEL))
    cache_k = nrm(k[2], (DEPTH, n_pool, PAGE_SIZE, A_KV_HEADS, A_HEAD_DIM))
    cache_v = nrm(k[3], (DEPTH, n_pool, PAGE_SIZE, A_KV_HEADS, A_HEAD_DIM))
    cache_kidx = nrm(k[4], (DEPTH, n_pool, PAGE_SIZE, IDX_DIM))
    state_ret = nrm(k[5], (DEPTH, DEC_BATCH, R_HEADS, R_DK, R_DV), 0.5)
    state_hgrn = nrm(k[6], (DEPTH, DEC_BATCH, H_HEADS, H_DK, H_DV), 0.5)
    state_ssm = nrm(k[7], (DEPTH, DEC_BATCH, M_HEADS, M_HEAD_DIM, M_STATE), 0.5)
    state_conv = nrm(k[8], (DEPTH, DEC_BATCH, M_CONV - 1, M_CONV_DIM))
    page_table = jax.random.permutation(k[9], n_pool)[:n_used].reshape(DEC_BATCH, n_pages).astype(jnp.int32)
    norm1_g = gain(k[10], (DEPTH, D_MODEL))
    w_in = nrm(k[11], (DEPTH, D_MODEL, D_IN), D_MODEL ** -0.5)
    q_norm_g = gain(k[12], (DEPTH, A_HEAD_DIM))
    k_norm_g = gain(k[13], (DEPTH, A_HEAD_DIM))
    lb_param = nrm(k[14], (DEPTH, H_HEADS * H_DK), 0.1)
    hgrn_norm_g = gain(k[15], (DEPTH, H_DV))
    A_log = jnp.log(jax.random.uniform(k[16], (DEPTH, M_HEADS), F32, 1.0, 16.0))
    dt0 = jnp.exp(jax.random.uniform(k[17], (DEPTH, M_HEADS), F32, math.log(1e-3), math.log(1e-1)))
    dt_bias = dt0 + jnp.log(-jnp.expm1(-dt0))
    D_skip = gain(k[18], (DEPTH, M_HEADS), 0.1)
    conv_w = nrm(k[19], (DEPTH, M_CONV, M_CONV_DIM), M_CONV ** -0.5)
    conv_b = nrm(k[20], (DEPTH, M_CONV_DIM), 0.02)
    ssm_norm_g = gain(k[21], (DEPTH, M_INNER))
    w_branch = nrm(k[22], (DEPTH, N_BRANCH, BRANCH_WIDTH, D_MODEL), BRANCH_WIDTH ** -0.5)
    w_out = nrm(k[23], (DEPTH, D_MODEL, D_MODEL), D_MODEL ** -0.5)
    norm2_g = gain(k[24], (DEPTH, D_MODEL))
    w_gu = nrm(k[25], (DEPTH, D_MODEL, 2 * D_FF), D_MODEL ** -0.5)
    w_down = nrm(k[26], (DEPTH, D_FF, D_MODEL), D_FF ** -0.5)
    return {'x_prompt': x_prompt, 'x_sample': x_sample, 'cache_k': cache_k, 'cache_v': cache_v,
            'cache_kidx': cache_kidx, 'state_ret': state_ret, 'state_hgrn': state_hgrn,
            'state_ssm': state_ssm, 'state_conv': state_conv, 'page_table': page_table,
            'norm1_g': norm1_g, 'w_in': w_in, 'q_norm_g': q_norm_g, 'k_norm_g': k_norm_g,
            'lb_param': lb_param, 'hgrn_norm_g': hgrn_norm_g, 'A_log': A_log, 'dt_bias': dt_bias,
            'D_skip': D_skip, 'conv_w': conv_w, 'conv_b': conv_b, 'ssm_norm_g': ssm_norm_g,
            'w_branch': w_branch, 'w_out': w_out, 'norm2_g': norm2_g, 'w_gu': w_gu, 'w_down': w_down}


def reference(x_prompt, x_sample, cache_k, cache_v, cache_kidx, state_ret, state_hgrn, state_ssm,
              state_conv, page_table, norm1_g, w_in, q_norm_g, k_norm_g, lb_param, hgrn_norm_g,
              A_log, dt_bias, D_skip, conv_w, conv_b, ssm_norm_g, w_branch, w_out, norm2_g, w_gu, w_down):
    B, T = x_prompt.shape[:2]
    TS = x_sample.shape[1]
    past_len = page_table.shape[1] * PAGE_SIZE
    pos_p = jnp.arange(T, dtype=jnp.int32)
    pos_s = past_len + jnp.arange(TS, dtype=jnp.int32)
    lower_bounds = hgrn_lower_bounds(lb_param)
    zero_states = (jnp.zeros((B, R_HEADS, R_DK, R_DV), F32),
                   jnp.zeros((B, H_HEADS, H_DK, H_DV), F32),
                   jnp.zeros((B, M_HEADS, M_HEAD_DIM, M_STATE), F32),
                   jnp.zeros((B, M_CONV - 1, M_CONV_DIM), F32))
    out_p = [[] for _ in range(7)]
    out_s = [[] for _ in range(7)]
    xp, xs = x_prompt, x_sample
    for l in range(DEPTH):
        lp = dict(norm1=norm1_g[l], w_in=w_in[l], q_norm=q_norm_g[l], k_norm=k_norm_g[l],
                  lb=lower_bounds[l], hgrn_norm=hgrn_norm_g[l], A_log=A_log[l], dt_bias=dt_bias[l],
                  D_skip=D_skip[l], conv_w=conv_w[l], conv_b=conv_b[l], ssm_norm=ssm_norm_g[l],
                  w_branch=w_branch[l], w_out=w_out[l], norm2=norm2_g[l], w_gu=w_gu[l], w_down=w_down[l])
        xp, new_p = trunk_layer(xp, pos_p, lp, zero_states, None)
        past = (gather_pages(cache_k[l], page_table), gather_pages(cache_v[l], page_table),
                gather_pages(cache_kidx[l], page_table))
        st = (state_ret[l], state_hgrn[l], state_ssm[l], state_conv[l])
        xs, new_s = trunk_layer(xs, pos_s, lp, st, past)
        for i in range(7):
            out_p[i].append(new_p[i])
            out_s[i].append(new_s[i])
    dtypes = (cache_k.dtype, cache_v.dtype, cache_kidx.dtype, state_ret.dtype,
              state_hgrn.dtype, state_ssm.dtype, state_conv.dtype)
    k_p, v_p, kidx_p, ret_p, hgrn_p, ssm_p, conv_p = [jnp.stack(a).astype(d) for a, d in zip(out_p, dtypes)]
    k_s, v_s, kidx_s, ret_s, hgrn_s, ssm_s, conv_s = [jnp.stack(a).astype(d) for a, d in zip(out_s, dtypes)]
    y_prompt = xp
    y_sample = xs
    return (y_prompt, y_sample, k_p, v_p, kidx_p, ret_p, hgrn_p, ssm_p, conv_p,
            k_s, v_s, kidx_s, ret_s, hgrn_s, ssm_s, conv_s)
```

```python
import functools
import math
import struct

import jax
import jax.numpy as jnp
from jax import lax
from jax.experimental import pallas as pl
from jax.experimental.pallas import tpu as pltpu

F32 = jnp.float32
BF16 = jnp.bfloat16
I32 = jnp.int32

D_MODEL = 1024
PAGE_SIZE = 128
BRANCH_WIDTH = D_MODEL // 2
N_BRANCH = 4
A_HEAD_DIM = 64
A_HEADS = BRANCH_WIDTH // A_HEAD_DIM
A_KV_HEADS = A_HEADS // 2
IDX_HEADS = 8
IDX_DIM = 64
TOPK_MAX = 256
Q_BLOCK = 128
R_HEADS = 4
R_DK = BRANCH_WIDTH // R_HEADS
R_DV = BRANCH_WIDTH // R_HEADS
R_CHUNK = 128
ROPE_BASE = 10000.0
H_HEADS = 4
H_DK = 128
H_DV = BRANCH_WIDTH // H_HEADS
H_CHUNK = 64
MIN_FORGET = 1e-30
M_HEAD_DIM = 64
M_HEADS = BRANCH_WIDTH // M_HEAD_DIM
M_INNER = M_HEADS * M_HEAD_DIM
M_GROUPS = 2
M_STATE = 128
M_CONV = 4
M_CHUNK = 128
M_CONV_DIM = M_INNER + 2 * M_GROUPS * M_STATE
D_FF = -(-8 * D_MODEL // (3 * 256)) * 256
RMS_EPS = 1e-6
GN_EPS = 1e-6
MASK_VALUE = -1e30

SPLIT_SIZES = (
    A_HEADS * A_HEAD_DIM, A_KV_HEADS * A_HEAD_DIM, A_KV_HEADS * A_HEAD_DIM,
    IDX_HEADS * IDX_DIM, IDX_DIM, IDX_HEADS,
    R_HEADS * R_DK, R_HEADS * R_DK, R_HEADS * R_DV, R_HEADS * R_DV,
    H_HEADS * H_DK, H_HEADS * H_DK, H_HEADS * H_DV, H_HEADS * H_DV,
    M_INNER, M_CONV_DIM, M_HEADS,
    N_BRANCH * D_MODEL,
)
SPLIT_POINTS = tuple(sum(SPLIT_SIZES[:i + 1]) for i in range(len(SPLIT_SIZES) - 1))

LANES = 128
SUBLANES = 8
VMEM_LIMIT = 56 * 1024 * 1024

C_GATES = 0
C_XBC = 4096
C_AQ = 5120
C_IQ = 5632
C_RQ = 6144
C_RK = 6656
C_RV = 7168
C_RG = 7680
C_CQ = 8192
C_CF = 8704
C_CI = 9216
C_CG = 9728
C_MZ = 10240
C_AK = 10752
C_AV = 11008
C_IKW = 11264
C_DT = 11392
NP = 11520
W = BRANCH_WIDTH


def _float_key(x):
    b = struct.unpack("<i", struct.pack("<f", x))[0]
    return b if b >= 0 else b ^ 0x7FFFFFFF


VALID_KEY = _float_key(0.5 * MASK_VALUE)
INT_MIN = -2 ** 31


def _cparams(*sem):
    return pltpu.CompilerParams(dimension_semantics=sem, vmem_limit_bytes=VMEM_LIMIT)


def _nt(a, b):
    return lax.dot_general(a, b, (((1,), (1,)), ((), ())), preferred_element_type=F32)


def _tn(a, b):
    return lax.dot_general(a, b, (((0,), (0,)), ((), ())), preferred_element_type=F32)


def _dot(a, b):
    return jnp.dot(a, b, preferred_element_type=F32)


def _dot_exact_lhs(a01, x):
    a = a01.astype(BF16)
    x0 = x.astype(BF16)
    r1 = x - x0.astype(F32)
    x1 = r1.astype(BF16)
    x2 = (r1 - x1.astype(F32)).astype(BF16)
    return _dot(a, x0) + _dot(a, x1) + _dot(a, x2)


def _dot_exact_rhs(x, b01):
    b = b01.astype(BF16)
    x0 = x.astype(BF16)
    r1 = x - x0.astype(F32)
    x1 = r1.astype(BF16)
    x2 = (r1 - x1.astype(F32)).astype(BF16)
    return _dot(x0, b) + _dot(x1, b) + _dot(x2, b)


def _sigmoid(x):
    return 1.0 / (1.0 + jnp.exp(-x))


def _silu(x):
    return x * _sigmoid(x)


def _iota(shape, dim):
    return lax.broadcasted_iota(I32, shape, dim)


def _tri_incl(n):
    return (_iota((n, n), 0) >= _iota((n, n), 1)).astype(F32)


def _inproj_kernel(x_ref, g_ref, w_ref, o_ref, h_ref):
    @pl.when(pl.program_id(1) == 0)
    def _():
        x = x_ref[...]
        ms = jnp.mean(x * x, axis=-1, keepdims=True)
        h_ref[...] = (x * lax.rsqrt(ms + RMS_EPS) * g_ref[...]).astype(BF16)

    o_ref[...] = _dot(h_ref[...], w_ref[...])


def _inproj(x, g, w, tm, tn):
    n = x.shape[0]
    return pl.pallas_call(
        _inproj_kernel,
        grid=(n // tm, NP // tn),
        in_specs=[pl.BlockSpec((tm, D_MODEL), lambda i, j: (i, 0)),
                  pl.BlockSpec((1, D_MODEL), lambda i, j: (0, 0)),
                  pl.BlockSpec((D_MODEL, tn), lambda i, j: (0, j))],
        out_specs=pl.BlockSpec((tm, tn), lambda i, j: (i, j)),
        out_shape=jax.ShapeDtypeStruct((n, NP), F32),
        scratch_shapes=[pltpu.VMEM((tm, D_MODEL), BF16)],
        compiler_params=_cparams("parallel", "arbitrary"),
        name="inproj",
    )(x, g, w)


def _head_rmsnorm(x, gain_row):
    pair = (_iota((LANES, LANES), 0) // A_HEAD_DIM == _iota((LANES, LANES), 1) // A_HEAD_DIM).astype(F32)
    outs = []
    for c in range(x.shape[1] // LANES):
        xc = x[:, c * LANES:(c + 1) * LANES]
        ms = _dot_exact_rhs(xc * xc, pair) * (1.0 / A_HEAD_DIM)
        outs.append(xc * lax.rsqrt(ms + RMS_EPS) * gain_row)
    return outs


def _kth_largest_key(count_ge, k, shape):
    nonneg = count_ge(jnp.zeros(shape, I32)) >= k
    prefix = jnp.where(nonneg, 0, INT_MIN).astype(I32)

    def body(t, prefix):
        cand = prefix + jnp.left_shift(jnp.int32(1), 30 - t)
        return jnp.where(count_ge(cand) >= k, cand, prefix)

    return lax.fori_loop(0, 31, body, prefix)


def _to_key(s):
    b = lax.bitcast_convert_type(s, I32)
    return jnp.where(b >= 0, b, b ^ 0x7FFFFFFF)


def _kprep_kernel(ak_ref, av_ref, ikw_ref, kg_ref, kn_ref, v_ref, ik_ref, vt_ref):
    kn = _head_rmsnorm(ak_ref[...], kg_ref[...])
    for c, t in enumerate(kn):
        kn_ref[:, c * LANES:(c + 1) * LANES] = t
    v = av_ref[...]
    v_ref[...] = v
    ik_ref[...] = ikw_ref[:, :IDX_DIM]
    vt_ref[0] = v.T.astype(BF16)


def _kprep(p, k_gain):
    n = p.shape[0]
    nb = n // Q_BLOCK
    kvw = A_KV_HEADS * A_HEAD_DIM
    return pl.pallas_call(
        _kprep_kernel,
        grid=(nb,),
        in_specs=[pl.BlockSpec((Q_BLOCK, kvw), lambda i: (i, C_AK // kvw)),
                  pl.BlockSpec((Q_BLOCK, kvw), lambda i: (i, C_AV // kvw)),
                  pl.BlockSpec((Q_BLOCK, LANES), lambda i: (i, C_IKW // LANES)),
                  pl.BlockSpec((1, LANES), lambda i: (0, 0))],
        out_specs=[pl.BlockSpec((Q_BLOCK, kvw), lambda i: (i, 0)),
                   pl.BlockSpec((Q_BLOCK, kvw), lambda i: (i, 0)),
                   pl.BlockSpec((Q_BLOCK, IDX_DIM), lambda i: (i, 0)),
                   pl.BlockSpec((1, kvw, Q_BLOCK), lambda i: (i, 0, 0))],
        out_shape=[jax.ShapeDtypeStruct((n, kvw), F32),
                   jax.ShapeDtypeStruct((n, kvw), F32),
                   jax.ShapeDtypeStruct((n, IDX_DIM), F32),
                   jax.ShapeDtypeStruct((nb, kvw, Q_BLOCK), BF16)],
        compiler_params=_cparams("parallel"),
        name="kprep",
    )(p, p, p, k_gain)


def _attn_prompt_kernel(aq_ref, iq_ref, ikw_ref, qg_ref, kn_ref, ik_ref, vt_ref, o_ref,
                        sc_ref, key_ref, cut_ref, *, topk):
    i = pl.program_id(1)
    nblk = i + 1
    q_iota0 = _iota((Q_BLOCK, Q_BLOCK), 0)
    q_iota1 = _iota((Q_BLOCK, Q_BLOCK), 1)

    qn = _head_rmsnorm(aq_ref[...], qg_ref[...])
    qnT = [(t * (A_HEAD_DIM ** -0.5)).T.astype(BF16) for t in qn]
    iqT = [(iq_ref[:, c * LANES:(c + 1) * LANES] * (IDX_DIM ** -0.5)).T.astype(BF16)
           for c in range(IDX_HEADS * IDX_DIM // LANES)]
    iwT = ikw_ref[...].T[IDX_DIM:IDX_DIM + IDX_HEADS, :] * (IDX_HEADS ** -0.5)

    def score_blk(j, carry):
        ks = pl.multiple_of(j * Q_BLOCK, Q_BLOCK)
        ik = ik_ref[pl.ds(ks, Q_BLOCK), :].astype(BF16)
        acc = jnp.zeros((Q_BLOCK, Q_BLOCK), F32)
        for h in range(IDX_HEADS):
            rhs = iqT[h // 2][(h % 2) * IDX_DIM:(h % 2 + 1) * IDX_DIM, :]
            acc = acc + jnp.maximum(_dot(ik, rhs), 0.0) * iwT[h:h + 1, :]
        adm = (ks + q_iota0) <= (i * Q_BLOCK + q_iota1)
        sm = jnp.where(adm, acc, MASK_VALUE)
        key_ref[pl.ds(ks, Q_BLOCK), :] = _to_key(sm)
        return carry

    lax.fori_loop(0, nblk, score_blk, 0)

    def count_cmp(cand, strict):
        def body(j, c):
            ks = pl.multiple_of(j * Q_BLOCK, Q_BLOCK)
            blk = key_ref[pl.ds(ks, Q_BLOCK), :]
            hit = (blk > cand) if strict else (blk >= cand)
            return c + jnp.where(hit, 1, 0)
        c = lax.fori_loop(0, nblk, body, jnp.zeros((Q_BLOCK, Q_BLOCK), I32))
        return jnp.sum(c, axis=0, keepdims=True)

    cut_ref[0:1, :] = jnp.full((1, Q_BLOCK), INT_MIN, I32)
    cut_ref[1:2, :] = jnp.full((1, Q_BLOCK), 2 ** 30, I32)

    @pl.when(nblk * Q_BLOCK > topk)
    def _():
        thr = _kth_largest_key(lambda cand: count_cmp(cand, False), topk, (1, Q_BLOCK))
        cut_ref[0:1, :] = thr
        n_gt = count_cmp(thr, True)
        n_eq = count_cmp(thr, False) - n_gt
        need = topk - n_gt

        @pl.when(jnp.max(n_eq - need) > 0)
        def _():
            def count_eq_upto(x):
                def body(j, c):
                    ks = pl.multiple_of(j * Q_BLOCK, Q_BLOCK)
                    blk = key_ref[pl.ds(ks, Q_BLOCK), :]
                    hit = (blk == thr) & ((ks + q_iota0) <= x)
                    return c + jnp.where(hit, 1, 0)
                c = lax.fori_loop(0, nblk, body, jnp.zeros((Q_BLOCK, Q_BLOCK), I32))
                return jnp.sum(c, axis=0, keepdims=True)

            def body(t, ans):
                cand = ans + jnp.left_shift(jnp.int32(1), 15 - t)
                return jnp.where(count_eq_upto(cand - 1) < need, cand, ans)

            cut_ref[1:2, :] = lax.fori_loop(0, 16, body, jnp.zeros((1, Q_BLOCK), I32))

    thr = cut_ref[0:1, :]
    last = cut_ref[1:2, :]
    lo = jnp.maximum(thr, VALID_KEY)

    def mask_blk(j, carry):
        ks = pl.multiple_of(j * Q_BLOCK, Q_BLOCK)
        blk = key_ref[pl.ds(ks, Q_BLOCK), :]
        sel = (blk > lo) | ((blk == thr) & (thr > VALID_KEY) & ((ks + q_iota0) <= last))
        sc_ref[pl.ds(ks, Q_BLOCK), :] = jnp.where(sel, 0.0, MASK_VALUE)
        return carry

    lax.fori_loop(0, nblk, mask_blk, 0)

    zeros_half = jnp.zeros((A_HEAD_DIM, Q_BLOCK), BF16)
    outs = []
    for h in range(A_HEADS):
        g = h // 2
        pair = g // 2
        qh = qnT[h // 2][(h % 2) * A_HEAD_DIM:(h % 2 + 1) * A_HEAD_DIM, :]
        qpad = jnp.concatenate([qh, zeros_half] if g % 2 == 0 else [zeros_half, qh], axis=0)

        def att_blk(j, carry, qpad=qpad, pair=pair, g=g):
            m, l, acc = carry
            ks = pl.multiple_of(j * Q_BLOCK, Q_BLOCK)
            kb = kn_ref[pl.ds(ks, Q_BLOCK), pair * LANES:(pair + 1) * LANES].astype(BF16)
            s = _dot(kb, qpad) + sc_ref[pl.ds(ks, Q_BLOCK), :]
            m_new = jnp.maximum(m, jnp.max(s, axis=0, keepdims=True))
            alpha = jnp.exp(m - m_new)
            e = jnp.exp(s - m_new)
            l = alpha * l + jnp.sum(e, axis=0, keepdims=True)
            vt = vt_ref[j, g * A_HEAD_DIM:(g + 1) * A_HEAD_DIM, :]
            acc = alpha * acc + _dot(vt, e.astype(BF16))
            return m_new, l, acc

        m0 = jnp.full((1, Q_BLOCK), MASK_VALUE, F32)
        l0 = jnp.zeros((1, Q_BLOCK), F32)
        a0 = jnp.zeros((A_HEAD_DIM, Q_BLOCK), F32)
        m, l, acc = lax.fori_loop(0, nblk, att_blk, (m0, l0, a0))
        outs.append(acc / l)
    oT = jnp.concatenate(outs, axis=0)
    o_ref[...] = oT.T.astype(o_ref.dtype)


def _attn_prompt(p, kn, ik, vt, q_gain, batch, seq):
    nqb = seq // Q_BLOCK
    kvw = A_KV_HEADS * A_HEAD_DIM
    topk = min(TOPK_MAX, seq // 4)
    return pl.pallas_call(
        functools.partial(_attn_prompt_kernel, topk=topk),
        grid=(batch, nqb),
        in_specs=[pl.BlockSpec((Q_BLOCK, W), lambda b, i: (b * nqb + i, C_AQ // W)),
                  pl.BlockSpec((Q_BLOCK, W), lambda b, i: (b * nqb + i, C_IQ // W)),
                  pl.BlockSpec((Q_BLOCK, LANES), lambda b, i: (b * nqb + i, C_IKW // LANES)),
                  pl.BlockSpec((1, LANES), lambda b, i: (0, 0)),
                  pl.BlockSpec((seq, kvw), lambda b, i: (b, 0)),
                  pl.BlockSpec((seq, IDX_DIM), lambda b, i: (b, 0)),
                  pl.BlockSpec((nqb, kvw, Q_BLOCK), lambda b, i: (b, 0, 0))],
        out_specs=pl.BlockSpec((Q_BLOCK, W), lambda b, i: (b * nqb + i, 0)),
        out_shape=jax.ShapeDtypeStruct((batch * seq, W), BF16),
        scratch_shapes=[pltpu.VMEM((seq, Q_BLOCK), F32),
                        pltpu.VMEM((seq, Q_BLOCK), I32),
                        pltpu.VMEM((SUBLANES, Q_BLOCK), I32)],
        compiler_params=_cparams("parallel", "arbitrary"),
        name="attn_prompt",
    )(p, p, p, q_gain, kn, ik, vt)


def _ret_kernel(q_ref, k_ref, v_ref, g_ref, cos_ref, sin_ref, s0_ref, o_ref, s_out_ref, st_ref,
                *, cp, c_true):
    tb = pl.program_id(1)

    @pl.when(tb == 0)
    def _():
        st_ref[...] = s0_ref[0]

    row = _iota((cp, 1), 0).astype(F32)
    diff = (_iota((cp, cp), 0) - _iota((cp, cp), 1)).astype(F32)

    def chunk(c, carry):
        r0 = pl.multiple_of(c * cp, cp)
        rows = pl.ds(r0, cp)
        cos = cos_ref[rows, :]
        sin = sin_ref[rows, :]
        for h in range(R_HEADS):
            lg = math.log1p(-2.0 ** (-5.0 - h))
            cols = slice(h * R_DK, (h + 1) * R_DK)
            q = q_ref[rows, cols]
            k = k_ref[rows, cols]
            v = v_ref[rows, cols]
            q = q * cos + pltpu.roll(q, R_DK // 2, 1) * sin
            k = (k * cos + pltpu.roll(k, R_DK // 2, 1) * sin) * (R_DK ** -0.5)
            decay = jnp.where(diff >= 0, jnp.exp(lg * jnp.maximum(diff, 0.0)), 0.0)
            scores = _nt(q.astype(BF16), k.astype(BF16)) * decay
            s = st_ref[h]
            o = _dot(scores.astype(BF16), v.astype(BF16))
            o = o + _dot((q * jnp.exp(lg * (row + 1.0))).astype(BF16), s.astype(BF16))
            kd = jnp.where(row < c_true, jnp.exp(lg * jnp.maximum(c_true - 1.0 - row, 0.0)), 0.0)
            st_ref[h] = math.exp(lg * c_true) * s + _tn((k * kd).astype(BF16), v.astype(BF16))
            mu = jnp.mean(o, axis=-1, keepdims=True)
            d = o - mu
            var = jnp.mean(d * d, axis=-1, keepdims=True)
            o_ref[rows, cols] = (_silu(g_ref[rows, cols]) * (d * lax.rsqrt(var + GN_EPS))).astype(o_ref.dtype)
        return carry

    lax.fori_loop(0, q_ref.shape[0] // cp, chunk, 0)

    @pl.when(tb == pl.num_programs(1) - 1)
    def _():
        s_out_ref[0] = st_ref[...]


def _hgrn_kernel(q_ref, f_ref, i_ref, g_ref, lb_ref, ng_ref, s0_ref, o_ref, s_out_ref, st_ref,
                 *, cp, c_true, sb):
    tb = pl.program_id(1)

    @pl.when(tb == 0)
    def _():
        st_ref[...] = s0_ref[0]

    tri = _tri_incl(cp)
    rowi = _iota((cp, 1), 0)
    real = rowi < c_true
    lane_a = _iota((sb, cp), 1)
    sub_n = _iota((sb, 1), 0)
    ng = ng_ref[...]

    def chunk(c, carry):
        r0 = pl.multiple_of(c * cp, cp)
        rows = pl.ds(r0, cp)
        for h in range(H_HEADS):
            cols = slice(h * H_DK, (h + 1) * H_DK)
            lb = lb_ref[:, cols]
            q = _silu(q_ref[rows, cols]) * (H_DK ** -0.5)
            f = f_ref[rows, cols]
            forget = lb + (1.0 - lb) * _sigmoid(f)
            logf = jnp.where(real, jnp.log(jnp.maximum(forget, MIN_FORGET)), 0.0)
            k = jnp.where(real, (1.0 - lb) * _sigmoid(-f), 0.0)
            v = jnp.where(real, i_ref[rows, cols], 0.0)
            vb = v.astype(BF16)
            gc = _dot_exact_lhs(tri, logf)
            s = st_ref[h]
            cross = _dot((q * jnp.exp(gc)).astype(BF16), s.astype(BF16))
            g_last = gc[cp - 1:cp, :]
            kd = k * jnp.exp(g_last - gc)
            eg_col = jnp.broadcast_to(jnp.exp(g_last), (SUBLANES, H_DK)).T[:, 0:1]
            st_ref[h] = eg_col * s + _tn(kd.astype(BF16), vb)
            outs = []
            for b in range(cp // sb):
                b0 = b * sb
                gi = gc[b0:b0 + sb, :]
                qi = q[b0:b0 + sb, :]
                ki = k[b0:b0 + sb, :]
                if b > 0:
                    gref = gc[b0:b0 + 1, :]
                    ks = jnp.where(rowi < b0, k * jnp.exp(jnp.minimum(gref - gc, 0.0)), 0.0)
                    a = _nt((qi * jnp.exp(gi - gref)).astype(BF16), ks.astype(BF16))
                else:
                    a = jnp.zeros((sb, cp), F32)
                for m in range(sb):
                    e = jnp.exp(jnp.where(sub_n >= m, gi - gi[m:m + 1, :], MASK_VALUE))
                    col = jnp.sum(qi * e * ki[m:m + 1, :], axis=1, keepdims=True)
                    a = jnp.where(lane_a == b0 + m, col, a)
                outs.append(_dot(a.astype(BF16), vb) + cross[b0:b0 + sb, :])
            o = outs[0] if len(outs) == 1 else jnp.concatenate(outs, axis=0)
            ms = jnp.mean(o * o, axis=-1, keepdims=True)
            o = o * lax.rsqrt(ms + RMS_EPS) * ng
            o_ref[rows, cols] = (_silu(g_ref[rows, cols]) * o).astype(o_ref.dtype)
        return carry

    lax.fori_loop(0, q_ref.shape[0] // cp, chunk, 0)

    @pl.when(tb == pl.num_programs(1) - 1)
    def _():
        s_out_ref[0] = st_ref[...]


def _softplus(x):
    return jnp.maximum(x, 0.0) + jnp.log1p(jnp.exp(-jnp.abs(x)))


def _ssd_kernel(z_ref, xbc_ref, dt_ref, cw_ref, cb_ref, alog_ref, dtb_ref, dskip_ref, ng_ref,
                h0_ref, buf0_ref, o_ref, h_out_ref, buf_out_ref, st_ref, tail_ref, ext_ref,
                *, cp, c_true):
    tb = pl.program_id(1)

    @pl.when(tb == 0)
    def _():
        st_ref[...] = h0_ref[0]
        tail_ref[...] = buf0_ref[0]

    tri = _tri_incl(cp)
    rowi = _iota((cp, 1), 0)
    real = rowi < c_true
    causal = _iota((cp, cp), 0) >= _iota((cp, cp), 1)
    lane = _iota((1, LANES), 1)
    left = lane < M_HEAD_DIM
    top = _iota((LANES, 1), 0) < M_HEAD_DIM
    cw = cw_ref[...]
    cb = cb_ref[...]
    neg_a = -jnp.exp(alog_ref[...])
    gw = M_INNER // M_GROUPS

    def chunk(c, carry):
        r0 = pl.multiple_of(c * cp, cp)
        rows = pl.ds(r0, cp)
        u = xbc_ref[rows, :]
        ext_ref[0:SUBLANES, :] = tail_ref[...]
        ext_ref[SUBLANES:SUBLANES + cp, :] = u
        y = cb + cw[M_CONV - 1:M_CONV, :] * u
        for j in range(M_CONV - 1):
            y = y + cw[j:j + 1, :] * ext_ref[pl.ds(SUBLANES - (M_CONV - 1) + j, cp), :]
        tail_ref[...] = ext_ref[pl.ds(c_true, SUBLANES), :]
        xbc = _silu(y)
        dt = jnp.where(real, _softplus(dt_ref[rows, :] + dtb_ref[...]), 0.0)
        a = neg_a * dt
        cum = _dot_exact_lhs(tri, a)
        cum_t = cum.T
        dt_t = dt.T
        e_cum = jnp.exp(cum)
        cum_last = cum[cp - 1:cp, :]
        w_all = jnp.exp(cum_last - cum) * dt
        e_last = jnp.exp(cum_last)
        z = z_ref[rows, :]
        ys = []
        for pr in range(M_HEADS // 2):
            g = (2 * pr) // (M_HEADS // M_GROUPS)
            bm = xbc[:, M_INNER + g * M_STATE:M_INNER + (g + 1) * M_STATE].astype(BF16)
            cm = xbc[:, M_INNER + (M_GROUPS + g) * M_STATE:M_INNER + (M_GROUPS + g + 1) * M_STATE].astype(BF16)
            xp = xbc[:, pr * LANES:(pr + 1) * LANES]
            xpb = xp.astype(BF16)
            cbm = _nt(cm, bm)
            hp = st_ref[pr]
            cross = _nt(cm, hp.astype(BF16))
            intra = []
            for hh in range(2):
                h = 2 * pr + hh
                seg = jnp.exp(jnp.where(causal, cum[:, h:h + 1] - cum_t[h:h + 1, :], MASK_VALUE))
                mat = cbm * seg * dt_t[h:h + 1, :]
                intra.append(_dot(mat.astype(BF16), xpb))
            h0, h1 = 2 * pr, 2 * pr + 1
            y_pair = (jnp.where(left, intra[0], intra[1])
                      + cross * jnp.where(left, e_cum[:, h0:h0 + 1], e_cum[:, h1:h1 + 1]))
            wx = xp * jnp.where(left, w_all[:, h0:h0 + 1], w_all[:, h1:h1 + 1])
            decay = jnp.where(top, e_last[:, h0:h0 + 1], e_last[:, h1:h1 + 1])
            st_ref[pr] = decay * hp + _tn(wx.astype(BF16), bm)
            y_pair = y_pair + dskip_ref[:, pr * LANES:(pr + 1) * LANES] * xp
            ys.append(y_pair * _silu(z[:, pr * LANES:(pr + 1) * LANES]))
        for g in range(M_GROUPS):
            yg = jnp.concatenate(ys[g * 2:(g + 1) * 2], axis=1)
            ms = jnp.mean(yg * yg, axis=-1, keepdims=True)
            o_ref[rows, g * gw:(g + 1) * gw] = (yg * lax.rsqrt(ms + RMS_EPS)
                                                * ng_ref[:, g * gw:(g + 1) * gw]).astype(o_ref.dtype)
        return carry

    lax.fori_loop(0, z_ref.shape[0] // cp, chunk, 0)

    @pl.when(tb == pl.num_programs(1) - 1)
    def _():
        h_out_ref[0] = st_ref[...]
        buf_out_ref[0] = tail_ref[...]


def _row_blocks(batch, rows_per_seq, tb):
    n_tb = rows_per_seq // tb
    return n_tb, (lambda col: (lambda b, t: (b * n_tb + t, col)))


def _retention(p, cos2, sin2, s0, batch, rows_per_seq, tb, cp, c_true):
    n_tb, at = _row_blocks(batch, rows_per_seq, tb)
    st_spec = pl.BlockSpec((1, R_HEADS, R_DK, R_DV), lambda b, t: (b, 0, 0, 0))
    return pl.pallas_call(
        functools.partial(_ret_kernel, cp=cp, c_true=c_true),
        grid=(batch, n_tb),
        in_specs=[pl.BlockSpec((tb, W), at(C_RQ // W)), pl.BlockSpec((tb, W), at(C_RK // W)),
                  pl.BlockSpec((tb, W), at(C_RV // W)), pl.BlockSpec((tb, W), at(C_RG // W)),
                  pl.BlockSpec((tb, R_DK), lambda b, t: (t, 0)), pl.BlockSpec((tb, R_DK), lambda b, t: (t, 0)),
                  st_spec],
        out_specs=[pl.BlockSpec((tb, W), at(0)), st_spec],
        out_shape=[jax.ShapeDtypeStruct((batch * rows_per_seq, W), BF16),
                   jax.ShapeDtypeStruct((batch, R_HEADS, R_DK, R_DV), F32)],
        scratch_shapes=[pltpu.VMEM((R_HEADS, R_DK, R_DV), F32)],
        compiler_params=_cparams("parallel", "arbitrary"),
        name="retention",
    )(p, p, p, p, cos2, sin2, s0)


def _hgrn2(p, lb, norm_g, s0, batch, rows_per_seq, tb, cp, c_true, sb):
    n_tb, at = _row_blocks(batch, rows_per_seq, tb)
    st_spec = pl.BlockSpec((1, H_HEADS, H_DK, H_DV), lambda b, t: (b, 0, 0, 0))
    return pl.pallas_call(
        functools.partial(_hgrn_kernel, cp=cp, c_true=c_true, sb=sb),
        grid=(batch, n_tb),
        in_specs=[pl.BlockSpec((tb, W), at(C_CQ // W)), pl.BlockSpec((tb, W), at(C_CF // W)),
                  pl.BlockSpec((tb, W), at(C_CI // W)), pl.BlockSpec((tb, W), at(C_CG // W)),
                  pl.BlockSpec((1, W), lambda b, t: (0, 0)), pl.BlockSpec((1, H_DV), lambda b, t: (0, 0)),
                  st_spec],
        out_specs=[pl.BlockSpec((tb, W), at(0)), st_spec],
        out_shape=[jax.ShapeDtypeStruct((batch * rows_per_seq, W), BF16),
                   jax.ShapeDtypeStruct((batch, H_HEADS, H_DK, H_DV), F32)],
        scratch_shapes=[pltpu.VMEM((H_HEADS, H_DK, H_DV), F32)],
        compiler_params=_cparams("parallel", "arbitrary"),
        name="hgrn2",
    )(p, p, p, p, lb, norm_g, s0)


def _ssd(p, conv_w, conv_b, a_log, dt_bias, d_skip, norm_g, h0, buf0, batch, rows_per_seq, tb, cp, c_true):
    n_tb, at = _row_blocks(batch, rows_per_seq, tb)
    npair = M_HEADS // 2
    st_spec = pl.BlockSpec((1, npair, 2 * M_HEAD_DIM, M_STATE), lambda b, t: (b, 0, 0, 0))
    buf_spec = pl.BlockSpec((1, SUBLANES, M_CONV_DIM), lambda b, t: (b, 0, 0))
    row = lambda wdt: pl.BlockSpec((1, wdt), lambda b, t: (0, 0))
    return pl.pallas_call(
        functools.partial(_ssd_kernel, cp=cp, c_true=c_true),
        grid=(batch, n_tb),
        in_specs=[pl.BlockSpec((tb, W), at(C_MZ // W)), pl.BlockSpec((tb, M_CONV_DIM), at(C_XBC // M_CONV_DIM)),
                  pl.BlockSpec((tb, LANES), at(C_DT // LANES)),
                  pl.BlockSpec((M_CONV, M_CONV_DIM), lambda b, t: (0, 0)), row(M_CONV_DIM),
                  row(LANES), row(LANES), row(W), row(W), st_spec, buf_spec],
        out_specs=[pl.BlockSpec((tb, W), at(0)), st_spec, buf_spec],
        out_shape=[jax.ShapeDtypeStruct((batch * rows_per_seq, W), BF16),
                   jax.ShapeDtypeStruct((batch, npair, 2 * M_HEAD_DIM, M_STATE), F32),
                   jax.ShapeDtypeStruct((batch, SUBLANES, M_CONV_DIM), F32)],
        scratch_shapes=[pltpu.VMEM((npair, 2 * M_HEAD_DIM, M_STATE), F32),
                        pltpu.VMEM((SUBLANES, M_CONV_DIM), F32),
                        pltpu.VMEM((cp + SUBLANES, M_CONV_DIM), F32)],
        compiler_params=_cparams("parallel", "arbitrary"),
        name="ssd",
    )(p, p, p, conv_w, conv_b, a_log, dt_bias, d_skip, norm_g, h0, buf0)


def _idx_sample_kernel(pt_ref, iq_ref, iw_ref, iknew_ref, *rest, pp, n_pages, topk, t_new):
    page_refs = rest[:pp]
    mask_ref = rest[pp]
    key_ref = rest[pp + 1]
    cut_ref = rest[pp + 2]
    del pt_ref
    c = pl.program_id(1)
    qs = SUBLANES
    iq = iq_ref[0]
    wcol = iw_ref[0]
    row = _iota((qs, LANES), 0)
    lane = _iota((qs, LANES), 1)

    def scores(keys_bf16):
        lg = _nt(iq, keys_bf16)
        sc = (jnp.maximum(lg, 0.0) * wcol).reshape(qs, IDX_HEADS, LANES).sum(axis=1)
        return sc + 0.0

    for j in range(pp):
        key_ref[c * pp + j] = _to_key(scores(page_refs[j][0, 0].astype(BF16)))

    @pl.when(c == pl.num_programs(1) - 1)
    def _():
        sc = scores(iknew_ref[0])
        sm = jnp.where(lane <= row, sc, MASK_VALUE)
        key_ref[n_pages] = jnp.where(lane < t_new, _to_key(sm), INT_MIN)
        n_all = n_pages + 1

        def count(pred):
            def body(pg, acc):
                return acc + jnp.where(pred(key_ref[pg], pg), 1, 0)
            acc = lax.fori_loop(0, n_all, body, jnp.zeros((qs, LANES), I32))
            return jnp.sum(acc, axis=1, keepdims=True)

        thr = _kth_largest_key(lambda cand: count(lambda blk, pg: blk >= cand), topk, (qs, 1))
        n_gt = count(lambda blk, pg: blk > thr)
        n_eq = count(lambda blk, pg: blk == thr)
        need = topk - n_gt
        cut_ref[...] = jnp.full((qs, LANES), 2 ** 30, I32)

        @pl.when(jnp.max(n_eq - need) > 0)
        def _():
            def body(t, ans):
                cand = ans + jnp.left_shift(jnp.int32(1), 15 - t)
                upto = count(lambda blk, pg: (blk == thr) & ((pg * LANES + lane) <= cand - 1))
                return jnp.where(upto < need, cand, ans)
            last = lax.fori_loop(0, 16, body, jnp.zeros((qs, 1), I32))
            cut_ref[...] = jnp.broadcast_to(last, (qs, LANES))

        last = cut_ref[...]
        lo = jnp.maximum(thr, VALID_KEY)

        def write(pg, carry):
            blk = key_ref[pg]
            sel = (blk > lo) | ((blk == thr) & (thr > VALID_KEY) & ((pg * LANES + lane) <= last))
            mask_ref[0, pg] = jnp.where(sel, 0.0, MASK_VALUE)
            return carry

        lax.fori_loop(0, n_all, write, 0)


def _idx_sample(page_table, iq_rows, iw_col, ik_new, cache_kidx, layer, pp, t_new):
    nseq, n_pages = page_table.shape
    topk = min(TOPK_MAX, (n_pages * PAGE_SIZE + t_new) // 4)
    qs = SUBLANES

    def page_spec(j):
        return pl.BlockSpec((1, 1, PAGE_SIZE, IDX_DIM), lambda b, c, pt: (layer, pt[b, c * pp + j], 0, 0))

    grid_spec = pltpu.PrefetchScalarGridSpec(
        num_scalar_prefetch=1,
        grid=(nseq, n_pages // pp),
        in_specs=[pl.BlockSpec((1, qs * IDX_HEADS, IDX_DIM), lambda b, c, pt: (b, 0, 0)),
                  pl.BlockSpec((1, qs * IDX_HEADS, 1), lambda b, c, pt: (b, 0, 0)),
                  pl.BlockSpec((1, PAGE_SIZE, IDX_DIM), lambda b, c, pt: (b, 0, 0))]
                 + [page_spec(j) for j in range(pp)],
        out_specs=pl.BlockSpec((1, n_pages + 1, qs, LANES), lambda b, c, pt: (b, 0, 0, 0)),
        scratch_shapes=[pltpu.VMEM((n_pages + 1, qs, LANES), I32),
                        pltpu.VMEM((qs, LANES), I32)],
    )
    return pl.pallas_call(
        functools.partial(_idx_sample_kernel, pp=pp, n_pages=n_pages, topk=topk, t_new=t_new),
        grid_spec=grid_spec,
        out_shape=jax.ShapeDtypeStruct((nseq, n_pages + 1, qs, LANES), F32),
        compiler_params=_cparams("parallel", "arbitrary"),
        name="idx_sample",
    )(page_table, iq_rows, iw_col, ik_new, *([cache_kidx] * pp))


def _attn_sample_kernel(pt_ref, q_ref, mask_ref, masknew_ref, knew_ref, vnew_ref, *rest, pp, t_new):
    k_refs = rest[:pp]
    v_refs = rest[pp:2 * pp]
    o_ref = rest[2 * pp]
    m_ref, l_ref, acc_ref = rest[2 * pp + 1:]
    del pt_ref
    c = pl.program_id(1)
    nrow = t_new * A_HEADS
    q = q_ref[0]

    @pl.when(c == 0)
    def _():
        m_ref[...] = jnp.full((nrow, 1), MASK_VALUE, F32)
        l_ref[...] = jnp.zeros((nrow, 1), F32)
        acc_ref[...] = jnp.zeros(acc_ref.shape, F32)

    def step(kb, vb, mask8):
        mrows = jnp.concatenate([jnp.broadcast_to(mask8[t:t + 1, :], (A_HEADS, LANES)) for t in range(t_new)], axis=0)
        s = _nt(q, kb) + mrows
        m = m_ref[...]
        m_new = jnp.maximum(m, jnp.max(s, axis=1, keepdims=True))
        alpha = jnp.exp(m - m_new)
        e = jnp.exp(s - m_new)
        l_ref[...] = alpha * l_ref[...] + jnp.sum(e, axis=1, keepdims=True)
        acc_ref[...] = alpha * acc_ref[...] + _dot(e.astype(BF16), vb)
        m_ref[...] = m_new

    for j in range(pp):
        step(k_refs[j][0, 0].astype(BF16), v_refs[j][0, 0].astype(BF16), mask_ref[0, j])

    @pl.when(c == pl.num_programs(1) - 1)
    def _():
        step(knew_ref[0], vnew_ref[0], masknew_ref[0, 0])
        o = acc_ref[...] / l_ref[...]
        grp = (_iota((nrow, 1), 0) % A_HEADS) // (A_HEADS // A_KV_HEADS)
        out = jnp.zeros((nrow, A_HEAD_DIM), F32)
        for g in range(A_KV_HEADS):
            out = out + jnp.where(grp == g, o[:, g * A_HEAD_DIM:(g + 1) * A_HEAD_DIM], 0.0)
        o_ref[0] = out


def _attn_sample(page_table, q_rows, mask, k_new, v_new, cache_k, cache_v, layer, pp, t_new):
    nseq, n_pages = page_table.shape
    kvw = A_KV_HEADS * A_HEAD_DIM
    nrow = t_new * A_HEADS

    def page_spec(j):
        return pl.BlockSpec((1, 1, PAGE_SIZE, kvw), lambda b, c, pt: (layer, pt[b, c * pp + j], 0, 0))

    grid_spec = pltpu.PrefetchScalarGridSpec(
        num_scalar_prefetch=1,
        grid=(nseq, n_pages // pp),
        in_specs=[pl.BlockSpec((1, nrow, kvw), lambda b, c, pt: (b, 0, 0)),
                  pl.BlockSpec((1, pp, SUBLANES, LANES), lambda b, c, pt: (b, c, 0, 0)),
                  pl.BlockSpec((1, 1, SUBLANES, LANES), lambda b, c, pt: (b, n_pages, 0, 0)),
                  pl.BlockSpec((1, PAGE_SIZE, kvw), lambda b, c, pt: (b, 0, 0)),
                  pl.BlockSpec((1, PAGE_SIZE, kvw), lambda b, c, pt: (b, 0, 0))]
                 + [page_spec(j) for j in range(pp)] * 2,
        out_specs=pl.BlockSpec((1, nrow, A_HEAD_DIM), lambda b, c, pt: (b, 0, 0)),
        scratch_shapes=[pltpu.VMEM((nrow, 1), F32), pltpu.VMEM((nrow, 1), F32), pltpu.VMEM((nrow, kvw), F32)],
    )
    return pl.pallas_call(
        functools.partial(_attn_sample_kernel, pp=pp, t_new=t_new),
        grid_spec=grid_spec,
        out_shape=jax.ShapeDtypeStruct((nseq, nrow, A_HEAD_DIM), F32),
        compiler_params=_cparams("parallel", "arbitrary"),
        name="attn_sample",
    )(page_table, q_rows, mask, mask, k_new, v_new, *([cache_k] * pp), *([cache_v] * pp))


def _sprep_kernel(aq_ref, ak_ref, iq_ref, ikw_ref, qg_ref, kg_ref, qn_ref, kn_ref, iqs_ref, iw_ref):
    for c, t in enumerate(_head_rmsnorm(aq_ref[...], qg_ref[...])):
        qn_ref[:, c * LANES:(c + 1) * LANES] = (t * (A_HEAD_DIM ** -0.5)).astype(BF16)
    for c, t in enumerate(_head_rmsnorm(ak_ref[...], kg_ref[...])):
        kn_ref[:, c * LANES:(c + 1) * LANES] = t
    iqs_ref[...] = (iq_ref[...] * (IDX_DIM ** -0.5)).astype(BF16)
    iw_ref[...] = ikw_ref[...] * (IDX_HEADS ** -0.5)


def _sprep(p, q_gain, k_gain):
    n = p.shape[0]
    kvw = A_KV_HEADS * A_HEAD_DIM
    return pl.pallas_call(
        _sprep_kernel,
        grid=(1,),
        in_specs=[pl.BlockSpec((n, W), lambda i: (0, C_AQ // W)),
                  pl.BlockSpec((n, kvw), lambda i: (0, C_AK // kvw)),
                  pl.BlockSpec((n, W), lambda i: (0, C_IQ // W)),
                  pl.BlockSpec((n, LANES), lambda i: (0, C_IKW // LANES)),
                  pl.BlockSpec((1, LANES), lambda i: (0, 0)),
                  pl.BlockSpec((1, LANES), lambda i: (0, 0))],
        out_specs=[pl.BlockSpec((n, W), lambda i: (0, 0)), pl.BlockSpec((n, kvw), lambda i: (0, 0)),
                   pl.BlockSpec((n, W), lambda i: (0, 0)), pl.BlockSpec((n, LANES), lambda i: (0, 0))],
        out_shape=[jax.ShapeDtypeStruct((n, W), BF16), jax.ShapeDtypeStruct((n, kvw), F32),
                   jax.ShapeDtypeStruct((n, W), BF16), jax.ShapeDtypeStruct((n, LANES), F32)],
        compiler_params=_cparams("arbitrary"),
        name="sprep",
    )(p, p, p, p, q_gain, k_gain)


def _merge_kernel(oa_ref, ob_ref, oc_ref, od_ref, gates_ref, x_ref, wb_ref, wo_ref, n2_ref, x1_ref, h2_ref):
    merged = None
    for b, o_ref in enumerate((oa_ref, ob_ref, oc_ref, od_ref)):
        up = _dot(o_ref[...], wb_ref[b])
        t = _sigmoid(gates_ref[:, b * D_MODEL:(b + 1) * D_MODEL]) * up
        merged = t if merged is None else merged + t
    x1 = x_ref[...] + _dot(merged.astype(BF16), wo_ref[...])
    x1_ref[...] = x1
    ms = jnp.mean(x1 * x1, axis=-1, keepdims=True)
    h2_ref[...] = (x1 * lax.rsqrt(ms + RMS_EPS) * n2_ref[...]).astype(BF16)


def _merge(oa, ob, oc, od, p, x, w_branch, w_out, norm2, tm):
    n = x.shape[0]
    row = lambda wdt: pl.BlockSpec((tm, wdt), lambda i: (i, 0))
    return pl.pallas_call(
        _merge_kernel,
        grid=(n // tm,),
        in_specs=[row(W), row(W), row(W), row(W),
                  pl.BlockSpec((tm, N_BRANCH * D_MODEL), lambda i: (i, C_GATES // (N_BRANCH * D_MODEL))),
                  row(D_MODEL),
                  pl.BlockSpec((N_BRANCH, W, D_MODEL), lambda i: (0, 0, 0)),
                  pl.BlockSpec((D_MODEL, D_MODEL), lambda i: (0, 0)),
                  pl.BlockSpec((1, D_MODEL), lambda i: (0, 0))],
        out_specs=[row(D_MODEL), row(D_MODEL)],
        out_shape=[jax.ShapeDtypeStruct((n, D_MODEL), F32), jax.ShapeDtypeStruct((n, D_MODEL), BF16)],
        compiler_params=_cparams("parallel"),
        name="merge",
    )(oa, ob, oc, od, p, x, w_branch, w_out, norm2)


def _ffn_kernel(h_ref, x_ref, wg_ref, wu_ref, wd_ref, o_ref, acc_ref):
    j = pl.program_id(1)

    @pl.when(j == 0)
    def _():
        acc_ref[...] = x_ref[...]

    h = h_ref[...]
    act = _silu(_dot(h, wg_ref[...])) * _dot(h, wu_ref[...])
    acc_ref[...] += _dot(act.astype(BF16), wd_ref[...])

    @pl.when(j == pl.num_programs(1) - 1)
    def _():
        o_ref[...] = acc_ref[...]


def _ffn(h2, x1, w_gu, w_down, tm, fc):
    n = x1.shape[0]
    nf = D_FF // fc
    return pl.pallas_call(
        _ffn_kernel,
        grid=(n // tm, nf),
        in_specs=[pl.BlockSpec((tm, D_MODEL), lambda i, j: (i, 0)),
                  pl.BlockSpec((tm, D_MODEL), lambda i, j: (i, 0)),
                  pl.BlockSpec((D_MODEL, fc), lambda i, j: (0, j)),
                  pl.BlockSpec((D_MODEL, fc), lambda i, j: (0, nf + j)),
                  pl.BlockSpec((fc, D_MODEL), lambda i, j: (j, 0))],
        out_specs=pl.BlockSpec((tm, D_MODEL), lambda i, j: (i, 0)),
        out_shape=jax.ShapeDtypeStruct((n, D_MODEL), F32),
        scratch_shapes=[pltpu.VMEM((tm, D_MODEL), F32)],
        compiler_params=_cparams("parallel", "arbitrary"),
        name="ffn",
    )(h2, x1, w_gu, w_gu, w_down)


def _pack_w_in(w_in):
    (a_q, a_k, a_v, a_iq, a_ik, a_iw, r_q, r_k, r_v, r_g, c_q, c_f, c_i, c_g,
     m_z, m_xbc, m_dt, gates) = jnp.split(w_in, SPLIT_POINTS, axis=-1)

    def pad_to_lanes(a):
        return jnp.pad(a, ((0, 0), (0, 0), (0, LANES - a.shape[-1])))

    ikw = pad_to_lanes(jnp.concatenate([a_ik, a_iw], axis=-1))
    packed = jnp.concatenate([gates, m_xbc, a_q, a_iq, r_q, r_k, r_v, r_g, c_q, c_f, c_i, c_g, m_z,
                              a_k, a_v, ikw, pad_to_lanes(m_dt)], axis=-1)
    return packed.astype(BF16)


def _rotary_tables(pos):
    half = R_DK // 2
    inv_freq = ROPE_BASE ** (-jnp.arange(half, dtype=F32) / half)
    ang = pos.astype(F32)[:, None] * inv_freq[None, :]
    c, s = jnp.cos(ang), jnp.sin(ang)
    return jnp.concatenate([c, c], axis=-1), jnp.concatenate([-s, s], axis=-1)


def _lower_bounds(lb_param):
    pr = jax.nn.softmax(lb_param.astype(F32), axis=0)
    return jnp.cumsum(pr, axis=0) - pr[0]


def _pad_lanes_row(v):
    return jnp.pad(v, (0, LANES - v.shape[0]))[None, :]


def kernel(x_prompt, x_sample, cache_k, cache_v, cache_kidx, state_ret, state_hgrn, state_ssm, state_conv,
           page_table, norm1_g, w_in, q_norm_g, k_norm_g, lb_param, hgrn_norm_g, A_log, dt_bias, D_skip,
           conv_w, conv_b, ssm_norm_g, w_branch, w_out, norm2_g, w_gu, w_down):
    bp, tp, _ = x_prompt.shape
    bs, ts, _ = x_sample.shape
    depth = w_in.shape[0]
    n_pool = cache_k.shape[1]
    n_pages = page_table.shape[1]
    kvw = A_KV_HEADS * A_HEAD_DIM
    tsp = SUBLANES
    npair = M_HEADS // 2

    w_in_p = _pack_w_in(w_in)
    w_branch_b = w_branch.astype(BF16)
    w_out_b = w_out.astype(BF16)
    w_gu_b = w_gu.astype(BF16)
    w_down_b = w_down.astype(BF16)
    lbs = _lower_bounds(lb_param)
    cos_p, sin_p = _rotary_tables(jnp.arange(tp, dtype=jnp.int32))
    pos_s = n_pages * PAGE_SIZE + jnp.arange(tsp, dtype=jnp.int32)
    cos_s, sin_s = _rotary_tables(pos_s)
    ck = cache_k.reshape(depth, n_pool, PAGE_SIZE, kvw)
    cv = cache_v.reshape(depth, n_pool, PAGE_SIZE, kvw)
    head_group = (jnp.arange(A_HEADS) // (A_HEADS // A_KV_HEADS))[:, None] == jnp.arange(A_KV_HEADS)[None, :]

    zero_ret = jnp.zeros((bp, R_HEADS, R_DK, R_DV), F32)
    zero_hgrn = jnp.zeros((bp, H_HEADS, H_DK, H_DV), F32)
    zero_ssm = jnp.zeros((bp, npair, 2 * M_HEAD_DIM, M_STATE), F32)
    zero_buf = jnp.zeros((bp, SUBLANES, M_CONV_DIM), F32)

    xp = x_prompt.reshape(bp * tp, D_MODEL)
    xs = x_sample.reshape(bs * ts, D_MODEL)
    outs_p = [[] for _ in range(7)]
    outs_s = [[] for _ in range(7)]
    for l in range(depth):
        n1 = norm1_g[l][None, :]
        n2 = norm2_g[l][None, :]
        qg = jnp.tile(q_norm_g[l], 2)[None, :]
        kg = jnp.tile(k_norm_g[l], 2)[None, :]
        lb = lbs[l][None, :]
        hng = hgrn_norm_g[l][None, :]
        alog = _pad_lanes_row(A_log[l])
        dtb = _pad_lanes_row(dt_bias[l])
        dsk = jnp.repeat(D_skip[l], M_HEAD_DIM)[None, :]
        sng = ssm_norm_g[l][None, :]
        cw = conv_w[l]
        cb = conv_b[l][None, :]

        p = _inproj(xp, n1, w_in_p[l], 512, 1280)
        kn, vv, ik, vt = _kprep(p, kg)
        oa = _attn_prompt(p, kn, ik, vt, qg, bp, tp)
        ob, ret_new = _retention(p, cos_p, sin_p, zero_ret, bp, tp, 256, R_CHUNK, R_CHUNK)
        oc, hgrn_new = _hgrn2(p, lb, hng, zero_hgrn, bp, tp, 256, H_CHUNK, H_CHUNK, 16)
        od, ssm_new, buf_new = _ssd(p, cw, cb, alog, dtb, dsk, sng, zero_ssm, zero_buf, bp, tp, 256, M_CHUNK, M_CHUNK)
        x1, h2 = _merge(oa, ob, oc, od, p, xp, w_branch_b[l], w_out_b[l], n2, 256)
        xp = _ffn(h2, x1, w_gu_b[l], w_down_b[l], 512, D_FF // 2)
        for i, a in enumerate((kn.reshape(bp, tp, A_KV_HEADS, A_HEAD_DIM), vv.reshape(bp, tp, A_KV_HEADS, A_HEAD_DIM),
                               ik.reshape(bp, tp, IDX_DIM), ret_new, hgrn_new,
                               ssm_new.reshape(bp, M_HEADS, M_HEAD_DIM, M_STATE),
                               buf_new[:, SUBLANES - (M_CONV - 1):, :])):
            outs_p[i].append(a)

        ps = _inproj(xs, n1, w_in_p[l], bs * ts, 1280)
        qn_s, kn_s, iq_s, iw_s = _sprep(ps, qg, kg)
        v_s = ps[:, C_AV:C_AV + kvw]
        ik_s = ps[:, C_IKW:C_IKW + IDX_DIM]
        pad_q = lambda a: jnp.pad(a, ((0, 0), (0, tsp - ts)) + ((0, 0),) * (a.ndim - 2))
        pad_keys = lambda a: jnp.pad(a.reshape(bs, ts, -1), ((0, 0), (0, PAGE_SIZE - ts), (0, 0)))
        iq_rows = pad_q(iq_s.reshape(bs, ts, IDX_HEADS, IDX_DIM)).reshape(bs, tsp * IDX_HEADS, IDX_DIM)
        iw_col = pad_q(iw_s[:, IDX_DIM:IDX_DIM + IDX_HEADS].reshape(bs, ts, IDX_HEADS)).reshape(bs, tsp * IDX_HEADS, 1)
        mask = _idx_sample(page_table, iq_rows, iw_col, pad_keys(ik_s).astype(BF16), cache_kidx, l, 8, ts)
        q4 = qn_s.reshape(bs, ts, A_HEADS, 1, A_HEAD_DIM)
        q_rows = jnp.where(head_group[None, None, :, :, None], q4, jnp.zeros_like(q4)).reshape(bs, ts * A_HEADS, kvw)
        oa_s = _attn_sample(page_table, q_rows, mask, pad_keys(kn_s).astype(BF16), pad_keys(v_s).astype(BF16),
                            ck, cv, l, 8, ts)
        oa_s = oa_s.reshape(bs * ts, W).astype(BF16)
        psp = pad_q(ps.reshape(bs, ts, NP)).reshape(bs * tsp, NP)
        ssm0 = state_ssm[l].reshape(bs, npair, 2 * M_HEAD_DIM, M_STATE)
        buf0 = jnp.pad(state_conv[l], ((0, 0), (SUBLANES - (M_CONV - 1), 0), (0, 0)))
        ob_s, ret_s = _retention(psp, cos_s, sin_s, state_ret[l], bs, tsp, tsp, tsp, ts)
        oc_s, hgrn_s = _hgrn2(psp, lb, hng, state_hgrn[l], bs, tsp, tsp, tsp, ts, tsp)
        od_s, ssm_s, buf_s = _ssd(psp, cw, cb, alog, dtb, dsk, sng, ssm0, buf0, bs, tsp, tsp, tsp, ts)
        unpad = lambda a: a.reshape(bs, tsp, W)[:, :ts].reshape(bs * ts, W)
        x1s, h2s = _merge(oa_s, unpad(ob_s), unpad(oc_s), unpad(od_s), ps, xs, w_branch_b[l], w_out_b[l], n2, bs * ts)
        xs = _ffn(h2s, x1s, w_gu_b[l], w_down_b[l], bs * ts, D_FF // 2)
        for i, a in enumerate((kn_s.reshape(bs, ts, A_KV_HEADS, A_HEAD_DIM), v_s.reshape(bs, ts, A_KV_HEADS, A_HEAD_DIM),
                               ik_s.reshape(bs, ts, IDX_DIM), ret_s, hgrn_s,
                               ssm_s.reshape(bs, M_HEADS, M_HEAD_DIM, M_STATE),
                               buf_s[:, SUBLANES - (M_CONV - 1):, :])):
            outs_s[i].append(a)

    dtypes = (cache_k.dtype, cache_v.dtype, cache_kidx.dtype, state_ret.dtype,
              state_hgrn.dtype, state_ssm.dtype, state_conv.dtype)
    res_p = [jnp.stack(a).astype(d) for a, d in zip(outs_p, dtypes)]
    res_s = [jnp.stack(a).astype(d) for a, d in zip(outs_s, dtypes)]
    return (xp.reshape(bp, tp, D_MODEL), xs.reshape(bs, ts, D_MODEL), *res_p, *res_s)
```

```python
import functools
import math
import struct

import jax
import jax.numpy as jnp
from jax import lax
from jax.experimental import pallas as pl
from jax.experimental.pallas import tpu as pltpu

F32 = jnp.float32
BF16 = jnp.bfloat16
I32 = jnp.int32

D_MODEL = 1024
PAGE_SIZE = 128
BRANCH_WIDTH = D_MODEL // 2
N_BRANCH = 4
A_HEAD_DIM = 64
A_HEADS = BRANCH_WIDTH // A_HEAD_DIM
A_KV_HEADS = A_HEADS // 2
IDX_HEADS = 8
IDX_DIM = 64
TOPK_MAX = 256
Q_BLOCK = 128
R_HEADS = 4
R_DK = BRANCH_WIDTH // R_HEADS
R_DV = BRANCH_WIDTH // R_HEADS
R_CHUNK = 128
ROPE_BASE = 10000.0
H_HEADS = 4
H_DK = 128
H_DV = BRANCH_WIDTH // H_HEADS
H_CHUNK = 64
MIN_FORGET = 1e-30
M_HEAD_DIM = 64
M_HEADS = BRANCH_WIDTH // M_HEAD_DIM
M_INNER = M_HEADS * M_HEAD_DIM
M_GROUPS = 2
M_STATE = 128
M_CONV = 4
M_CHUNK = 128
M_CONV_DIM = M_INNER + 2 * M_GROUPS * M_STATE
D_FF = -(-8 * D_MODEL // (3 * 256)) * 256
RMS_EPS = 1e-6
GN_EPS = 1e-6
MASK_VALUE = -1e30

SPLIT_SIZES = (
    A_HEADS * A_HEAD_DIM, A_KV_HEADS * A_HEAD_DIM, A_KV_HEADS * A_HEAD_DIM,
    IDX_HEADS * IDX_DIM, IDX_DIM, IDX_HEADS,
    R_HEADS * R_DK, R_HEADS * R_DK, R_HEADS * R_DV, R_HEADS * R_DV,
    H_HEADS * H_DK, H_HEADS * H_DK, H_HEADS * H_DV, H_HEADS * H_DV,
    M_INNER, M_CONV_DIM, M_HEADS,
    N_BRANCH * D_MODEL,
)
SPLIT_POINTS = tuple(sum(SPLIT_SIZES[:i + 1]) for i in range(len(SPLIT_SIZES) - 1))

LANES = 128
SUBLANES = 8
VMEM_LIMIT = 56 * 1024 * 1024

C_GATES = 0
C_XBC = 4096
C_AQ = 5120
C_IQ = 5632
C_RQ = 6144
C_RK = 6656
C_RV = 7168
C_RG = 7680
C_CQ = 8192
C_CF = 8704
C_CI = 9216
C_CG = 9728
C_MZ = 10240
C_AK = 10752
C_AV = 11008
C_IKW = 11264
C_DT = 11392
NP = 11520
W = BRANCH_WIDTH


def _float_key(x):
    b = struct.unpack("<i", struct.pack("<f", x))[0]
    return b if b >= 0 else b ^ 0x7FFFFFFF


VALID_KEY = _float_key(0.5 * MASK_VALUE)
INT_MIN = -2 ** 31
LOG2_E = math.log2(math.e)


def _cparams(*sem):
    return pltpu.CompilerParams(dimension_semantics=sem, vmem_limit_bytes=VMEM_LIMIT)


def _nt(a, b):
    return lax.dot_general(a, b, (((1,), (1,)), ((), ())), preferred_element_type=F32)


def _tn(a, b):
    return lax.dot_general(a, b, (((0,), (0,)), ((), ())), preferred_element_type=F32)


def _dot(a, b):
    return jnp.dot(a, b, preferred_element_type=F32)


def _dot_exact_lhs(a01, x):
    a = a01.astype(BF16)
    x0 = x.astype(BF16)
    r1 = x - x0.astype(F32)
    x1 = r1.astype(BF16)
    x2 = (r1 - x1.astype(F32)).astype(BF16)
    return _dot(a, x0) + _dot(a, x1) + _dot(a, x2)


def _dot_exact_rhs(x, b01):
    b = b01.astype(BF16)
    x0 = x.astype(BF16)
    r1 = x - x0.astype(F32)
    x1 = r1.astype(BF16)
    x2 = (r1 - x1.astype(F32)).astype(BF16)
    return _dot(x0, b) + _dot(x1, b) + _dot(x2, b)


def _sigmoid(x):
    return 1.0 / (1.0 + jnp.exp(-x))


def _silu(x):
    return x * _sigmoid(x)


def _iota(shape, dim):
    return lax.broadcasted_iota(I32, shape, dim)


def _tri_incl(n):
    return (_iota((n, n), 0) >= _iota((n, n), 1)).astype(F32)


def _inproj_kernel(x_ref, g_ref, w_ref, o_ref, h_ref):
    @pl.when(pl.program_id(1) == 0)
    def _():
        x = x_ref[...]
        ms = jnp.mean(x * x, axis=-1, keepdims=True)
        h_ref[...] = (x * lax.rsqrt(ms + RMS_EPS) * g_ref[...]).astype(BF16)

    o_ref[...] = _dot(h_ref[...], w_ref[...]).astype(o_ref.dtype)


def _inproj(x, g, w, tm, tn, out_dtype):
    n = x.shape[0]
    return pl.pallas_call(
        _inproj_kernel,
        grid=(n // tm, NP // tn),
        in_specs=[pl.BlockSpec((tm, D_MODEL), lambda i, j: (i, 0)),
                  pl.BlockSpec((1, D_MODEL), lambda i, j: (0, 0)),
                  pl.BlockSpec((D_MODEL, tn), lambda i, j: (0, j))],
        out_specs=pl.BlockSpec((tm, tn), lambda i, j: (i, j)),
        out_shape=jax.ShapeDtypeStruct((n, NP), out_dtype),
        scratch_shapes=[pltpu.VMEM((tm, D_MODEL), BF16)],
        compiler_params=_cparams("parallel", "arbitrary"),
        name="inproj",
    )(x, g, w)


def _head_rmsnorm(x, gain_row):
    pair = (_iota((LANES, LANES), 0) // A_HEAD_DIM == _iota((LANES, LANES), 1) // A_HEAD_DIM).astype(F32)
    outs = []
    for c in range(x.shape[1] // LANES):
        xc = x[:, c * LANES:(c + 1) * LANES]
        ms = _dot_exact_rhs(xc * xc, pair) * (1.0 / A_HEAD_DIM)
        outs.append(xc * lax.rsqrt(ms + RMS_EPS) * gain_row)
    return outs


def _kth_largest_key(count_ge, k, shape):
    nonneg = count_ge(jnp.zeros(shape, I32)) >= k
    prefix = jnp.where(nonneg, 0, INT_MIN).astype(I32)

    def body(t, prefix):
        cand = prefix + jnp.left_shift(jnp.int32(1), 30 - t)
        return jnp.where(count_ge(cand) >= k, cand, prefix)

    return lax.fori_loop(0, 31, body, prefix)


def _to_key(s):
    b = lax.bitcast_convert_type(s, I32)
    return jnp.where(b >= 0, b, b ^ 0x7FFFFFFF)


def _kprep_kernel(ak_ref, av_ref, ikw_ref, kg_ref, kn_ref, v_ref, ik_ref, knb_ref, vb_ref, ik2_ref):
    kn = _head_rmsnorm(ak_ref[...].astype(F32), kg_ref[...])
    for c, t in enumerate(kn):
        kn_ref[:, c * LANES:(c + 1) * LANES] = t
        knb_ref[:, c * LANES:(c + 1) * LANES] = t.astype(BF16)
    v = av_ref[...].astype(F32)
    v_ref[...] = v
    vb_ref[...] = v.astype(BF16)
    ikw = ikw_ref[...].astype(F32)
    ik_ref[...] = ikw[:, :IDX_DIM]
    lane = _iota(ikw.shape, 1)
    ik2_ref[...] = jnp.where(lane < IDX_DIM, ikw, pltpu.roll(ikw, IDX_DIM, 1)).astype(BF16)


def _kprep(p, k_gain, tm):
    n = p.shape[0]
    kvw = A_KV_HEADS * A_HEAD_DIM
    row = lambda wdt: pl.BlockSpec((tm, wdt), lambda i: (i, 0))
    return pl.pallas_call(
        _kprep_kernel,
        grid=(n // tm,),
        in_specs=[pl.BlockSpec((tm, kvw), lambda i: (i, C_AK // kvw)),
                  pl.BlockSpec((tm, kvw), lambda i: (i, C_AV // kvw)),
                  pl.BlockSpec((tm, LANES), lambda i: (i, C_IKW // LANES)),
                  pl.BlockSpec((1, LANES), lambda i: (0, 0))],
        out_specs=[row(kvw), row(kvw), row(IDX_DIM), row(kvw), row(kvw), row(LANES)],
        out_shape=[jax.ShapeDtypeStruct((n, kvw), F32),
                   jax.ShapeDtypeStruct((n, kvw), F32),
                   jax.ShapeDtypeStruct((n, IDX_DIM), F32),
                   jax.ShapeDtypeStruct((n, kvw), BF16),
                   jax.ShapeDtypeStruct((n, kvw), BF16),
                   jax.ShapeDtypeStruct((n, LANES), BF16)],
        compiler_params=_cparams("parallel"),
        name="kprep",
    )(p, p, p, k_gain)


def _attn_prompt_kernel(aq_ref, iq_ref, ikw_ref, qg_ref, knb_ref, vb_ref, ik2_ref, o_ref,
                        key_ref, msk_ref, s_ref, qt_ref, iqt_ref, acc_ref, cut_ref, *, topk, kb):
    i = pl.program_id(1)
    qb = Q_BLOCK
    nblk = (i * qb + qb + kb - 1) // kb
    row = _iota((kb, qb), 0)
    lane = _iota((kb, qb), 1)
    lane_q = _iota((qb, LANES), 1)
    own_half = [lane_q < A_HEAD_DIM, lane_q >= A_HEAD_DIM]
    n_pairs = A_KV_HEADS // 2
    hpp = A_HEADS // n_pairs
    nsub = kb // SUBLANES

    qn = _head_rmsnorm(aq_ref[...].astype(F32), qg_ref[...])
    for h in range(A_HEADS):
        src, dst = h % 2, (h // 2) % 2
        t = qn[h // 2] * (A_HEAD_DIM ** -0.5 * LOG2_E)
        if src != dst:
            t = pltpu.roll(t, A_HEAD_DIM, 1)
        t = jnp.where(own_half[dst], t, 0.0)
        qt_ref[h // hpp, :, (h % hpp) * qb:(h % hpp + 1) * qb] = t.T.astype(BF16)
    for h in range(IDX_HEADS):
        t = iq_ref[:, (h // 2) * LANES:(h // 2 + 1) * LANES].astype(F32) * (IDX_DIM ** -0.5)
        iqt_ref[:, h * qb:(h + 1) * qb] = jnp.where(own_half[h % 2], t, 0.0).T.astype(BF16)
    iw_t = ikw_ref[...].astype(F32).T[IDX_DIM:IDX_DIM + IDX_HEADS, :] * (IDX_HEADS ** -0.5)

    def score_blk(j, carry):
        ks = pl.multiple_of(j * kb, kb)
        lg = _dot(ik2_ref[pl.ds(ks, kb), :], iqt_ref[...])
        acc = jnp.zeros((kb, qb), F32)
        for h in range(IDX_HEADS):
            acc = acc + jnp.maximum(lg[:, h * qb:(h + 1) * qb], 0.0) * iw_t[h:h + 1, :]
        adm = (ks + row) <= (i * qb + lane)
        key_ref[j] = _to_key(jnp.where(adm, acc, MASK_VALUE))
        return carry

    lax.fori_loop(0, nblk, score_blk, 0)

    def count(pred):
        def body(j, c):
            return c + jnp.where(pred(key_ref[j], j), 1, 0)
        c = lax.fori_loop(0, nblk, body, jnp.zeros((kb, qb), I32))
        return jnp.sum(c, axis=0, keepdims=True)

    cut_ref[0:1, :] = jnp.full((1, qb), INT_MIN, I32)
    cut_ref[1:2, :] = jnp.full((1, qb), 2 ** 30, I32)

    @pl.when(i * qb + qb > topk)
    def _():
        thr = _kth_largest_key(lambda cand: count(lambda blk, j: blk >= cand), topk, (1, qb))
        cut_ref[0:1, :] = thr
        n_gt = count(lambda blk, j: blk > thr)
        n_eq = count(lambda blk, j: blk == thr)
        need = topk - n_gt

        @pl.when(jnp.max(n_eq - need) > 0)
        def _():
            def body(t, ans):
                cand = ans + jnp.left_shift(jnp.int32(1), 15 - t)
                upto = count(lambda blk, j: (blk == thr) & ((j * kb + row) <= cand - 1))
                return jnp.where(upto < need, cand, ans)

            cut_ref[1:2, :] = lax.fori_loop(0, 16, body, jnp.zeros((1, qb), I32))

    thr = cut_ref[0:1, :]
    last = cut_ref[1:2, :]
    lo = jnp.maximum(thr, VALID_KEY)
    tie_ok = thr > VALID_KEY

    def mask_blk(j, carry):
        blk = key_ref[j]
        sel = (blk > lo) | ((blk == thr) & tie_ok & ((j * kb + row) <= last))
        msk_ref[j] = jnp.where(sel, 0.0, MASK_VALUE)
        return carry

    lax.fori_loop(0, nblk, mask_blk, 0)

    def fold(t, op):
        return op(t.reshape(nsub, SUBLANES, qb), axis=0)

    def logits_blk(j, m8):
        ks = pl.multiple_of(j * kb, kb)
        mk = msk_ref[j]
        out = []
        for pr in range(n_pairs):
            s4 = _dot(knb_ref[pl.ds(ks, kb), pr * LANES:(pr + 1) * LANES], qt_ref[pr])
            for hh in range(hpp):
                h = pr * hpp + hh
                s = s4[:, hh * qb:(hh + 1) * qb] + mk
                s_ref[j, h] = s
                out.append(jnp.maximum(m8[h], fold(s, jnp.max)))
        return tuple(out)

    m8 = lax.fori_loop(0, nblk, logits_blk, tuple(jnp.full((SUBLANES, qb), MASK_VALUE, F32) for _ in range(A_HEADS)))
    m_row = [jnp.max(t, axis=0, keepdims=True) for t in m8]

    for pr in range(n_pairs):
        acc_ref[pr] = jnp.zeros((hpp * qb, LANES), F32)

    def values_blk(j, l8):
        ks = pl.multiple_of(j * kb, kb)
        out = []
        for pr in range(n_pairs):
            e_t = []
            for hh in range(hpp):
                h = pr * hpp + hh
                e = jnp.exp2(s_ref[j, h] - m_row[h])
                out.append(l8[h] + fold(e, jnp.sum))
                e_t.append(e.T.astype(BF16))
            lhs = jnp.concatenate(e_t, axis=0)
            acc_ref[pr] = acc_ref[pr] + _dot(lhs, vb_ref[pl.ds(ks, kb), pr * LANES:(pr + 1) * LANES])
        return tuple(out)

    l8 = lax.fori_loop(0, nblk, values_blk, tuple(jnp.zeros((SUBLANES, qb), F32) for _ in range(A_HEADS)))

    for h in range(A_HEADS):
        pr, hh, half = h // hpp, h % hpp, (h // 2) % 2
        l_row = jnp.sum(l8[h], axis=0, keepdims=True)
        l_col = jnp.broadcast_to(l_row, (SUBLANES, qb)).T[:, 0:1]
        o = acc_ref[pr, hh * qb:(hh + 1) * qb, half * A_HEAD_DIM:(half + 1) * A_HEAD_DIM] / l_col
        o_ref[:, h * A_HEAD_DIM:(h + 1) * A_HEAD_DIM] = o.astype(o_ref.dtype)


def _attn_prompt(p, knb, vb, ik2, q_gain, batch, seq, kb):
    nqb = seq // Q_BLOCK
    nkb = seq // kb
    kvw = A_KV_HEADS * A_HEAD_DIM
    topk = min(TOPK_MAX, seq // 4)
    n_pairs = A_KV_HEADS // 2
    hpp = A_HEADS // n_pairs
    return pl.pallas_call(
        functools.partial(_attn_prompt_kernel, topk=topk, kb=kb),
        grid=(batch, nqb),
        in_specs=[pl.BlockSpec((Q_BLOCK, W), lambda b, i: (b * nqb + i, C_AQ // W)),
                  pl.BlockSpec((Q_BLOCK, W), lambda b, i: (b * nqb + i, C_IQ // W)),
                  pl.BlockSpec((Q_BLOCK, LANES), lambda b, i: (b * nqb + i, C_IKW // LANES)),
                  pl.BlockSpec((1, LANES), lambda b, i: (0, 0)),
                  pl.BlockSpec((seq, kvw), lambda b, i: (b, 0)),
                  pl.BlockSpec((seq, kvw), lambda b, i: (b, 0)),
                  pl.BlockSpec((seq, LANES), lambda b, i: (b, 0))],
        out_specs=pl.BlockSpec((Q_BLOCK, W), lambda b, i: (b * nqb + i, 0)),
        out_shape=jax.ShapeDtypeStruct((batch * seq, W), BF16),
        scratch_shapes=[pltpu.VMEM((nkb, kb, Q_BLOCK), I32),
                        pltpu.VMEM((nkb, kb, Q_BLOCK), F32),
                        pltpu.VMEM((nkb, A_HEADS, kb, Q_BLOCK), F32),
                        pltpu.VMEM((n_pairs, LANES, hpp * Q_BLOCK), BF16),
                        pltpu.VMEM((LANES, IDX_HEADS * Q_BLOCK), BF16),
                        pltpu.VMEM((n_pairs, hpp * Q_BLOCK, LANES), F32),
                        pltpu.VMEM((SUBLANES, Q_BLOCK), I32)],
        compiler_params=_cparams("parallel", "arbitrary"),
        name="attn_prompt",
    )(p, p, p, q_gain, knb, vb, ik2)


def _ret_kernel(q_ref, k_ref, v_ref, g_ref, cos_ref, sin_ref, s0_ref, o_ref, s_out_ref, st_ref,
                *, cp, c_true):
    tb = pl.program_id(1)

    @pl.when(tb == 0)
    def _():
        st_ref[...] = s0_ref[0]

    row = _iota((cp, 1), 0).astype(F32)
    diff = (_iota((cp, cp), 0) - _iota((cp, cp), 1)).astype(F32)

    def chunk(c, carry):
        r0 = pl.multiple_of(c * cp, cp)
        rows = pl.ds(r0, cp)
        cos = cos_ref[rows, :]
        sin = sin_ref[rows, :]
        for h in range(R_HEADS):
            lg = math.log1p(-2.0 ** (-5.0 - h))
            cols = slice(h * R_DK, (h + 1) * R_DK)
            q = q_ref[rows, cols].astype(F32)
            k = k_ref[rows, cols].astype(F32)
            v = v_ref[rows, cols].astype(F32)
            q = q * cos + pltpu.roll(q, R_DK // 2, 1) * sin
            k = (k * cos + pltpu.roll(k, R_DK // 2, 1) * sin) * (R_DK ** -0.5)
            decay = jnp.where(diff >= 0, jnp.exp(lg * jnp.maximum(diff, 0.0)), 0.0)
            scores = _nt(q.astype(BF16), k.astype(BF16)) * decay
            s = st_ref[h]
            o = _dot(scores.astype(BF16), v.astype(BF16))
            o = o + _dot((q * jnp.exp(lg * (row + 1.0))).astype(BF16), s.astype(BF16))
            kd = jnp.where(row < c_true, jnp.exp(lg * jnp.maximum(c_true - 1.0 - row, 0.0)), 0.0)
            st_ref[h] = math.exp(lg * c_true) * s + _tn((k * kd).astype(BF16), v.astype(BF16))
            mu = jnp.mean(o, axis=-1, keepdims=True)
            d = o - mu
            var = jnp.mean(d * d, axis=-1, keepdims=True)
            o_ref[rows, cols] = (_silu(g_ref[rows, cols].astype(F32)) * (d * lax.rsqrt(var + GN_EPS))).astype(o_ref.dtype)
        return carry

    lax.fori_loop(0, q_ref.shape[0] // cp, chunk, 0)

    @pl.when(tb == pl.num_programs(1) - 1)
    def _():
        s_out_ref[0] = st_ref[...]


def _hgrn_kernel(q_ref, f_ref, i_ref, g_ref, lb_ref, ng_ref, s0_ref, o_ref, s_out_ref, st_ref,
                 *, cp, c_true, sb):
    tb = pl.program_id(1)

    @pl.when(tb == 0)
    def _():
        st_ref[...] = s0_ref[0]

    tri = _tri_incl(cp)
    rowi = _iota((cp, 1), 0)
    real = rowi < c_true
    lane_a = _iota((sb, cp), 1)
    sub_n = _iota((sb, 1), 0)
    ng = ng_ref[...]

    def chunk(c, carry):
        r0 = pl.multiple_of(c * cp, cp)
        rows = pl.ds(r0, cp)
        for h in range(H_HEADS):
            cols = slice(h * H_DK, (h + 1) * H_DK)
            lb = lb_ref[:, cols]
            q = _silu(q_ref[rows, cols].astype(F32)) * (H_DK ** -0.5)
            f = f_ref[rows, cols].astype(F32)
            forget = lb + (1.0 - lb) * _sigmoid(f)
            logf = jnp.where(real, jnp.log(jnp.maximum(forget, MIN_FORGET)), 0.0)
            k = jnp.where(real, (1.0 - lb) * _sigmoid(-f), 0.0)
            v = jnp.where(real, i_ref[rows, cols].astype(F32), 0.0)
            vb = v.astype(BF16)
            gc = _dot_exact_lhs(tri, logf)
            s = st_ref[h]
            cross = _dot((q * jnp.exp(gc)).astype(BF16), s.astype(BF16))
            g_last = gc[cp - 1:cp, :]
            kd = k * jnp.exp(g_last - gc)
            eg_col = jnp.broadcast_to(jnp.exp(g_last), (SUBLANES, H_DK)).T[:, 0:1]
            st_ref[h] = eg_col * s + _tn(kd.astype(BF16), vb)
            outs = []
            for b in range(cp // sb):
                b0 = b * sb
                gi = gc[b0:b0 + sb, :]
                qi = q[b0:b0 + sb, :]
                ki = k[b0:b0 + sb, :]
                if b > 0:
                    gref = gc[b0:b0 + 1, :]
                    ks = jnp.where(rowi < b0, k * jnp.exp(jnp.minimum(gref - gc, 0.0)), 0.0)
                    a = _nt((qi * jnp.exp(gi - gref)).astype(BF16), ks.astype(BF16))
                else:
                    a = jnp.zeros((sb, cp), F32)
                for m in range(sb):
                    e = jnp.exp(jnp.where(sub_n >= m, gi - gi[m:m + 1, :], MASK_VALUE))
                    col = jnp.sum(qi * e * ki[m:m + 1, :], axis=1, keepdims=True)
                    a = jnp.where(lane_a == b0 + m, col, a)
                outs.append(_dot(a.astype(BF16), vb) + cross[b0:b0 + sb, :])
            o = outs[0] if len(outs) == 1 else jnp.concatenate(outs, axis=0)
            ms = jnp.mean(o * o, axis=-1, keepdims=True)
            o = o * lax.rsqrt(ms + RMS_EPS) * ng
            o_ref[rows, cols] = (_silu(g_ref[rows, cols].astype(F32)) * o).astype(o_ref.dtype)
        return carry

    lax.fori_loop(0, q_ref.shape[0] // cp, chunk, 0)

    @pl.when(tb == pl.num_programs(1) - 1)
    def _():
        s_out_ref[0] = st_ref[...]


def _softplus(x):
    return jnp.maximum(x, 0.0) + jnp.log1p(jnp.exp(-jnp.abs(x)))


def _ssd_kernel(z_ref, xbc_ref, dt_ref, cw_ref, cb_ref, alog_ref, dtb_ref, dskip_ref, ng_ref,
                h0_ref, buf0_ref, o_ref, h_out_ref, buf_out_ref, st_ref, tail_ref, ext_ref,
                *, cp, c_true):
    tb = pl.program_id(1)

    @pl.when(tb == 0)
    def _():
        st_ref[...] = h0_ref[0]
        tail_ref[...] = buf0_ref[0]

    tri = _tri_incl(cp)
    rowi = _iota((cp, 1), 0)
    real = rowi < c_true
    causal = _iota((cp, cp), 0) >= _iota((cp, cp), 1)
    lane = _iota((1, LANES), 1)
    left = lane < M_HEAD_DIM
    top = _iota((LANES, 1), 0) < M_HEAD_DIM
    cw = cw_ref[...]
    cb = cb_ref[...]
    neg_a = -jnp.exp(alog_ref[...])
    gw = M_INNER // M_GROUPS

    def chunk(c, carry):
        r0 = pl.multiple_of(c * cp, cp)
        rows = pl.ds(r0, cp)
        u = xbc_ref[rows, :].astype(F32)
        ext_ref[0:SUBLANES, :] = tail_ref[...]
        ext_ref[SUBLANES:SUBLANES + cp, :] = u
        y = cb + cw[M_CONV - 1:M_CONV, :] * u
        for j in range(M_CONV - 1):
            y = y + cw[j:j + 1, :] * ext_ref[pl.ds(SUBLANES - (M_CONV - 1) + j, cp), :]
        tail_ref[...] = ext_ref[pl.ds(c_true, SUBLANES), :]
        xbc = _silu(y)
        dt = jnp.where(real, _softplus(dt_ref[rows, :].astype(F32) + dtb_ref[...]), 0.0)
        a = neg_a * dt
        cum = _dot_exact_lhs(tri, a)
        cum_t = cum.T
        dt_t = dt.T
        e_cum = jnp.exp(cum)
        cum_last = cum[cp - 1:cp, :]
        w_all = jnp.exp(cum_last - cum) * dt
        e_last = jnp.exp(cum_last)
        z = z_ref[rows, :].astype(F32)
        ys = []
        for pr in range(M_HEADS // 2):
            g = (2 * pr) // (M_HEADS // M_GROUPS)
            bm = xbc[:, M_INNER + g * M_STATE:M_INNER + (g + 1) * M_STATE].astype(BF16)
            cm = xbc[:, M_INNER + (M_GROUPS + g) * M_STATE:M_INNER + (M_GROUPS + g + 1) * M_STATE].astype(BF16)
            xp = xbc[:, pr * LANES:(pr + 1) * LANES]
            xpb = xp.astype(BF16)
            cbm = _nt(cm, bm)
            hp = st_ref[pr]
            cross = _nt(cm, hp.astype(BF16))
            intra = []
            for hh in range(2):
                h = 2 * pr + hh
                seg = jnp.exp(jnp.where(causal, cum[:, h:h + 1] - cum_t[h:h + 1, :], MASK_VALUE))
                mat = cbm * seg * dt_t[h:h + 1, :]
                intra.append(_dot(mat.astype(BF16), xpb))
            h0, h1 = 2 * pr, 2 * pr + 1
            y_pair = (jnp.where(left, intra[0], intra[1])
                      + cross * jnp.where(left, e_cum[:, h0:h0 + 1], e_cum[:, h1:h1 + 1]))
            wx = xp * jnp.where(left, w_all[:, h0:h0 + 1], w_all[:, h1:h1 + 1])
            decay = jnp.where(top, e_last[:, h0:h0 + 1], e_last[:, h1:h1 + 1])
            st_ref[pr] = decay * hp + _tn(wx.astype(BF16), bm)
            y_pair = y_pair + dskip_ref[:, pr * LANES:(pr + 1) * LANES] * xp
            ys.append(y_pair * _silu(z[:, pr * LANES:(pr + 1) * LANES]))
        for g in range(M_GROUPS):
            yg = jnp.concatenate(ys[g * 2:(g + 1) * 2], axis=1)
            ms = jnp.mean(yg * yg, axis=-1, keepdims=True)
            o_ref[rows, g * gw:(g + 1) * gw] = (yg * lax.rsqrt(ms + RMS_EPS)
                                                * ng_ref[:, g * gw:(g + 1) * gw]).astype(o_ref.dtype)
        return carry

    lax.fori_loop(0, z_ref.shape[0] // cp, chunk, 0)

    @pl.when(tb == pl.num_programs(1) - 1)
    def _():
        h_out_ref[0] = st_ref[...]
        buf_out_ref[0] = tail_ref[...]


def _row_blocks(batch, rows_per_seq, tb):
    n_tb = rows_per_seq // tb
    return n_tb, (lambda col: (lambda b, t: (b * n_tb + t, col)))


def _retention(p, cos2, sin2, s0, batch, rows_per_seq, tb, cp, c_true):
    n_tb, at = _row_blocks(batch, rows_per_seq, tb)
    st_spec = pl.BlockSpec((1, R_HEADS, R_DK, R_DV), lambda b, t: (b, 0, 0, 0))
    return pl.pallas_call(
        functools.partial(_ret_kernel, cp=cp, c_true=c_true),
        grid=(batch, n_tb),
        in_specs=[pl.BlockSpec((tb, W), at(C_RQ // W)), pl.BlockSpec((tb, W), at(C_RK // W)),
                  pl.BlockSpec((tb, W), at(C_RV // W)), pl.BlockSpec((tb, W), at(C_RG // W)),
                  pl.BlockSpec((tb, R_DK), lambda b, t: (t, 0)), pl.BlockSpec((tb, R_DK), lambda b, t: (t, 0)),
                  st_spec],
        out_specs=[pl.BlockSpec((tb, W), at(0)), st_spec],
        out_shape=[jax.ShapeDtypeStruct((batch * rows_per_seq, W), BF16),
                   jax.ShapeDtypeStruct((batch, R_HEADS, R_DK, R_DV), F32)],
        scratch_shapes=[pltpu.VMEM((R_HEADS, R_DK, R_DV), F32)],
        compiler_params=_cparams("parallel", "arbitrary"),
        name="retention",
    )(p, p, p, p, cos2, sin2, s0)


def _hgrn2(p, lb, norm_g, s0, batch, rows_per_seq, tb, cp, c_true, sb):
    n_tb, at = _row_blocks(batch, rows_per_seq, tb)
    st_spec = pl.BlockSpec((1, H_HEADS, H_DK, H_DV), lambda b, t: (b, 0, 0, 0))
    return pl.pallas_call(
        functools.partial(_hgrn_kernel, cp=cp, c_true=c_true, sb=sb),
        grid=(batch, n_tb),
        in_specs=[pl.BlockSpec((tb, W), at(C_CQ // W)), pl.BlockSpec((tb, W), at(C_CF // W)),
                  pl.BlockSpec((tb, W), at(C_CI // W)), pl.BlockSpec((tb, W), at(C_CG // W)),
                  pl.BlockSpec((1, W), lambda b, t: (0, 0)), pl.BlockSpec((1, H_DV), lambda b, t: (0, 0)),
                  st_spec],
        out_specs=[pl.BlockSpec((tb, W), at(0)), st_spec],
        out_shape=[jax.ShapeDtypeStruct((batch * rows_per_seq, W), BF16),
                   jax.ShapeDtypeStruct((batch, H_HEADS, H_DK, H_DV), F32)],
        scratch_shapes=[pltpu.VMEM((H_HEADS, H_DK, H_DV), F32)],
        compiler_params=_cparams("parallel", "arbitrary"),
        name="hgrn2",
    )(p, p, p, p, lb, norm_g, s0)


def _ssd(p, conv_w, conv_b, a_log, dt_bias, d_skip, norm_g, h0, buf0, batch, rows_per_seq, tb, cp, c_true):
    n_tb, at = _row_blocks(batch, rows_per_seq, tb)
    npair = M_HEADS // 2
    st_spec = pl.BlockSpec((1, npair, 2 * M_HEAD_DIM, M_STATE), lambda b, t: (b, 0, 0, 0))
    buf_spec = pl.BlockSpec((1, SUBLANES, M_CONV_DIM), lambda b, t: (b, 0, 0))
    row = lambda wdt: pl.BlockSpec((1, wdt), lambda b, t: (0, 0))
    return pl.pallas_call(
        functools.partial(_ssd_kernel, cp=cp, c_true=c_true),
        grid=(batch, n_tb),
        in_specs=[pl.BlockSpec((tb, W), at(C_MZ // W)), pl.BlockSpec((tb, M_CONV_DIM), at(C_XBC // M_CONV_DIM)),
                  pl.BlockSpec((tb, LANES), at(C_DT // LANES)),
                  pl.BlockSpec((M_CONV, M_CONV_DIM), lambda b, t: (0, 0)), row(M_CONV_DIM),
                  row(LANES), row(LANES), row(W), row(W), st_spec, buf_spec],
        out_specs=[pl.BlockSpec((tb, W), at(0)), st_spec, buf_spec],
        out_shape=[jax.ShapeDtypeStruct((batch * rows_per_seq, W), BF16),
                   jax.ShapeDtypeStruct((batch, npair, 2 * M_HEAD_DIM, M_STATE), F32),
                   jax.ShapeDtypeStruct((batch, SUBLANES, M_CONV_DIM), F32)],
        scratch_shapes=[pltpu.VMEM((npair, 2 * M_HEAD_DIM, M_STATE), F32),
                        pltpu.VMEM((SUBLANES, M_CONV_DIM), F32),
                        pltpu.VMEM((cp + SUBLANES, M_CONV_DIM), F32)],
        compiler_params=_cparams("parallel", "arbitrary"),
        name="ssd",
    )(p, p, p, conv_w, conv_b, a_log, dt_bias, d_skip, norm_g, h0, buf0)


def _idx_sample_kernel(pt_ref, iq_ref, iw_ref, iknew_ref, *rest, pp, n_pages, topk, t_new):
    page_refs = rest[:pp]
    mask_ref = rest[pp]
    key_ref = rest[pp + 1]
    cut_ref = rest[pp + 2]
    del pt_ref
    b = pl.program_id(0)
    c = pl.program_id(1)
    qs = SUBLANES
    nrow = mask_ref.shape[1]
    iq = iq_ref[0]
    wcol = iw_ref[0]
    r0 = pl.multiple_of(b * qs, qs)

    def scores(keys_t):
        lg = _dot(iq, keys_t)
        sc = (jnp.maximum(lg, 0.0) * wcol).reshape(IDX_HEADS, qs, LANES).sum(axis=0)
        return sc + 0.0

    for j in range(pp):
        key_ref[c * pp + j, pl.ds(r0, qs), :] = _to_key(scores(page_refs[j][0, 0].astype(BF16)))

    @pl.when(c == pl.num_programs(1) - 1)
    def _():
        row = _iota((qs, LANES), 0)
        lane = _iota((qs, LANES), 1)
        sm = jnp.where(lane <= row, scores(iknew_ref[0]), MASK_VALUE)
        key_ref[n_pages, pl.ds(r0, qs), :] = jnp.where(lane < t_new, _to_key(sm), INT_MIN)

    @pl.when((c == pl.num_programs(1) - 1) & (b == pl.num_programs(0) - 1))
    def _():
        n_all = n_pages + 1
        lane = _iota((nrow, LANES), 1)

        def count(pred):
            def body(pg, acc):
                return acc + jnp.where(pred(key_ref[pg], pg), 1, 0)
            acc = lax.fori_loop(0, n_all, body, jnp.zeros((nrow, LANES), I32))
            return jnp.sum(acc, axis=1, keepdims=True)

        def count_ge(cand):
            cb = jnp.broadcast_to(cand, (nrow, LANES))
            return count(lambda blk, pg: blk >= cb)

        thr = jnp.broadcast_to(_kth_largest_key(count_ge, topk, (nrow, 1)), (nrow, LANES))
        n_gt = count(lambda blk, pg: blk > thr)
        n_eq = count(lambda blk, pg: blk == thr)
        need = topk - n_gt
        cut_ref[...] = jnp.full((nrow, LANES), 2 ** 30, I32)

        @pl.when(jnp.max(n_eq - need) > 0)
        def _():
            def body(t, ans):
                cand = ans + jnp.left_shift(jnp.int32(1), 15 - t)
                cb = jnp.broadcast_to(cand - 1, (nrow, LANES))
                upto = count(lambda blk, pg: (blk == thr) & ((pg * LANES + lane) <= cb))
                return jnp.where(upto < need, cand, ans)
            last = lax.fori_loop(0, 16, body, jnp.zeros((nrow, 1), I32))
            cut_ref[...] = jnp.broadcast_to(last, (nrow, LANES))

        last = cut_ref[...]
        lo = jnp.maximum(thr, VALID_KEY)
        tie_ok = thr > VALID_KEY

        def write(pg, carry):
            blk = key_ref[pg]
            sel = (blk > lo) | ((blk == thr) & tie_ok & ((pg * LANES + lane) <= last))
            mask_ref[pg] = jnp.where(sel, 0.0, MASK_VALUE)
            return carry

        lax.fori_loop(0, n_all, write, 0)


def _idx_sample(page_table, iq_rows, iw_col, ik_new_t, cache_kidx_t, layer, pp, t_new):
    nseq, n_pages = page_table.shape
    topk = min(TOPK_MAX, (n_pages * PAGE_SIZE + t_new) // 4)
    qs = SUBLANES
    nrow = nseq * qs

    def page_spec(j):
        return pl.BlockSpec((1, 1, IDX_DIM, PAGE_SIZE), lambda b, c, pt: (layer, pt[b, c * pp + j], 0, 0))

    grid_spec = pltpu.PrefetchScalarGridSpec(
        num_scalar_prefetch=1,
        grid=(nseq, n_pages // pp),
        in_specs=[pl.BlockSpec((1, qs * IDX_HEADS, IDX_DIM), lambda b, c, pt: (b, 0, 0)),
                  pl.BlockSpec((1, qs * IDX_HEADS, 1), lambda b, c, pt: (b, 0, 0)),
                  pl.BlockSpec((1, IDX_DIM, PAGE_SIZE), lambda b, c, pt: (b, 0, 0))]
                 + [page_spec(j) for j in range(pp)],
        out_specs=pl.BlockSpec((n_pages + 1, nrow, LANES), lambda b, c, pt: (0, 0, 0)),
        scratch_shapes=[pltpu.VMEM((n_pages + 1, nrow, LANES), I32),
                        pltpu.VMEM((nrow, LANES), I32)],
    )
    return pl.pallas_call(
        functools.partial(_idx_sample_kernel, pp=pp, n_pages=n_pages, topk=topk, t_new=t_new),
        grid_spec=grid_spec,
        out_shape=jax.ShapeDtypeStruct((n_pages + 1, nrow, LANES), F32),
        compiler_params=_cparams("arbitrary", "arbitrary"),
        name="idx_sample",
    )(page_table, iq_rows, iw_col, ik_new_t, *([cache_kidx_t] * pp))


def _attn_sample_kernel(pt_ref, q_ref, mask_ref, masknew_ref, knew_ref, vnew_ref, *rest, pp, t_new):
    k_refs = rest[:pp]
    v_refs = rest[pp:2 * pp]
    o_ref = rest[2 * pp]
    m_ref, l_ref, acc_ref = rest[2 * pp + 1:]
    del pt_ref
    c = pl.program_id(1)
    nrow = t_new * A_HEADS
    q = q_ref[0]

    @pl.when(c == 0)
    def _():
        m_ref[...] = jnp.full(m_ref.shape, MASK_VALUE, F32)
        l_ref[...] = jnp.zeros(l_ref.shape, F32)
        acc_ref[...] = jnp.zeros(acc_ref.shape, F32)

    def update(k_pages, v_pages, masks):
        s = []
        for kt, mk in zip(k_pages, masks):
            mrows = jnp.concatenate([jnp.broadcast_to(mk[t:t + 1, :], (A_HEADS, LANES)) for t in range(t_new)], axis=0)
            s.append(_dot(q, kt) + mrows)
        m_blk = s[0]
        for t in s[1:]:
            m_blk = jnp.maximum(m_blk, t)
        m_old = m_ref[...]
        m_new = jnp.maximum(m_old, jnp.max(m_blk, axis=1, keepdims=True))
        alpha = jnp.exp2(m_old - m_new)
        l_sum = jnp.zeros((nrow, LANES), F32)
        acc = alpha * acc_ref[...]
        for t, vt in zip(s, v_pages):
            e = jnp.exp2(t - m_new)
            l_sum = l_sum + e
            acc = acc + _nt(e.astype(BF16), vt)
        l_ref[...] = alpha * l_ref[...] + jnp.sum(l_sum, axis=1, keepdims=True)
        acc_ref[...] = acc
        m_ref[...] = m_new

    update([r[0, 0].astype(BF16) for r in k_refs], [r[0, 0].astype(BF16) for r in v_refs],
           [mask_ref[j] for j in range(pp)])

    @pl.when(c == pl.num_programs(1) - 1)
    def _():
        update([knew_ref[0]], [vnew_ref[0]], [masknew_ref[0]])
        o = acc_ref[...] / l_ref[...]
        grp = (_iota((nrow, 1), 0) % A_HEADS) // (A_HEADS // A_KV_HEADS)
        out = jnp.zeros((nrow, A_HEAD_DIM), F32)
        for g in range(A_KV_HEADS):
            out = out + jnp.where(grp == g, o[:, g * A_HEAD_DIM:(g + 1) * A_HEAD_DIM], 0.0)
        o_ref[0] = out


def _attn_sample(page_table, q_rows, mask, k_new_t, v_new_t, cache_k_t, cache_v_t, layer, pp, t_new):
    nseq, n_pages = page_table.shape
    kvw = A_KV_HEADS * A_HEAD_DIM
    nrow = t_new * A_HEADS

    def page_spec(j):
        return pl.BlockSpec((1, 1, kvw, PAGE_SIZE), lambda b, c, pt: (layer, pt[b, c * pp + j], 0, 0))

    grid_spec = pltpu.PrefetchScalarGridSpec(
        num_scalar_prefetch=1,
        grid=(nseq, n_pages // pp),
        in_specs=[pl.BlockSpec((1, nrow, kvw), lambda b, c, pt: (b, 0, 0)),
                  pl.BlockSpec((pp, SUBLANES, LANES), lambda b, c, pt: (c, b, 0)),
                  pl.BlockSpec((1, SUBLANES, LANES), lambda b, c, pt: (n_pages, b, 0)),
                  pl.BlockSpec((1, kvw, PAGE_SIZE), lambda b, c, pt: (b, 0, 0)),
                  pl.BlockSpec((1, kvw, PAGE_SIZE), lambda b, c, pt: (b, 0, 0))]
                 + [page_spec(j) for j in range(pp)] * 2,
        out_specs=pl.BlockSpec((1, nrow, A_HEAD_DIM), lambda b, c, pt: (b, 0, 0)),
        scratch_shapes=[pltpu.VMEM((nrow, 1), F32), pltpu.VMEM((nrow, 1), F32), pltpu.VMEM((nrow, kvw), F32)],
    )
    return pl.pallas_call(
        functools.partial(_attn_sample_kernel, pp=pp, t_new=t_new),
        grid_spec=grid_spec,
        out_shape=jax.ShapeDtypeStruct((nseq, nrow, A_HEAD_DIM), F32),
        compiler_params=_cparams("parallel", "arbitrary"),
        name="attn_sample",
    )(page_table, q_rows, mask, mask, k_new_t, v_new_t, *([cache_k_t] * pp), *([cache_v_t] * pp))


def _sprep_kernel(aq_ref, ak_ref, iq_ref, ikw_ref, qg_ref, kg_ref, qn_ref, kn_ref, iqs_ref, iw_ref):
    for c, t in enumerate(_head_rmsnorm(aq_ref[...], qg_ref[...])):
        qn_ref[:, c * LANES:(c + 1) * LANES] = (t * (A_HEAD_DIM ** -0.5 * LOG2_E)).astype(BF16)
    for c, t in enumerate(_head_rmsnorm(ak_ref[...], kg_ref[...])):
        kn_ref[:, c * LANES:(c + 1) * LANES] = t
    iqs_ref[...] = (iq_ref[...] * (IDX_DIM ** -0.5)).astype(BF16)
    iw_ref[...] = ikw_ref[...] * (IDX_HEADS ** -0.5)


def _sprep(p, q_gain, k_gain):
    n = p.shape[0]
    kvw = A_KV_HEADS * A_HEAD_DIM
    return pl.pallas_call(
        _sprep_kernel,
        grid=(1,),
        in_specs=[pl.BlockSpec((n, W), lambda i: (0, C_AQ // W)),
                  pl.BlockSpec((n, kvw), lambda i: (0, C_AK // kvw)),
                  pl.BlockSpec((n, W), lambda i: (0, C_IQ // W)),
                  pl.BlockSpec((n, LANES), lambda i: (0, C_IKW // LANES)),
                  pl.BlockSpec((1, LANES), lambda i: (0, 0)),
                  pl.BlockSpec((1, LANES), lambda i: (0, 0))],
        out_specs=[pl.BlockSpec((n, W), lambda i: (0, 0)), pl.BlockSpec((n, kvw), lambda i: (0, 0)),
                   pl.BlockSpec((n, W), lambda i: (0, 0)), pl.BlockSpec((n, LANES), lambda i: (0, 0))],
        out_shape=[jax.ShapeDtypeStruct((n, W), BF16), jax.ShapeDtypeStruct((n, kvw), F32),
                   jax.ShapeDtypeStruct((n, W), BF16), jax.ShapeDtypeStruct((n, LANES), F32)],
        compiler_params=_cparams("arbitrary"),
        name="sprep",
    )(p, p, p, p, q_gain, k_gain)


def _merge_kernel(oa_ref, ob_ref, oc_ref, od_ref, gates_ref, x_ref, wb_ref, wo_ref, n2_ref, x1_ref, h2_ref):
    merged = None
    for b, o_ref in enumerate((oa_ref, ob_ref, oc_ref, od_ref)):
        up = _dot(o_ref[...], wb_ref[b])
        t = _sigmoid(gates_ref[:, b * D_MODEL:(b + 1) * D_MODEL].astype(F32)) * up
        merged = t if merged is None else merged + t
    x1 = x_ref[...] + _dot(merged.astype(BF16), wo_ref[...])
    x1_ref[...] = x1
    ms = jnp.mean(x1 * x1, axis=-1, keepdims=True)
    h2_ref[...] = (x1 * lax.rsqrt(ms + RMS_EPS) * n2_ref[...]).astype(BF16)


def _merge(oa, ob, oc, od, p, x, w_branch, w_out, norm2, tm):
    n = x.shape[0]
    row = lambda wdt: pl.BlockSpec((tm, wdt), lambda i: (i, 0))
    return pl.pallas_call(
        _merge_kernel,
        grid=(n // tm,),
        in_specs=[row(W), row(W), row(W), row(W),
                  pl.BlockSpec((tm, N_BRANCH * D_MODEL), lambda i: (i, C_GATES // (N_BRANCH * D_MODEL))),
                  row(D_MODEL),
                  pl.BlockSpec((N_BRANCH, W, D_MODEL), lambda i: (0, 0, 0)),
                  pl.BlockSpec((D_MODEL, D_MODEL), lambda i: (0, 0)),
                  pl.BlockSpec((1, D_MODEL), lambda i: (0, 0))],
        out_specs=[row(D_MODEL), row(D_MODEL)],
        out_shape=[jax.ShapeDtypeStruct((n, D_MODEL), F32), jax.ShapeDtypeStruct((n, D_MODEL), BF16)],
        compiler_params=_cparams("parallel"),
        name="merge",
    )(oa, ob, oc, od, p, x, w_branch, w_out, norm2)


def _ffn_kernel(h_ref, x_ref, wg_ref, wu_ref, wd_ref, o_ref, acc_ref):
    j = pl.program_id(1)

    @pl.when(j == 0)
    def _():
        acc_ref[...] = x_ref[...]

    h = h_ref[...]
    act = _silu(_dot(h, wg_ref[...])) * _dot(h, wu_ref[...])
    acc_ref[...] += _dot(act.astype(BF16), wd_ref[...])

    @pl.when(j == pl.num_programs(1) - 1)
    def _():
        o_ref[...] = acc_ref[...]


def _ffn(h2, x1, w_gu, w_down, tm, fc):
    n = x1.shape[0]
    nf = D_FF // fc
    return pl.pallas_call(
        _ffn_kernel,
        grid=(n // tm, nf),
        in_specs=[pl.BlockSpec((tm, D_MODEL), lambda i, j: (i, 0)),
                  pl.BlockSpec((tm, D_MODEL), lambda i, j: (i, 0)),
                  pl.BlockSpec((D_MODEL, fc), lambda i, j: (0, j)),
                  pl.BlockSpec((D_MODEL, fc), lambda i, j: (0, nf + j)),
                  pl.BlockSpec((fc, D_MODEL), lambda i, j: (j, 0))],
        out_specs=pl.BlockSpec((tm, D_MODEL), lambda i, j: (i, 0)),
        out_shape=jax.ShapeDtypeStruct((n, D_MODEL), F32),
        scratch_shapes=[pltpu.VMEM((tm, D_MODEL), F32)],
        compiler_params=_cparams("parallel", "arbitrary"),
        name="ffn",
    )(h2, x1, w_gu, w_gu, w_down)


def _pack_w_in(w_in):
    (a_q, a_k, a_v, a_iq, a_ik, a_iw, r_q, r_k, r_v, r_g, c_q, c_f, c_i, c_g,
     m_z, m_xbc, m_dt, gates) = jnp.split(w_in, SPLIT_POINTS, axis=-1)

    def pad_to_lanes(a):
        return jnp.pad(a, ((0, 0), (0, 0), (0, LANES - a.shape[-1])))

    ikw = pad_to_lanes(jnp.concatenate([a_ik, a_iw], axis=-1))
    packed = jnp.concatenate([gates, m_xbc, a_q, a_iq, r_q, r_k, r_v, r_g, c_q, c_f, c_i, c_g, m_z,
                              a_k, a_v, ikw, pad_to_lanes(m_dt)], axis=-1)
    return packed.astype(BF16)


def _rotary_tables(pos):
    half = R_DK // 2
    inv_freq = ROPE_BASE ** (-jnp.arange(half, dtype=F32) / half)
    ang = pos.astype(F32)[:, None] * inv_freq[None, :]
    c, s = jnp.cos(ang), jnp.sin(ang)
    return jnp.concatenate([c, c], axis=-1), jnp.concatenate([-s, s], axis=-1)


def _lower_bounds(lb_param):
    pr = jax.nn.softmax(lb_param.astype(F32), axis=0)
    return jnp.cumsum(pr, axis=0) - pr[0]


def _pad_lanes_row(v):
    return jnp.pad(v, (0, LANES - v.shape[0]))[None, :]


def kernel(x_prompt, x_sample, cache_k, cache_v, cache_kidx, state_ret, state_hgrn, state_ssm, state_conv,
           page_table, norm1_g, w_in, q_norm_g, k_norm_g, lb_param, hgrn_norm_g, A_log, dt_bias, D_skip,
           conv_w, conv_b, ssm_norm_g, w_branch, w_out, norm2_g, w_gu, w_down):
    bp, tp, _ = x_prompt.shape
    bs, ts, _ = x_sample.shape
    depth = w_in.shape[0]
    n_pool = cache_k.shape[1]
    n_pages = page_table.shape[1]
    kvw = A_KV_HEADS * A_HEAD_DIM
    tsp = SUBLANES
    npair = M_HEADS // 2

    w_in_p = _pack_w_in(w_in)
    w_branch_b = w_branch.astype(BF16)
    w_out_b = w_out.astype(BF16)
    w_gu_b = w_gu.astype(BF16)
    w_down_b = w_down.astype(BF16)
    lbs = _lower_bounds(lb_param)
    cos_p, sin_p = _rotary_tables(jnp.arange(tp, dtype=jnp.int32))
    pos_s = n_pages * PAGE_SIZE + jnp.arange(tsp, dtype=jnp.int32)
    cos_s, sin_s = _rotary_tables(pos_s)
    ck_t = jnp.transpose(cache_k, (0, 1, 3, 4, 2)).reshape(depth, n_pool, kvw, PAGE_SIZE)
    cv_t = jnp.transpose(cache_v, (0, 1, 3, 4, 2)).reshape(depth, n_pool, kvw, PAGE_SIZE)
    ckidx_t = jnp.transpose(cache_kidx, (0, 1, 3, 2))
    head_group = (jnp.arange(A_HEADS) // (A_HEADS // A_KV_HEADS))[:, None] == jnp.arange(A_KV_HEADS)[None, :]

    zero_ret = jnp.zeros((bp, R_HEADS, R_DK, R_DV), F32)
    zero_hgrn = jnp.zeros((bp, H_HEADS, H_DK, H_DV), F32)
    zero_ssm = jnp.zeros((bp, npair, 2 * M_HEAD_DIM, M_STATE), F32)
    zero_buf = jnp.zeros((bp, SUBLANES, M_CONV_DIM), F32)

    xp = x_prompt.reshape(bp * tp, D_MODEL)
    xs = x_sample.reshape(bs * ts, D_MODEL)
    outs_p = [[] for _ in range(7)]
    outs_s = [[] for _ in range(7)]
    for l in range(depth):
        n1 = norm1_g[l][None, :]
        n2 = norm2_g[l][None, :]
        qg = jnp.tile(q_norm_g[l], 2)[None, :]
        kg = jnp.tile(k_norm_g[l], 2)[None, :]
        lb = lbs[l][None, :]
        hng = hgrn_norm_g[l][None, :]
        alog = _pad_lanes_row(A_log[l])
        dtb = _pad_lanes_row(dt_bias[l])
        dsk = jnp.repeat(D_skip[l], M_HEAD_DIM)[None, :]
        sng = ssm_norm_g[l][None, :]
        cw = conv_w[l]
        cb = conv_b[l][None, :]

        p = _inproj(xp, n1, w_in_p[l], 1024, 1280, BF16)
        kn, vv, ik, knb, vb, ik2 = _kprep(p, kg, 512)
        oa = _attn_prompt(p, knb, vb, ik2, qg, bp, tp, 256)
        ob, ret_new = _retention(p, cos_p, sin_p, zero_ret, bp, tp, 256, R_CHUNK, R_CHUNK)
        oc, hgrn_new = _hgrn2(p, lb, hng, zero_hgrn, bp, tp, 256, H_CHUNK, H_CHUNK, 16)
        od, ssm_new, buf_new = _ssd(p, cw, cb, alog, dtb, dsk, sng, zero_ssm, zero_buf, bp, tp, 256, M_CHUNK, M_CHUNK)
        x1, h2 = _merge(oa, ob, oc, od, p, xp, w_branch_b[l], w_out_b[l], n2, 512)
        xp = _ffn(h2, x1, w_gu_b[l], w_down_b[l], 512, D_FF // 2)
        for i, a in enumerate((kn.reshape(bp, tp, A_KV_HEADS, A_HEAD_DIM), vv.reshape(bp, tp, A_KV_HEADS, A_HEAD_DIM),
                               ik.reshape(bp, tp, IDX_DIM), ret_new, hgrn_new,
                               ssm_new.reshape(bp, M_HEADS, M_HEAD_DIM, M_STATE),
                               buf_new[:, SUBLANES - (M_CONV - 1):, :])):
            outs_p[i].append(a)

        ps = _inproj(xs, n1, w_in_p[l], bs * ts, 1280, F32)
        qn_s, kn_s, iq_s, iw_s = _sprep(ps, qg, kg)
        v_s = ps[:, C_AV:C_AV + kvw]
        ik_s = ps[:, C_IKW:C_IKW + IDX_DIM]
        pad_q = lambda a: jnp.pad(a, ((0, 0), (0, tsp - ts)) + ((0, 0),) * (a.ndim - 2))
        new_keys_t = lambda a: jnp.swapaxes(jnp.pad(a.reshape(bs, ts, -1), ((0, 0), (0, PAGE_SIZE - ts), (0, 0))), 1, 2)
        head_major = lambda a: jnp.swapaxes(pad_q(a), 1, 2)
        iq_rows = head_major(iq_s.reshape(bs, ts, IDX_HEADS, IDX_DIM)).reshape(bs, IDX_HEADS * tsp, IDX_DIM)
        iw_col = head_major(iw_s[:, IDX_DIM:IDX_DIM + IDX_HEADS].reshape(bs, ts, IDX_HEADS)).reshape(bs, IDX_HEADS * tsp, 1)
        mask = _idx_sample(page_table, iq_rows, iw_col, new_keys_t(ik_s).astype(BF16), ckidx_t, l, min(8, n_pages), ts)
        q4 = qn_s.reshape(bs, ts, A_HEADS, 1, A_HEAD_DIM)
        q_rows = jnp.where(head_group[None, None, :, :, None], q4, jnp.zeros_like(q4)).reshape(bs, ts * A_HEADS, kvw)
        oa_s = _attn_sample(page_table, q_rows, mask, new_keys_t(kn_s).astype(BF16), new_keys_t(v_s).astype(BF16),
                            ck_t, cv_t, l, min(16, n_pages), ts)
        oa_s = oa_s.reshape(bs * ts, W).astype(BF16)
        psp = pad_q(ps.reshape(bs, ts, NP)).reshape(bs * tsp, NP)
        ssm0 = state_ssm[l].reshape(bs, npair, 2 * M_HEAD_DIM, M_STATE)
        buf0 = jnp.pad(state_conv[l], ((0, 0), (SUBLANES - (M_CONV - 1), 0), (0, 0)))
        ob_s, ret_s = _retention(psp, cos_s, sin_s, state_ret[l], bs, tsp, tsp, tsp, ts)
        oc_s, hgrn_s = _hgrn2(psp, lb, hng, state_hgrn[l], bs, tsp, tsp, tsp, ts, tsp)
        od_s, ssm_s, buf_s = _ssd(psp, cw, cb, alog, dtb, dsk, sng, ssm0, buf0, bs, tsp, tsp, tsp, ts)
        unpad = lambda a: a.reshape(bs, tsp, W)[:, :ts].reshape(bs * ts, W)
        x1s, h2s = _merge(oa_s, unpad(ob_s), unpad(oc_s), unpad(od_s), ps, xs, w_branch_b[l], w_out_b[l], n2, bs * ts)
        xs = _ffn(h2s, x1s, w_gu_b[l], w_down_b[l], bs * ts, D_FF // 2)
        for i, a in enumerate((kn_s.reshape(bs, ts, A_KV_HEADS, A_HEAD_DIM), v_s.reshape(bs, ts, A_KV_HEADS, A_HEAD_DIM),
                               ik_s.reshape(bs, ts, IDX_DIM), ret_s, hgrn_s,
                               ssm_s.reshape(bs, M_HEADS, M_HEAD_DIM, M_STATE),
                               buf_s[:, SUBLANES - (M_CONV - 1):, :])):
            outs_s[i].append(a)

    dtypes = (cache_k.dtype, cache_v.dtype, cache_kidx.dtype, state_ret.dtype,
              state_hgrn.dtype, state_ssm.dtype, state_conv.dtype)
    res_p = [jnp.stack(a).astype(d) for a, d in zip(outs_p, dtypes)]
    res_s = [jnp.stack(a).astype(d) for a, d in zip(outs_s, dtypes)]
    return (xp.reshape(bp, tp, D_MODEL), xs.reshape(bs, ts, D_MODEL), *res_p, *res_s)
```

```python
import functools
import math
import struct

import jax
import jax.numpy as jnp
from jax import lax
from jax.experimental import pallas as pl
from jax.experimental.pallas import tpu as pltpu

F32 = jnp.float32
BF16 = jnp.bfloat16
I32 = jnp.int32

D_MODEL = 1024
PAGE_SIZE = 128
BRANCH_WIDTH = D_MODEL // 2
N_BRANCH = 4
A_HEAD_DIM = 64
A_HEADS = BRANCH_WIDTH // A_HEAD_DIM
A_KV_HEADS = A_HEADS // 2
IDX_HEADS = 8
IDX_DIM = 64
TOPK_MAX = 256
Q_BLOCK = 128
R_HEADS = 4
R_DK = BRANCH_WIDTH // R_HEADS
R_DV = BRANCH_WIDTH // R_HEADS
R_CHUNK = 128
ROPE_BASE = 10000.0
H_HEADS = 4
H_DK = 128
H_DV = BRANCH_WIDTH // H_HEADS
H_CHUNK = 64
MIN_FORGET = 1e-30
M_HEAD_DIM = 64
M_HEADS = BRANCH_WIDTH // M_HEAD_DIM
M_INNER = M_HEADS * M_HEAD_DIM
M_GROUPS = 2
M_STATE = 128
M_CONV = 4
M_CHUNK = 128
M_CONV_DIM = M_INNER + 2 * M_GROUPS * M_STATE
D_FF = -(-8 * D_MODEL // (3 * 256)) * 256
RMS_EPS = 1e-6
GN_EPS = 1e-6
MASK_VALUE = -1e30

SPLIT_SIZES = (
    A_HEADS * A_HEAD_DIM, A_KV_HEADS * A_HEAD_DIM, A_KV_HEADS * A_HEAD_DIM,
    IDX_HEADS * IDX_DIM, IDX_DIM, IDX_HEADS,
    R_HEADS * R_DK, R_HEADS * R_DK, R_HEADS * R_DV, R_HEADS * R_DV,
    H_HEADS * H_DK, H_HEADS * H_DK, H_HEADS * H_DV, H_HEADS * H_DV,
    M_INNER, M_CONV_DIM, M_HEADS,
    N_BRANCH * D_MODEL,
)
SPLIT_POINTS = tuple(sum(SPLIT_SIZES[:i + 1]) for i in range(len(SPLIT_SIZES) - 1))

LANES = 128
SUBLANES = 8
VMEM_LIMIT = 56 * 1024 * 1024

C_GATES = 0
C_XBC = 4096
C_AQ = 5120
C_IQ = 5632
C_RQ = 6144
C_RK = 6656
C_RV = 7168
C_RG = 7680
C_CQ = 8192
C_CF = 8704
C_CI = 9216
C_CG = 9728
C_MZ = 10240
C_AK = 10752
C_AV = 11008
C_IKW = 11264
C_DT = 11392
NP = 11520
W = BRANCH_WIDTH


def _float_key(x):
    b = struct.unpack("<i", struct.pack("<f", x))[0]
    return b if b >= 0 else b ^ 0x7FFFFFFF


VALID_KEY = _float_key(0.5 * MASK_VALUE)
INT_MIN = -2 ** 31
LOG2_E = math.log2(math.e)


def _cparams(*sem):
    return pltpu.CompilerParams(dimension_semantics=sem, vmem_limit_bytes=VMEM_LIMIT)


def _nt(a, b):
    return lax.dot_general(a, b, (((1,), (1,)), ((), ())), preferred_element_type=F32)


def _tn(a, b):
    return lax.dot_general(a, b, (((0,), (0,)), ((), ())), preferred_element_type=F32)


def _dot(a, b):
    return jnp.dot(a, b, preferred_element_type=F32)


def _dot_exact_lhs(a01, x):
    a = a01.astype(BF16)
    x0 = x.astype(BF16)
    r1 = x - x0.astype(F32)
    x1 = r1.astype(BF16)
    x2 = (r1 - x1.astype(F32)).astype(BF16)
    return _dot(a, x0) + _dot(a, x1) + _dot(a, x2)


def _dot_exact_rhs(x, b01):
    b = b01.astype(BF16)
    x0 = x.astype(BF16)
    r1 = x - x0.astype(F32)
    x1 = r1.astype(BF16)
    x2 = (r1 - x1.astype(F32)).astype(BF16)
    return _dot(x0, b) + _dot(x1, b) + _dot(x2, b)


def _sigmoid(x):
    return 1.0 / (1.0 + jnp.exp(-x))


def _silu(x):
    return x * _sigmoid(x)


def _iota(shape, dim):
    return lax.broadcasted_iota(I32, shape, dim)


def _tri_incl(n):
    return (_iota((n, n), 0) >= _iota((n, n), 1)).astype(F32)


def _inproj_kernel(x_ref, g_ref, w_ref, o_ref, h_ref):
    @pl.when(pl.program_id(1) == 0)
    def _():
        x = x_ref[...]
        ms = jnp.mean(x * x, axis=-1, keepdims=True)
        h_ref[...] = (x * lax.rsqrt(ms + RMS_EPS) * g_ref[...]).astype(BF16)

    o_ref[...] = _dot(h_ref[...], w_ref[...]).astype(o_ref.dtype)


def _inproj(x, g, w, layer, tm, tn, out_dtype):
    n = x.shape[0]
    return pl.pallas_call(
        _inproj_kernel,
        grid=(n // tm, NP // tn),
        in_specs=[pl.BlockSpec((tm, D_MODEL), lambda i, j: (i, 0)),
                  pl.BlockSpec((1, D_MODEL), lambda i, j: (0, 0)),
                  pl.BlockSpec((None, D_MODEL, tn), lambda i, j: (layer, 0, j))],
        out_specs=pl.BlockSpec((tm, tn), lambda i, j: (i, j)),
        out_shape=jax.ShapeDtypeStruct((n, NP), out_dtype),
        scratch_shapes=[pltpu.VMEM((tm, D_MODEL), BF16)],
        compiler_params=_cparams("parallel", "arbitrary"),
        name="inproj",
    )(x, g, w)


def _head_rmsnorm(x, gain_row):
    pair = (_iota((LANES, LANES), 0) // A_HEAD_DIM == _iota((LANES, LANES), 1) // A_HEAD_DIM).astype(F32)
    outs = []
    for c in range(x.shape[1] // LANES):
        xc = x[:, c * LANES:(c + 1) * LANES]
        ms = _dot_exact_rhs(xc * xc, pair) * (1.0 / A_HEAD_DIM)
        outs.append(xc * lax.rsqrt(ms + RMS_EPS) * gain_row)
    return outs


def _kth_largest_key(count_ge, k, shape):
    nonneg = count_ge(jnp.zeros(shape, I32)) >= k
    prefix = jnp.where(nonneg, 0, INT_MIN).astype(I32)

    def body(t, prefix):
        cand = prefix + jnp.left_shift(jnp.int32(1), 30 - t)
        return jnp.where(count_ge(cand) >= k, cand, prefix)

    return lax.fori_loop(0, 31, body, prefix)


def _to_key(s):
    b = lax.bitcast_convert_type(s, I32)
    return jnp.where(b >= 0, b, b ^ 0x7FFFFFFF)


def _kprep_kernel(ak_ref, av_ref, ikw_ref, kg_ref, kn_ref, v_ref, ik_ref, knb_ref, vb_ref, ik2_ref):
    kn = _head_rmsnorm(ak_ref[...].astype(F32), kg_ref[...])
    for c, t in enumerate(kn):
        kn_ref[:, c * LANES:(c + 1) * LANES] = t
        knb_ref[:, c * LANES:(c + 1) * LANES] = t.astype(BF16)
    v = av_ref[...].astype(F32)
    v_ref[...] = v
    vb_ref[...] = v.astype(BF16)
    ikw = ikw_ref[...].astype(F32)
    ik_ref[...] = ikw[:, :IDX_DIM]
    lane = _iota(ikw.shape, 1)
    ik2_ref[...] = jnp.where(lane < IDX_DIM, ikw, pltpu.roll(ikw, IDX_DIM, 1)).astype(BF16)


def _kprep(p, k_gain, tm):
    n = p.shape[0]
    kvw = A_KV_HEADS * A_HEAD_DIM
    row = lambda wdt: pl.BlockSpec((tm, wdt), lambda i: (i, 0))
    return pl.pallas_call(
        _kprep_kernel,
        grid=(n // tm,),
        in_specs=[pl.BlockSpec((tm, kvw), lambda i: (i, C_AK // kvw)),
                  pl.BlockSpec((tm, kvw), lambda i: (i, C_AV // kvw)),
                  pl.BlockSpec((tm, LANES), lambda i: (i, C_IKW // LANES)),
                  pl.BlockSpec((1, LANES), lambda i: (0, 0))],
        out_specs=[row(kvw), row(kvw), row(IDX_DIM), row(kvw), row(kvw), row(LANES)],
        out_shape=[jax.ShapeDtypeStruct((n, kvw), F32),
                   jax.ShapeDtypeStruct((n, kvw), F32),
                   jax.ShapeDtypeStruct((n, IDX_DIM), F32),
                   jax.ShapeDtypeStruct((n, kvw), BF16),
                   jax.ShapeDtypeStruct((n, kvw), BF16),
                   jax.ShapeDtypeStruct((n, LANES), BF16)],
        compiler_params=_cparams("parallel"),
        name="kprep",
    )(p, p, p, k_gain)


def _attn_prompt_kernel(aq_ref, iq_ref, ikw_ref, qg_ref, knb_ref, vb_ref, ik2_ref, o_ref,
                        key_ref, msk_ref, s_ref, qt_ref, iqt_ref, acc_ref, cut_ref, *, topk, kb, qb):
    i = pl.program_id(1)
    nblk = (i * qb + qb + kb - 1) // kb
    row = _iota((kb, qb), 0)
    lane = _iota((kb, qb), 1)
    lane_q = _iota((qb, LANES), 1)
    own_half = [lane_q < A_HEAD_DIM, lane_q >= A_HEAD_DIM]
    n_pairs = A_KV_HEADS // 2
    hpp = A_HEADS // n_pairs
    nsub = kb // SUBLANES

    qn = _head_rmsnorm(aq_ref[...].astype(F32), qg_ref[...])
    for h in range(A_HEADS):
        src, dst = h % 2, (h // 2) % 2
        t = qn[h // 2] * (A_HEAD_DIM ** -0.5 * LOG2_E)
        if src != dst:
            t = pltpu.roll(t, A_HEAD_DIM, 1)
        t = jnp.where(own_half[dst], t, 0.0)
        qt_ref[h // hpp, :, (h % hpp) * qb:(h % hpp + 1) * qb] = t.T.astype(BF16)
    for h in range(IDX_HEADS):
        t = iq_ref[:, (h // 2) * LANES:(h // 2 + 1) * LANES].astype(F32) * (IDX_DIM ** -0.5)
        iqt_ref[:, h * qb:(h + 1) * qb] = jnp.where(own_half[h % 2], t, 0.0).T.astype(BF16)
    iw_t = ikw_ref[...].astype(F32).T[IDX_DIM:IDX_DIM + IDX_HEADS, :] * (IDX_HEADS ** -0.5)

    def score_blk(j, carry):
        ks = pl.multiple_of(j * kb, kb)
        lg = _dot(ik2_ref[pl.ds(ks, kb), :], iqt_ref[...])
        acc = jnp.zeros((kb, qb), F32)
        for h in range(IDX_HEADS):
            acc = acc + jnp.maximum(lg[:, h * qb:(h + 1) * qb], 0.0) * iw_t[h:h + 1, :]
        adm = (ks + row) <= (i * qb + lane)
        key_ref[j] = _to_key(jnp.where(adm, acc, MASK_VALUE))
        return carry

    lax.fori_loop(0, nblk, score_blk, 0)

    def count(pred):
        def body(j, c):
            return c + jnp.sum(jnp.where(pred(key_ref[j], j), 1, 0).reshape(nsub, SUBLANES, qb), axis=0)
        c = lax.fori_loop(0, nblk, body, jnp.zeros((SUBLANES, qb), I32))
        return jnp.sum(c, axis=0, keepdims=True)

    cut_ref[0:1, :] = jnp.full((1, qb), INT_MIN, I32)
    cut_ref[1:2, :] = jnp.full((1, qb), 2 ** 30, I32)

    @pl.when(i * qb + qb > topk)
    def _():
        thr = _kth_largest_key(lambda cand: count(lambda blk, j: blk >= cand), topk, (1, qb))
        cut_ref[0:1, :] = thr
        n_gt = count(lambda blk, j: blk > thr)
        n_eq = count(lambda blk, j: blk == thr)
        need = topk - n_gt

        @pl.when(jnp.max(n_eq - need) > 0)
        def _():
            def body(t, ans):
                cand = ans + jnp.left_shift(jnp.int32(1), 15 - t)
                upto = count(lambda blk, j: (blk == thr) & ((j * kb + row) <= cand - 1))
                return jnp.where(upto < need, cand, ans)

            cut_ref[1:2, :] = lax.fori_loop(0, 16, body, jnp.zeros((1, qb), I32))

    thr = cut_ref[0:1, :]
    last = cut_ref[1:2, :]
    lo = jnp.maximum(thr, VALID_KEY)
    tie_ok = thr > VALID_KEY

    def mask_blk(j, carry):
        blk = key_ref[j]
        sel = (blk > lo) | ((blk == thr) & tie_ok & ((j * kb + row) <= last))
        msk_ref[j] = jnp.where(sel, 0.0, MASK_VALUE)
        return carry

    lax.fori_loop(0, nblk, mask_blk, 0)

    def fold(t, op):
        return op(t.reshape(nsub, SUBLANES, qb), axis=0)

    def logits_blk(j, m8):
        ks = pl.multiple_of(j * kb, kb)
        mk = msk_ref[j]
        out = []
        for pr in range(n_pairs):
            s4 = _dot(knb_ref[pl.ds(ks, kb), pr * LANES:(pr + 1) * LANES], qt_ref[pr])
            for hh in range(hpp):
                h = pr * hpp + hh
                s = s4[:, hh * qb:(hh + 1) * qb] + mk
                s_ref[j, h] = s
                out.append(jnp.maximum(m8[h], fold(s, jnp.max)))
        return tuple(out)

    m8 = lax.fori_loop(0, nblk, logits_blk, tuple(jnp.full((SUBLANES, qb), MASK_VALUE, F32) for _ in range(A_HEADS)))
    m_row = [jnp.max(t, axis=0, keepdims=True) for t in m8]

    for pr in range(n_pairs):
        acc_ref[pr] = jnp.zeros((hpp * qb, LANES), F32)

    def values_blk(j, l8):
        ks = pl.multiple_of(j * kb, kb)
        out = []
        for pr in range(n_pairs):
            e_t = []
            for hh in range(hpp):
                h = pr * hpp + hh
                e = jnp.exp2(s_ref[j, h] - m_row[h])
                out.append(l8[h] + fold(e, jnp.sum))
                e_t.append(e.T.astype(BF16))
            lhs = jnp.concatenate(e_t, axis=0)
            acc_ref[pr] = acc_ref[pr] + _dot(lhs, vb_ref[pl.ds(ks, kb), pr * LANES:(pr + 1) * LANES])
        return tuple(out)

    l8 = lax.fori_loop(0, nblk, values_blk, tuple(jnp.zeros((SUBLANES, qb), F32) for _ in range(A_HEADS)))

    for h in range(A_HEADS):
        pr, hh, half = h // hpp, h % hpp, (h // 2) % 2
        l_row = jnp.sum(l8[h], axis=0, keepdims=True)
        l_col = jnp.broadcast_to(l_row, (SUBLANES, qb)).T[:, 0:1]
        o = acc_ref[pr, hh * qb:(hh + 1) * qb, half * A_HEAD_DIM:(half + 1) * A_HEAD_DIM] / l_col
        o_ref[:, h * A_HEAD_DIM:(h + 1) * A_HEAD_DIM] = o.astype(o_ref.dtype)


def _attn_prompt(p, knb, vb, ik2, q_gain, batch, seq, kb, qb):
    nqb = seq // qb
    nkb = seq // kb
    kvw = A_KV_HEADS * A_HEAD_DIM
    topk = min(TOPK_MAX, seq // 4)
    n_pairs = A_KV_HEADS // 2
    hpp = A_HEADS // n_pairs
    return pl.pallas_call(
        functools.partial(_attn_prompt_kernel, topk=topk, kb=kb, qb=qb),
        grid=(batch, nqb),
        in_specs=[pl.BlockSpec((qb, W), lambda b, i: (b * nqb + i, C_AQ // W)),
                  pl.BlockSpec((qb, W), lambda b, i: (b * nqb + i, C_IQ // W)),
                  pl.BlockSpec((qb, LANES), lambda b, i: (b * nqb + i, C_IKW // LANES)),
                  pl.BlockSpec((1, LANES), lambda b, i: (0, 0)),
                  pl.BlockSpec((seq, kvw), lambda b, i: (b, 0)),
                  pl.BlockSpec((seq, kvw), lambda b, i: (b, 0)),
                  pl.BlockSpec((seq, LANES), lambda b, i: (b, 0))],
        out_specs=pl.BlockSpec((qb, W), lambda b, i: (b * nqb + i, 0)),
        out_shape=jax.ShapeDtypeStruct((batch * seq, W), BF16),
        scratch_shapes=[pltpu.VMEM((nkb, kb, qb), I32),
                        pltpu.VMEM((nkb, kb, qb), F32),
                        pltpu.VMEM((nkb, A_HEADS, kb, qb), F32),
                        pltpu.VMEM((n_pairs, LANES, hpp * qb), BF16),
                        pltpu.VMEM((LANES, IDX_HEADS * qb), BF16),
                        pltpu.VMEM((n_pairs, hpp * qb, LANES), F32),
                        pltpu.VMEM((SUBLANES, qb), I32)],
        compiler_params=_cparams("parallel", "arbitrary"),
        name="attn_prompt",
    )(p, p, p, q_gain, knb, vb, ik2)


def _ret_kernel(q_ref, k_ref, v_ref, g_ref, cos_ref, sin_ref, s0_ref, o_ref, s_out_ref, st_ref,
                *, cp, c_true):
    tb = pl.program_id(1)

    @pl.when(tb == 0)
    def _():
        st_ref[...] = s0_ref[0]

    row = _iota((cp, 1), 0).astype(F32)
    diff = (_iota((cp, cp), 0) - _iota((cp, cp), 1)).astype(F32)

    def chunk(c, carry):
        r0 = pl.multiple_of(c * cp, cp)
        rows = pl.ds(r0, cp)
        cos = cos_ref[rows, :]
        sin = sin_ref[rows, :]
        for h in range(R_HEADS):
            lg = math.log1p(-2.0 ** (-5.0 - h))
            cols = slice(h * R_DK, (h + 1) * R_DK)
            q = q_ref[rows, cols].astype(F32)
            k = k_ref[rows, cols].astype(F32)
            v = v_ref[rows, cols].astype(F32)
            q = q * cos + pltpu.roll(q, R_DK // 2, 1) * sin
            k = (k * cos + pltpu.roll(k, R_DK // 2, 1) * sin) * (R_DK ** -0.5)
            decay = jnp.where(diff >= 0, jnp.exp(lg * jnp.maximum(diff, 0.0)), 0.0)
            scores = _nt(q.astype(BF16), k.astype(BF16)) * decay
            s = st_ref[h]
            o = _dot(scores.astype(BF16), v.astype(BF16))
            o = o + _dot((q * jnp.exp(lg * (row + 1.0))).astype(BF16), s.astype(BF16))
            kd = jnp.where(row < c_true, jnp.exp(lg * jnp.maximum(c_true - 1.0 - row, 0.0)), 0.0)
            st_ref[h] = math.exp(lg * c_true) * s + _tn((k * kd).astype(BF16), v.astype(BF16))
            mu = jnp.mean(o, axis=-1, keepdims=True)
            d = o - mu
            var = jnp.mean(d * d, axis=-1, keepdims=True)
            o_ref[rows, cols] = (_silu(g_ref[rows, cols].astype(F32)) * (d * lax.rsqrt(var + GN_EPS))).astype(o_ref.dtype)
        return carry

    lax.fori_loop(0, q_ref.shape[0] // cp, chunk, 0)

    @pl.when(tb == pl.num_programs(1) - 1)
    def _():
        s_out_ref[0] = st_ref[...]


def _hgrn_kernel(q_ref, f_ref, i_ref, g_ref, lb_ref, ng_ref, s0_ref, o_ref, s_out_ref, st_ref,
                 *, cp, c_true, sb):
    tb = pl.program_id(1)

    @pl.when(tb == 0)
    def _():
        st_ref[...] = s0_ref[0]

    tri = _tri_incl(cp)
    rowi = _iota((cp, 1), 0)
    real = rowi < c_true
    lane_a = _iota((sb, cp), 1)
    sub_n = _iota((sb, 1), 0)
    ng = ng_ref[...]

    def chunk(c, carry):
        r0 = pl.multiple_of(c * cp, cp)
        rows = pl.ds(r0, cp)
        for h in range(H_HEADS):
            cols = slice(h * H_DK, (h + 1) * H_DK)
            lb = lb_ref[:, cols]
            q = _silu(q_ref[rows, cols].astype(F32)) * (H_DK ** -0.5)
            f = f_ref[rows, cols].astype(F32)
            forget = lb + (1.0 - lb) * _sigmoid(f)
            logf = jnp.where(real, jnp.log(jnp.maximum(forget, MIN_FORGET)), 0.0)
            k = jnp.where(real, (1.0 - lb) * _sigmoid(-f), 0.0)
            v = jnp.where(real, i_ref[rows, cols].astype(F32), 0.0)
            vb = v.astype(BF16)
            gc = _dot_exact_lhs(tri, logf)
            s = st_ref[h]
            cross = _dot((q * jnp.exp(gc)).astype(BF16), s.astype(BF16))
            g_last = gc[cp - 1:cp, :]
            kd = k * jnp.exp(g_last - gc)
            eg_col = jnp.broadcast_to(jnp.exp(g_last), (SUBLANES, H_DK)).T[:, 0:1]
            st_ref[h] = eg_col * s + _tn(kd.astype(BF16), vb)
            outs = []
            for b in range(cp // sb):
                b0 = b * sb
                gi = gc[b0:b0 + sb, :]
                qi = q[b0:b0 + sb, :]
                ki = k[b0:b0 + sb, :]
                if b > 0:
                    gref = gc[b0:b0 + 1, :]
                    ks = jnp.where(rowi < b0, k * jnp.exp(jnp.minimum(gref - gc, 0.0)), 0.0)
                    a = _nt((qi * jnp.exp(gi - gref)).astype(BF16), ks.astype(BF16))
                else:
                    a = jnp.zeros((sb, cp), F32)
                for m in range(sb):
                    e = jnp.exp(jnp.where(sub_n >= m, gi - gi[m:m + 1, :], MASK_VALUE))
                    col = jnp.sum(qi * e * ki[m:m + 1, :], axis=1, keepdims=True)
                    a = jnp.where(lane_a == b0 + m, col, a)
                outs.append(_dot(a.astype(BF16), vb) + cross[b0:b0 + sb, :])
            o = outs[0] if len(outs) == 1 else jnp.concatenate(outs, axis=0)
            ms = jnp.mean(o * o, axis=-1, keepdims=True)
            o = o * lax.rsqrt(ms + RMS_EPS) * ng
            o_ref[rows, cols] = (_silu(g_ref[rows, cols].astype(F32)) * o).astype(o_ref.dtype)
        return carry

    n_chunks = q_ref.shape[0] // cp
    lax.fori_loop(0, n_chunks, chunk, 0, unroll=2 if n_chunks % 2 == 0 else 1)

    @pl.when(tb == pl.num_programs(1) - 1)
    def _():
        s_out_ref[0] = st_ref[...]


def _softplus(x):
    return jnp.maximum(x, 0.0) + jnp.log1p(jnp.exp(-jnp.abs(x)))


def _ssd_kernel(z_ref, xbc_ref, dt_ref, cw_ref, cb_ref, alog_ref, dtb_ref, dskip_ref, ng_ref,
                h0_ref, buf0_ref, o_ref, h_out_ref, buf_out_ref, st_ref, tail_ref, ext_ref,
                *, cp, c_true):
    tb = pl.program_id(1)

    @pl.when(tb == 0)
    def _():
        st_ref[...] = h0_ref[0]
        tail_ref[...] = buf0_ref[0]

    tri = _tri_incl(cp)
    rowi = _iota((cp, 1), 0)
    real = rowi < c_true
    causal = _iota((cp, cp), 0) >= _iota((cp, cp), 1)
    lane = _iota((1, LANES), 1)
    left = lane < M_HEAD_DIM
    top = _iota((LANES, 1), 0) < M_HEAD_DIM
    cw = cw_ref[...]
    cb = cb_ref[...]
    neg_a = -jnp.exp(alog_ref[...])
    gw = M_INNER // M_GROUPS

    def chunk(c, carry):
        r0 = pl.multiple_of(c * cp, cp)
        rows = pl.ds(r0, cp)
        u = xbc_ref[rows, :].astype(F32)
        ext_ref[0:SUBLANES, :] = tail_ref[...]
        ext_ref[SUBLANES:SUBLANES + cp, :] = u
        y = cb + cw[M_CONV - 1:M_CONV, :] * u
        for j in range(M_CONV - 1):
            y = y + cw[j:j + 1, :] * ext_ref[pl.ds(SUBLANES - (M_CONV - 1) + j, cp), :]
        tail_ref[...] = ext_ref[pl.ds(c_true, SUBLANES), :]
        xbc = _silu(y)
        dt = jnp.where(real, _softplus(dt_ref[rows, :].astype(F32) + dtb_ref[...]), 0.0)
        a = neg_a * dt
        cum = _dot_exact_lhs(tri, a)
        cum_t = cum.T
        dt_t = dt.T
        e_cum = jnp.exp(cum)
        cum_last = cum[cp - 1:cp, :]
        w_all = jnp.exp(cum_last - cum) * dt
        e_last = jnp.exp(cum_last)
        z = z_ref[rows, :].astype(F32)
        ys = []
        for pr in range(M_HEADS // 2):
            g = (2 * pr) // (M_HEADS // M_GROUPS)
            bm = xbc[:, M_INNER + g * M_STATE:M_INNER + (g + 1) * M_STATE].astype(BF16)
            cm = xbc[:, M_INNER + (M_GROUPS + g) * M_STATE:M_INNER + (M_GROUPS + g + 1) * M_STATE].astype(BF16)
            xp = xbc[:, pr * LANES:(pr + 1) * LANES]
            xpb = xp.astype(BF16)
            cbm = _nt(cm, bm)
            hp = st_ref[pr]
            cross = _nt(cm, hp.astype(BF16))
            intra = []
            for hh in range(2):
                h = 2 * pr + hh
                seg = jnp.exp(jnp.where(causal, cum[:, h:h + 1] - cum_t[h:h + 1, :], MASK_VALUE))
                mat = cbm * seg * dt_t[h:h + 1, :]
                intra.append(_dot(mat.astype(BF16), xpb))
            h0, h1 = 2 * pr, 2 * pr + 1
            y_pair = (jnp.where(left, intra[0], intra[1])
                      + cross * jnp.where(left, e_cum[:, h0:h0 + 1], e_cum[:, h1:h1 + 1]))
            wx = xp * jnp.where(left, w_all[:, h0:h0 + 1], w_all[:, h1:h1 + 1])
            decay = jnp.where(top, e_last[:, h0:h0 + 1], e_last[:, h1:h1 + 1])
            st_ref[pr] = decay * hp + _tn(wx.astype(BF16), bm)
            y_pair = y_pair + dskip_ref[:, pr * LANES:(pr + 1) * LANES] * xp
            ys.append(y_pair * _silu(z[:, pr * LANES:(pr + 1) * LANES]))
        for g in range(M_GROUPS):
            yg = jnp.concatenate(ys[g * 2:(g + 1) * 2], axis=1)
            ms = jnp.mean(yg * yg, axis=-1, keepdims=True)
            o_ref[rows, g * gw:(g + 1) * gw] = (yg * lax.rsqrt(ms + RMS_EPS)
                                                * ng_ref[:, g * gw:(g + 1) * gw]).astype(o_ref.dtype)
        return carry

    lax.fori_loop(0, z_ref.shape[0] // cp, chunk, 0)

    @pl.when(tb == pl.num_programs(1) - 1)
    def _():
        h_out_ref[0] = st_ref[...]
        buf_out_ref[0] = tail_ref[...]


def _row_blocks(batch, rows_per_seq, tb):
    n_tb = rows_per_seq // tb
    return n_tb, (lambda col: (lambda b, t: (b * n_tb + t, col)))


def _retention(p, cos2, sin2, s0, batch, rows_per_seq, tb, cp, c_true):
    n_tb, at = _row_blocks(batch, rows_per_seq, tb)
    st_spec = pl.BlockSpec((1, R_HEADS, R_DK, R_DV), lambda b, t: (b, 0, 0, 0))
    return pl.pallas_call(
        functools.partial(_ret_kernel, cp=cp, c_true=c_true),
        grid=(batch, n_tb),
        in_specs=[pl.BlockSpec((tb, W), at(C_RQ // W)), pl.BlockSpec((tb, W), at(C_RK // W)),
                  pl.BlockSpec((tb, W), at(C_RV // W)), pl.BlockSpec((tb, W), at(C_RG // W)),
                  pl.BlockSpec((tb, R_DK), lambda b, t: (t, 0)), pl.BlockSpec((tb, R_DK), lambda b, t: (t, 0)),
                  st_spec],
        out_specs=[pl.BlockSpec((tb, W), at(0)), st_spec],
        out_shape=[jax.ShapeDtypeStruct((batch * rows_per_seq, W), BF16),
                   jax.ShapeDtypeStruct((batch, R_HEADS, R_DK, R_DV), F32)],
        scratch_shapes=[pltpu.VMEM((R_HEADS, R_DK, R_DV), F32)],
        compiler_params=_cparams("parallel", "arbitrary"),
        name="retention",
    )(p, p, p, p, cos2, sin2, s0)


def _hgrn2(p, lb, norm_g, s0, batch, rows_per_seq, tb, cp, c_true, sb):
    n_tb, at = _row_blocks(batch, rows_per_seq, tb)
    st_spec = pl.BlockSpec((1, H_HEADS, H_DK, H_DV), lambda b, t: (b, 0, 0, 0))
    return pl.pallas_call(
        functools.partial(_hgrn_kernel, cp=cp, c_true=c_true, sb=sb),
        grid=(batch, n_tb),
        in_specs=[pl.BlockSpec((tb, W), at(C_CQ // W)), pl.BlockSpec((tb, W), at(C_CF // W)),
                  pl.BlockSpec((tb, W), at(C_CI // W)), pl.BlockSpec((tb, W), at(C_CG // W)),
                  pl.BlockSpec((1, W), lambda b, t: (0, 0)), pl.BlockSpec((1, H_DV), lambda b, t: (0, 0)),
                  st_spec],
        out_specs=[pl.BlockSpec((tb, W), at(0)), st_spec],
        out_shape=[jax.ShapeDtypeStruct((batch * rows_per_seq, W), BF16),
                   jax.ShapeDtypeStruct((batch, H_HEADS, H_DK, H_DV), F32)],
        scratch_shapes=[pltpu.VMEM((H_HEADS, H_DK, H_DV), F32)],
        compiler_params=_cparams("parallel", "arbitrary"),
        name="hgrn2",
    )(p, p, p, p, lb, norm_g, s0)


def _ssd(p, conv_w, conv_b, a_log, dt_bias, d_skip, norm_g, h0, buf0, batch, rows_per_seq, tb, cp, c_true):
    n_tb, at = _row_blocks(batch, rows_per_seq, tb)
    npair = M_HEADS // 2
    st_spec = pl.BlockSpec((1, npair, 2 * M_HEAD_DIM, M_STATE), lambda b, t: (b, 0, 0, 0))
    buf_spec = pl.BlockSpec((1, SUBLANES, M_CONV_DIM), lambda b, t: (b, 0, 0))
    row = lambda wdt: pl.BlockSpec((1, wdt), lambda b, t: (0, 0))
    return pl.pallas_call(
        functools.partial(_ssd_kernel, cp=cp, c_true=c_true),
        grid=(batch, n_tb),
        in_specs=[pl.BlockSpec((tb, W), at(C_MZ // W)), pl.BlockSpec((tb, M_CONV_DIM), at(C_XBC // M_CONV_DIM)),
                  pl.BlockSpec((tb, LANES), at(C_DT // LANES)),
                  pl.BlockSpec((M_CONV, M_CONV_DIM), lambda b, t: (0, 0)), row(M_CONV_DIM),
                  row(LANES), row(LANES), row(W), row(W), st_spec, buf_spec],
        out_specs=[pl.BlockSpec((tb, W), at(0)), st_spec, buf_spec],
        out_shape=[jax.ShapeDtypeStruct((batch * rows_per_seq, W), BF16),
                   jax.ShapeDtypeStruct((batch, npair, 2 * M_HEAD_DIM, M_STATE), F32),
                   jax.ShapeDtypeStruct((batch, SUBLANES, M_CONV_DIM), F32)],
        scratch_shapes=[pltpu.VMEM((npair, 2 * M_HEAD_DIM, M_STATE), F32),
                        pltpu.VMEM((SUBLANES, M_CONV_DIM), F32),
                        pltpu.VMEM((cp + SUBLANES, M_CONV_DIM), F32)],
        compiler_params=_cparams("parallel", "arbitrary"),
        name="ssd",
    )(p, p, p, conv_w, conv_b, a_log, dt_bias, d_skip, norm_g, h0, buf0)


def _idx_sample_kernel(pt_ref, iq_ref, iw_ref, iknew_ref, *rest, pp, n_pages, topk, t_new):
    page_refs = rest[:pp]
    mask_ref = rest[pp]
    key_ref = rest[pp + 1]
    cut_ref = rest[pp + 2]
    del pt_ref
    b = pl.program_id(0)
    c = pl.program_id(1)
    qs = SUBLANES
    nrow = mask_ref.shape[1]
    iq = iq_ref[0]
    wcol = iw_ref[0]
    r0 = pl.multiple_of(b * qs, qs)

    def scores(keys_t):
        lg = _dot(iq, keys_t)
        sc = (jnp.maximum(lg, 0.0) * wcol).reshape(IDX_HEADS, qs, LANES).sum(axis=0)
        return sc + 0.0

    for j in range(pp):
        key_ref[c * pp + j, pl.ds(r0, qs), :] = _to_key(scores(page_refs[j][0, 0].astype(BF16)))

    @pl.when(c == pl.num_programs(1) - 1)
    def _():
        row = _iota((qs, LANES), 0)
        lane = _iota((qs, LANES), 1)
        sm = jnp.where(lane <= row, scores(iknew_ref[0]), MASK_VALUE)
        key_ref[n_pages, pl.ds(r0, qs), :] = jnp.where(lane < t_new, _to_key(sm), INT_MIN)

    @pl.when((c == pl.num_programs(1) - 1) & (b == pl.num_programs(0) - 1))
    def _():
        n_all = n_pages + 1
        lane = _iota((nrow, LANES), 1)

        def count(pred):
            def body(pg, acc):
                return acc + jnp.where(pred(key_ref[pg], pg), 1, 0)
            acc = lax.fori_loop(0, n_all, body, jnp.zeros((nrow, LANES), I32))
            return jnp.sum(acc, axis=1, keepdims=True)

        def count_ge(cand):
            cb = jnp.broadcast_to(cand, (nrow, LANES))
            return count(lambda blk, pg: blk >= cb)

        thr = jnp.broadcast_to(_kth_largest_key(count_ge, topk, (nrow, 1)), (nrow, LANES))
        n_gt = count(lambda blk, pg: blk > thr)
        n_eq = count(lambda blk, pg: blk == thr)
        need = topk - n_gt
        cut_ref[...] = jnp.full((nrow, LANES), 2 ** 30, I32)

        @pl.when(jnp.max(n_eq - need) > 0)
        def _():
            def body(t, ans):
                cand = ans + jnp.left_shift(jnp.int32(1), 15 - t)
                cb = jnp.broadcast_to(cand - 1, (nrow, LANES))
                upto = count(lambda blk, pg: (blk == thr) & ((pg * LANES + lane) <= cb))
                return jnp.where(upto < need, cand, ans)
            last = lax.fori_loop(0, 16, body, jnp.zeros((nrow, 1), I32))
            cut_ref[...] = jnp.broadcast_to(last, (nrow, LANES))

        last = cut_ref[...]
        lo = jnp.maximum(thr, VALID_KEY)
        tie_ok = thr > VALID_KEY

        def write(pg, carry):
            blk = key_ref[pg]
            sel = (blk > lo) | ((blk == thr) & tie_ok & ((pg * LANES + lane) <= last))
            mask_ref[pg] = jnp.where(sel, 0.0, MASK_VALUE)
            return carry

        lax.fori_loop(0, n_all, write, 0)


def _idx_sample(page_table, iq_rows, iw_col, ik_new_t, cache_kidx_t, layer, pp, t_new):
    nseq, n_pages = page_table.shape
    topk = min(TOPK_MAX, (n_pages * PAGE_SIZE + t_new) // 4)
    qs = SUBLANES
    nrow = nseq * qs

    def page_spec(j):
        return pl.BlockSpec((1, 1, IDX_DIM, PAGE_SIZE), lambda b, c, pt: (layer, pt[b, c * pp + j], 0, 0))

    grid_spec = pltpu.PrefetchScalarGridSpec(
        num_scalar_prefetch=1,
        grid=(nseq, n_pages // pp),
        in_specs=[pl.BlockSpec((1, qs * IDX_HEADS, IDX_DIM), lambda b, c, pt: (b, 0, 0)),
                  pl.BlockSpec((1, qs * IDX_HEADS, 1), lambda b, c, pt: (b, 0, 0)),
                  pl.BlockSpec((1, IDX_DIM, PAGE_SIZE), lambda b, c, pt: (b, 0, 0))]
                 + [page_spec(j) for j in range(pp)],
        out_specs=pl.BlockSpec((n_pages + 1, nrow, LANES), lambda b, c, pt: (0, 0, 0)),
        scratch_shapes=[pltpu.VMEM((n_pages + 1, nrow, LANES), I32),
                        pltpu.VMEM((nrow, LANES), I32)],
    )
    return pl.pallas_call(
        functools.partial(_idx_sample_kernel, pp=pp, n_pages=n_pages, topk=topk, t_new=t_new),
        grid_spec=grid_spec,
        out_shape=jax.ShapeDtypeStruct((n_pages + 1, nrow, LANES), F32),
        compiler_params=_cparams("arbitrary", "arbitrary"),
        name="idx_sample",
    )(page_table, iq_rows, iw_col, ik_new_t, *([cache_kidx_t] * pp))


def _attn_sample_kernel(pt_ref, q_ref, mask_ref, masknew_ref, knew_ref, vnew_ref, *rest, pp, t_new):
    k_refs = rest[:pp]
    v_refs = rest[pp:2 * pp]
    o_ref = rest[2 * pp]
    m_ref, l_ref, acc_ref = rest[2 * pp + 1:]
    del pt_ref
    c = pl.program_id(1)
    nrow = t_new * A_HEADS
    q = q_ref[0]

    @pl.when(c == 0)
    def _():
        m_ref[...] = jnp.full(m_ref.shape, MASK_VALUE, F32)
        l_ref[...] = jnp.zeros(l_ref.shape, F32)
        acc_ref[...] = jnp.zeros(acc_ref.shape, F32)

    def update(k_pages, v_pages, masks):
        s = []
        for kt, mk in zip(k_pages, masks):
            mrows = jnp.concatenate([jnp.broadcast_to(mk[t:t + 1, :], (A_HEADS, LANES)) for t in range(t_new)], axis=0)
            s.append(_dot(q, kt) + mrows)
        m_blk = s[0]
        for t in s[1:]:
            m_blk = jnp.maximum(m_blk, t)
        m_old = m_ref[...]
        m_new = jnp.maximum(m_old, jnp.max(m_blk, axis=1, keepdims=True))
        alpha = jnp.exp2(m_old - m_new)
        l_sum = jnp.zeros((nrow, LANES), F32)
        acc = alpha * acc_ref[...]
        for t, vt in zip(s, v_pages):
            e = jnp.exp2(t - m_new)
            l_sum = l_sum + e
            acc = acc + _nt(e.astype(BF16), vt)
        l_ref[...] = alpha * l_ref[...] + jnp.sum(l_sum, axis=1, keepdims=True)
        acc_ref[...] = acc
        m_ref[...] = m_new

    update([r[0, 0].astype(BF16) for r in k_refs], [r[0, 0].astype(BF16) for r in v_refs],
           [mask_ref[j] for j in range(pp)])

    @pl.when(c == pl.num_programs(1) - 1)
    def _():
        update([knew_ref[0]], [vnew_ref[0]], [masknew_ref[0]])
        o = acc_ref[...] / l_ref[...]
        grp = (_iota((nrow, 1), 0) % A_HEADS) // (A_HEADS // A_KV_HEADS)
        out = jnp.zeros((nrow, A_HEAD_DIM), F32)
        for g in range(A_KV_HEADS):
            out = out + jnp.where(grp == g, o[:, g * A_HEAD_DIM:(g + 1) * A_HEAD_DIM], 0.0)
        o_ref[0] = out


def _attn_sample(page_table, q_rows, mask, k_new_t, v_new_t, cache_k_t, cache_v_t, layer, pp, t_new):
    nseq, n_pages = page_table.shape
    kvw = A_KV_HEADS * A_HEAD_DIM
    nrow = t_new * A_HEADS

    def page_spec(j):
        return pl.BlockSpec((1, 1, kvw, PAGE_SIZE), lambda b, c, pt: (layer, pt[b, c * pp + j], 0, 0))

    grid_spec = pltpu.PrefetchScalarGridSpec(
        num_scalar_prefetch=1,
        grid=(nseq, n_pages // pp),
        in_specs=[pl.BlockSpec((1, nrow, kvw), lambda b, c, pt: (b, 0, 0)),
                  pl.BlockSpec((pp, SUBLANES, LANES), lambda b, c, pt: (c, b, 0)),
                  pl.BlockSpec((1, SUBLANES, LANES), lambda b, c, pt: (n_pages, b, 0)),
                  pl.BlockSpec((1, kvw, PAGE_SIZE), lambda b, c, pt: (b, 0, 0)),
                  pl.BlockSpec((1, kvw, PAGE_SIZE), lambda b, c, pt: (b, 0, 0))]
                 + [page_spec(j) for j in range(pp)] * 2,
        out_specs=pl.BlockSpec((1, nrow, A_HEAD_DIM), lambda b, c, pt: (b, 0, 0)),
        scratch_shapes=[pltpu.VMEM((nrow, 1), F32), pltpu.VMEM((nrow, 1), F32), pltpu.VMEM((nrow, kvw), F32)],
    )
    return pl.pallas_call(
        functools.partial(_attn_sample_kernel, pp=pp, t_new=t_new),
        grid_spec=grid_spec,
        out_shape=jax.ShapeDtypeStruct((nseq, nrow, A_HEAD_DIM), F32),
        compiler_params=_cparams("parallel", "arbitrary"),
        name="attn_sample",
    )(page_table, q_rows, mask, mask, k_new_t, v_new_t, *([cache_k_t] * pp), *([cache_v_t] * pp))


def _sprep_kernel(aq_ref, ak_ref, iq_ref, ikw_ref, qg_ref, kg_ref, qn_ref, kn_ref, iqs_ref, iw_ref):
    for c, t in enumerate(_head_rmsnorm(aq_ref[...], qg_ref[...])):
        qn_ref[:, c * LANES:(c + 1) * LANES] = (t * (A_HEAD_DIM ** -0.5 * LOG2_E)).astype(BF16)
    for c, t in enumerate(_head_rmsnorm(ak_ref[...], kg_ref[...])):
        kn_ref[:, c * LANES:(c + 1) * LANES] = t
    iqs_ref[...] = (iq_ref[...] * (IDX_DIM ** -0.5)).astype(BF16)
    iw_ref[...] = ikw_ref[...] * (IDX_HEADS ** -0.5)


def _sprep(p, q_gain, k_gain):
    n = p.shape[0]
    kvw = A_KV_HEADS * A_HEAD_DIM
    return pl.pallas_call(
        _sprep_kernel,
        grid=(1,),
        in_specs=[pl.BlockSpec((n, W), lambda i: (0, C_AQ // W)),
                  pl.BlockSpec((n, kvw), lambda i: (0, C_AK // kvw)),
                  pl.BlockSpec((n, W), lambda i: (0, C_IQ // W)),
                  pl.BlockSpec((n, LANES), lambda i: (0, C_IKW // LANES)),
                  pl.BlockSpec((1, LANES), lambda i: (0, 0)),
                  pl.BlockSpec((1, LANES), lambda i: (0, 0))],
        out_specs=[pl.BlockSpec((n, W), lambda i: (0, 0)), pl.BlockSpec((n, kvw), lambda i: (0, 0)),
                   pl.BlockSpec((n, W), lambda i: (0, 0)), pl.BlockSpec((n, LANES), lambda i: (0, 0))],
        out_shape=[jax.ShapeDtypeStruct((n, W), BF16), jax.ShapeDtypeStruct((n, kvw), F32),
                   jax.ShapeDtypeStruct((n, W), BF16), jax.ShapeDtypeStruct((n, LANES), F32)],
        compiler_params=_cparams("arbitrary"),
        name="sprep",
    )(p, p, p, p, q_gain, k_gain)


def _merge_kernel(oa_ref, ob_ref, oc_ref, od_ref, gates_ref, x_ref, wb_ref, wo_ref, n2_ref, x1_ref, h2_ref):
    merged = None
    for b, o_ref in enumerate((oa_ref, ob_ref, oc_ref, od_ref)):
        up = _dot(o_ref[...], wb_ref[b])
        t = _sigmoid(gates_ref[:, b * D_MODEL:(b + 1) * D_MODEL].astype(F32)) * up
        merged = t if merged is None else merged + t
    x1 = x_ref[...] + _dot(merged.astype(BF16), wo_ref[...])
    x1_ref[...] = x1
    ms = jnp.mean(x1 * x1, axis=-1, keepdims=True)
    h2_ref[...] = (x1 * lax.rsqrt(ms + RMS_EPS) * n2_ref[...]).astype(BF16)


def _merge(oa, ob, oc, od, p, x, w_branch, w_out, norm2, layer, tm):
    n = x.shape[0]
    row = lambda wdt: pl.BlockSpec((tm, wdt), lambda i: (i, 0))
    return pl.pallas_call(
        _merge_kernel,
        grid=(n // tm,),
        in_specs=[row(W), row(W), row(W), row(W),
                  pl.BlockSpec((tm, N_BRANCH * D_MODEL), lambda i: (i, C_GATES // (N_BRANCH * D_MODEL))),
                  row(D_MODEL),
                  pl.BlockSpec((None, N_BRANCH, W, D_MODEL), lambda i: (layer, 0, 0, 0)),
                  pl.BlockSpec((None, D_MODEL, D_MODEL), lambda i: (layer, 0, 0)),
                  pl.BlockSpec((1, D_MODEL), lambda i: (0, 0))],
        out_specs=[row(D_MODEL), row(D_MODEL)],
        out_shape=[jax.ShapeDtypeStruct((n, D_MODEL), F32), jax.ShapeDtypeStruct((n, D_MODEL), BF16)],
        compiler_params=_cparams("parallel"),
        name="merge",
    )(oa, ob, oc, od, p, x, w_branch, w_out, norm2)


def _ffn_kernel(h_ref, x_ref, wg_ref, wu_ref, wd_ref, o_ref, acc_ref):
    j = pl.program_id(1)

    @pl.when(j == 0)
    def _():
        acc_ref[...] = x_ref[...]

    h = h_ref[...]
    act = _silu(_dot(h, wg_ref[...])) * _dot(h, wu_ref[...])
    acc_ref[...] += _dot(act.astype(BF16), wd_ref[...])

    @pl.when(j == pl.num_programs(1) - 1)
    def _():
        o_ref[...] = acc_ref[...]


def _ffn(h2, x1, w_gu, w_down, layer, tm, fc):
    n = x1.shape[0]
    nf = D_FF // fc
    return pl.pallas_call(
        _ffn_kernel,
        grid=(n // tm, nf),
        in_specs=[pl.BlockSpec((tm, D_MODEL), lambda i, j: (i, 0)),
                  pl.BlockSpec((tm, D_MODEL), lambda i, j: (i, 0)),
                  pl.BlockSpec((None, D_MODEL, fc), lambda i, j: (layer, 0, j)),
                  pl.BlockSpec((None, D_MODEL, fc), lambda i, j: (layer, 0, nf + j)),
                  pl.BlockSpec((None, fc, D_MODEL), lambda i, j: (layer, j, 0))],
        out_specs=pl.BlockSpec((tm, D_MODEL), lambda i, j: (i, 0)),
        out_shape=jax.ShapeDtypeStruct((n, D_MODEL), F32),
        scratch_shapes=[pltpu.VMEM((tm, D_MODEL), F32)],
        compiler_params=_cparams("parallel", "arbitrary"),
        name="ffn",
    )(h2, x1, w_gu, w_gu, w_down)


def _pack_w_in(w_in):
    (a_q, a_k, a_v, a_iq, a_ik, a_iw, r_q, r_k, r_v, r_g, c_q, c_f, c_i, c_g,
     m_z, m_xbc, m_dt, gates) = jnp.split(w_in, SPLIT_POINTS, axis=-1)

    def pad_to_lanes(a):
        return jnp.pad(a, ((0, 0), (0, 0), (0, LANES - a.shape[-1])))

    ikw = pad_to_lanes(jnp.concatenate([a_ik, a_iw], axis=-1))
    packed = jnp.concatenate([gates, m_xbc, a_q, a_iq, r_q, r_k, r_v, r_g, c_q, c_f, c_i, c_g, m_z,
                              a_k, a_v, ikw, pad_to_lanes(m_dt)], axis=-1)
    return packed.astype(BF16)


def _rotary_tables(pos):
    half = R_DK // 2
    inv_freq = ROPE_BASE ** (-jnp.arange(half, dtype=F32) / half)
    ang = pos.astype(F32)[:, None] * inv_freq[None, :]
    c, s = jnp.cos(ang), jnp.sin(ang)
    return jnp.concatenate([c, c], axis=-1), jnp.concatenate([-s, s], axis=-1)


def _lower_bounds(lb_param):
    pr = jax.nn.softmax(lb_param.astype(F32), axis=0)
    return jnp.cumsum(pr, axis=0) - pr[0]


def _pad_lanes_row(v):
    return jnp.pad(v, (0, LANES - v.shape[0]))[None, :]


def kernel(x_prompt, x_sample, cache_k, cache_v, cache_kidx, state_ret, state_hgrn, state_ssm, state_conv,
           page_table, norm1_g, w_in, q_norm_g, k_norm_g, lb_param, hgrn_norm_g, A_log, dt_bias, D_skip,
           conv_w, conv_b, ssm_norm_g, w_branch, w_out, norm2_g, w_gu, w_down):
    bp, tp, _ = x_prompt.shape
    bs, ts, _ = x_sample.shape
    depth = w_in.shape[0]
    n_pool = cache_k.shape[1]
    n_pages = page_table.shape[1]
    kvw = A_KV_HEADS * A_HEAD_DIM
    tsp = SUBLANES
    npair = M_HEADS // 2

    w_in_p = _pack_w_in(w_in)
    w_branch_b = w_branch.astype(BF16)
    w_out_b = w_out.astype(BF16)
    w_gu_b = w_gu.astype(BF16)
    w_down_b = w_down.astype(BF16)
    lbs = _lower_bounds(lb_param)
    cos_p, sin_p = _rotary_tables(jnp.arange(tp, dtype=jnp.int32))
    pos_s = n_pages * PAGE_SIZE + jnp.arange(tsp, dtype=jnp.int32)
    cos_s, sin_s = _rotary_tables(pos_s)
    ck_t = jnp.transpose(cache_k, (0, 1, 3, 4, 2)).reshape(depth, n_pool, kvw, PAGE_SIZE)
    cv_t = jnp.transpose(cache_v, (0, 1, 3, 4, 2)).reshape(depth, n_pool, kvw, PAGE_SIZE)
    ckidx_t = jnp.transpose(cache_kidx, (0, 1, 3, 2))
    head_group = (jnp.arange(A_HEADS) // (A_HEADS // A_KV_HEADS))[:, None] == jnp.arange(A_KV_HEADS)[None, :]

    zero_ret = jnp.zeros((bp, R_HEADS, R_DK, R_DV), F32)
    zero_hgrn = jnp.zeros((bp, H_HEADS, H_DK, H_DV), F32)
    zero_ssm = jnp.zeros((bp, npair, 2 * M_HEAD_DIM, M_STATE), F32)
    zero_buf = jnp.zeros((bp, SUBLANES, M_CONV_DIM), F32)

    xp = x_prompt.reshape(bp * tp, D_MODEL)
    xs = x_sample.reshape(bs * ts, D_MODEL)
    outs_p = [[] for _ in range(7)]
    outs_s = [[] for _ in range(7)]
    for l in range(depth):
        n1 = norm1_g[l][None, :]
        n2 = norm2_g[l][None, :]
        qg = jnp.tile(q_norm_g[l], 2)[None, :]
        kg = jnp.tile(k_norm_g[l], 2)[None, :]
        lb = lbs[l][None, :]
        hng = hgrn_norm_g[l][None, :]
        alog = _pad_lanes_row(A_log[l])
        dtb = _pad_lanes_row(dt_bias[l])
        dsk = jnp.repeat(D_skip[l], M_HEAD_DIM)[None, :]
        sng = ssm_norm_g[l][None, :]
        cw = conv_w[l]
        cb = conv_b[l][None, :]

        p = _inproj(xp, n1, w_in_p, l, 1024, 1280, BF16)
        kn, vv, ik, knb, vb, ik2 = _kprep(p, kg, 512)
        oa = _attn_prompt(p, knb, vb, ik2, qg, bp, tp, 256, 256)
        ob, ret_new = _retention(p, cos_p, sin_p, zero_ret, bp, tp, 512, R_CHUNK, R_CHUNK)
        oc, hgrn_new = _hgrn2(p, lb, hng, zero_hgrn, bp, tp, 512, H_CHUNK, H_CHUNK, 16)
        od, ssm_new, buf_new = _ssd(p, cw, cb, alog, dtb, dsk, sng, zero_ssm, zero_buf, bp, tp, 512, M_CHUNK, M_CHUNK)
        x1, h2 = _merge(oa, ob, oc, od, p, xp, w_branch_b, w_out_b, n2, l, 512)
        xp = _ffn(h2, x1, w_gu_b, w_down_b, l, 512, D_FF // 2)
        for i, a in enumerate((kn.reshape(bp, tp, A_KV_HEADS, A_HEAD_DIM), vv.reshape(bp, tp, A_KV_HEADS, A_HEAD_DIM),
                               ik.reshape(bp, tp, IDX_DIM), ret_new, hgrn_new,
                               ssm_new.reshape(bp, M_HEADS, M_HEAD_DIM, M_STATE),
                               buf_new[:, SUBLANES - (M_CONV - 1):, :])):
            outs_p[i].append(a)

        ps = _inproj(xs, n1, w_in_p, l, bs * ts, 1280, F32)
        qn_s, kn_s, iq_s, iw_s = _sprep(ps, qg, kg)
        v_s = ps[:, C_AV:C_AV + kvw]
        ik_s = ps[:, C_IKW:C_IKW + IDX_DIM]
        pad_q = lambda a: jnp.pad(a, ((0, 0), (0, tsp - ts)) + ((0, 0),) * (a.ndim - 2))
        new_keys_t = lambda a: jnp.swapaxes(jnp.pad(a.reshape(bs, ts, -1), ((0, 0), (0, PAGE_SIZE - ts), (0, 0))), 1, 2)
        head_major = lambda a: jnp.swapaxes(pad_q(a), 1, 2)
        iq_rows = head_major(iq_s.reshape(bs, ts, IDX_HEADS, IDX_DIM)).reshape(bs, IDX_HEADS * tsp, IDX_DIM)
        iw_col = head_major(iw_s[:, IDX_DIM:IDX_DIM + IDX_HEADS].reshape(bs, ts, IDX_HEADS)).reshape(bs, IDX_HEADS * tsp, 1)
        mask = _idx_sample(page_table, iq_rows, iw_col, new_keys_t(ik_s).astype(BF16), ckidx_t, l, min(32, n_pages), ts)
        q4 = qn_s.reshape(bs, ts, A_HEADS, 1, A_HEAD_DIM)
        q_rows = jnp.where(head_group[None, None, :, :, None], q4, jnp.zeros_like(q4)).reshape(bs, ts * A_HEADS, kvw)
        oa_s = _attn_sample(page_table, q_rows, mask, new_keys_t(kn_s).astype(BF16), new_keys_t(v_s).astype(BF16),
                            ck_t, cv_t, l, min(32, n_pages), ts)
        oa_s = oa_s.reshape(bs * ts, W).astype(BF16)
        psp = pad_q(ps.reshape(bs, ts, NP)).reshape(bs * tsp, NP)
        ssm0 = state_ssm[l].reshape(bs, npair, 2 * M_HEAD_DIM, M_STATE)
        buf0 = jnp.pad(state_conv[l], ((0, 0), (SUBLANES - (M_CONV - 1), 0), (0, 0)))
        ob_s, ret_s = _retention(psp, cos_s, sin_s, state_ret[l], bs, tsp, tsp, tsp, ts)
        oc_s, hgrn_s = _hgrn2(psp, lb, hng, state_hgrn[l], bs, tsp, tsp, tsp, ts, tsp)
        od_s, ssm_s, buf_s = _ssd(psp, cw, cb, alog, dtb, dsk, sng, ssm0, buf0, bs, tsp, tsp, tsp, ts)
        unpad = lambda a: a.reshape(bs, tsp, W)[:, :ts].reshape(bs * ts, W)
        x1s, h2s = _merge(oa_s, unpad(ob_s), unpad(oc_s), unpad(od_s), ps, xs, w_branch_b, w_out_b, n2, l, bs * ts)
        xs = _ffn(h2s, x1s, w_gu_b, w_down_b, l, bs * ts, D_FF // 2)
        for i, a in enumerate((kn_s.reshape(bs, ts, A_KV_HEADS, A_HEAD_DIM), v_s.reshape(bs, ts, A_KV_HEADS, A_HEAD_DIM),
                               ik_s.reshape(bs, ts, IDX_DIM), ret_s, hgrn_s,
                               ssm_s.reshape(bs, M_HEADS, M_HEAD_DIM, M_STATE),
                               buf_s[:, SUBLANES - (M_CONV - 1):, :])):
            outs_s[i].append(a)

    dtypes = (cache_k.dtype, cache_v.dtype, cache_kidx.dtype, state_ret.dtype,
              state_hgrn.dtype, state_ssm.dtype, state_conv.dtype)
    res_p = [jnp.stack(a).astype(d) for a, d in zip(outs_p, dtypes)]
    res_s = [jnp.stack(a).astype(d) for a, d in zip(outs_s, dtypes)]
    return (xp.reshape(bp, tp, D_MODEL), xs.reshape(bs, ts, D_MODEL), *res_p, *res_s)
```

```python
import functools
import math
import struct

import jax
import jax.numpy as jnp
from jax import lax
from jax.experimental import pallas as pl
from jax.experimental.pallas import tpu as pltpu

F32 = jnp.float32
BF16 = jnp.bfloat16
I32 = jnp.int32

D_MODEL = 1024
PAGE_SIZE = 128
BRANCH_WIDTH = D_MODEL // 2
N_BRANCH = 4
A_HEAD_DIM = 64
A_HEADS = BRANCH_WIDTH // A_HEAD_DIM
A_KV_HEADS = A_HEADS // 2
IDX_HEADS = 8
IDX_DIM = 64
TOPK_MAX = 256
Q_BLOCK = 128
R_HEADS = 4
R_DK = BRANCH_WIDTH // R_HEADS
R_DV = BRANCH_WIDTH // R_HEADS
R_CHUNK = 128
ROPE_BASE = 10000.0
H_HEADS = 4
H_DK = 128
H_DV = BRANCH_WIDTH // H_HEADS
H_CHUNK = 64
MIN_FORGET = 1e-30
SAFE_EXP = 60.0
M_HEAD_DIM = 64
M_HEADS = BRANCH_WIDTH // M_HEAD_DIM
M_INNER = M_HEADS * M_HEAD_DIM
M_GROUPS = 2
M_STATE = 128
M_CONV = 4
M_CHUNK = 128
M_CONV_DIM = M_INNER + 2 * M_GROUPS * M_STATE
D_FF = -(-8 * D_MODEL // (3 * 256)) * 256
RMS_EPS = 1e-6
GN_EPS = 1e-6
MASK_VALUE = -1e30

SPLIT_SIZES = (
    A_HEADS * A_HEAD_DIM, A_KV_HEADS * A_HEAD_DIM, A_KV_HEADS * A_HEAD_DIM,
    IDX_HEADS * IDX_DIM, IDX_DIM, IDX_HEADS,
    R_HEADS * R_DK, R_HEADS * R_DK, R_HEADS * R_DV, R_HEADS * R_DV,
    H_HEADS * H_DK, H_HEADS * H_DK, H_HEADS * H_DV, H_HEADS * H_DV,
    M_INNER, M_CONV_DIM, M_HEADS,
    N_BRANCH * D_MODEL,
)
SPLIT_POINTS = tuple(sum(SPLIT_SIZES[:i + 1]) for i in range(len(SPLIT_SIZES) - 1))

LANES = 128
SUBLANES = 8
VMEM_LIMIT = 56 * 1024 * 1024

C_GATES = 0
C_XBC = 4096
C_AQ = 5120
C_IQ = 5632
C_RQ = 6144
C_RK = 6656
C_RV = 7168
C_RG = 7680
C_CQ = 8192
C_CF = 8704
C_CI = 9216
C_CG = 9728
C_MZ = 10240
C_AK = 10752
C_AV = 11008
C_IKW = 11264
C_DT = 11392
NP = 11520
W = BRANCH_WIDTH


def _float_key(x):
    b = struct.unpack("<i", struct.pack("<f", x))[0]
    return b if b >= 0 else b ^ 0x7FFFFFFF


VALID_KEY = _float_key(0.5 * MASK_VALUE)
INT_MIN = -2 ** 31
LOG2_E = math.log2(math.e)


def _cparams(*sem):
    return pltpu.CompilerParams(dimension_semantics=sem, vmem_limit_bytes=VMEM_LIMIT)


def _nt(a, b):
    return lax.dot_general(a, b, (((1,), (1,)), ((), ())), preferred_element_type=F32)


def _tn(a, b):
    return lax.dot_general(a, b, (((0,), (0,)), ((), ())), preferred_element_type=F32)


def _dot(a, b):
    return jnp.dot(a, b, preferred_element_type=F32)


def _dot_exact_lhs(a01, x):
    a = a01.astype(BF16)
    x0 = x.astype(BF16)
    r1 = x - x0.astype(F32)
    x1 = r1.astype(BF16)
    x2 = (r1 - x1.astype(F32)).astype(BF16)
    return _dot(a, x0) + _dot(a, x1) + _dot(a, x2)


def _dot_exact_rhs(x, b01):
    b = b01.astype(BF16)
    x0 = x.astype(BF16)
    r1 = x - x0.astype(F32)
    x1 = r1.astype(BF16)
    x2 = (r1 - x1.astype(F32)).astype(BF16)
    return _dot(x0, b) + _dot(x1, b) + _dot(x2, b)


def _sigmoid(x):
    return 1.0 / (1.0 + jnp.exp(-x))


def _silu(x):
    return x * _sigmoid(x)


def _iota(shape, dim):
    return lax.broadcasted_iota(I32, shape, dim)


def _tri_incl(n):
    return (_iota((n, n), 0) >= _iota((n, n), 1)).astype(F32)


def _inproj_kernel(x_ref, g_ref, w_ref, o_ref, h_ref):
    @pl.when(pl.program_id(1) == 0)
    def _():
        x = x_ref[...]
        ms = jnp.mean(x * x, axis=-1, keepdims=True)
        h_ref[...] = (x * lax.rsqrt(ms + RMS_EPS) * g_ref[...]).astype(BF16)

    o_ref[...] = _dot(h_ref[...], w_ref[...]).astype(o_ref.dtype)


def _inproj(x, g, w, layer, tm, tn, out_dtype):
    n = x.shape[0]
    return pl.pallas_call(
        _inproj_kernel,
        grid=(n // tm, NP // tn),
        in_specs=[pl.BlockSpec((tm, D_MODEL), lambda i, j: (i, 0)),
                  pl.BlockSpec((1, D_MODEL), lambda i, j: (0, 0)),
                  pl.BlockSpec((None, D_MODEL, tn), lambda i, j: (layer, 0, j))],
        out_specs=pl.BlockSpec((tm, tn), lambda i, j: (i, j)),
        out_shape=jax.ShapeDtypeStruct((n, NP), out_dtype),
        scratch_shapes=[pltpu.VMEM((tm, D_MODEL), BF16)],
        compiler_params=_cparams("parallel", "arbitrary"),
        name="inproj",
    )(x, g, w)


def _head_rmsnorm(x, gain_row):
    pair = (_iota((LANES, LANES), 0) // A_HEAD_DIM == _iota((LANES, LANES), 1) // A_HEAD_DIM).astype(F32)
    outs = []
    for c in range(x.shape[1] // LANES):
        xc = x[:, c * LANES:(c + 1) * LANES]
        ms = _dot_exact_rhs(xc * xc, pair) * (1.0 / A_HEAD_DIM)
        outs.append(xc * lax.rsqrt(ms + RMS_EPS) * gain_row)
    return outs


def _kth_largest_key(count_ge, k, shape):
    nonneg = count_ge(jnp.zeros(shape, I32)) >= k
    prefix = jnp.where(nonneg, 0, INT_MIN).astype(I32)

    def body(t, prefix):
        cand = prefix + jnp.left_shift(jnp.int32(1), 30 - t)
        return jnp.where(count_ge(cand) >= k, cand, prefix)

    return lax.fori_loop(0, 31, body, prefix)


def _to_key(s):
    b = lax.bitcast_convert_type(s, I32)
    return jnp.where(b >= 0, b, b ^ 0x7FFFFFFF)


def _kprep_kernel(ak_ref, av_ref, ikw_ref, kg_ref, kn_ref, v_ref, ik_ref, knb_ref, vb_ref, ik2_ref):
    kn = _head_rmsnorm(ak_ref[...].astype(F32), kg_ref[...])
    for c, t in enumerate(kn):
        kn_ref[:, c * LANES:(c + 1) * LANES] = t
        knb_ref[:, c * LANES:(c + 1) * LANES] = t.astype(BF16)
    v = av_ref[...].astype(F32)
    v_ref[...] = v
    vb_ref[...] = v.astype(BF16)
    ikw = ikw_ref[...].astype(F32)
    ik_ref[...] = ikw[:, :IDX_DIM]
    lane = _iota(ikw.shape, 1)
    ik2_ref[...] = jnp.where(lane < IDX_DIM, ikw, pltpu.roll(ikw, IDX_DIM, 1)).astype(BF16)


def _kprep(p, k_gain, tm):
    n = p.shape[0]
    kvw = A_KV_HEADS * A_HEAD_DIM
    row = lambda wdt: pl.BlockSpec((tm, wdt), lambda i: (i, 0))
    return pl.pallas_call(
        _kprep_kernel,
        grid=(n // tm,),
        in_specs=[pl.BlockSpec((tm, kvw), lambda i: (i, C_AK // kvw)),
                  pl.BlockSpec((tm, kvw), lambda i: (i, C_AV // kvw)),
                  pl.BlockSpec((tm, LANES), lambda i: (i, C_IKW // LANES)),
                  pl.BlockSpec((1, LANES), lambda i: (0, 0))],
        out_specs=[row(kvw), row(kvw), row(IDX_DIM), row(kvw), row(kvw), row(LANES)],
        out_shape=[jax.ShapeDtypeStruct((n, kvw), F32),
                   jax.ShapeDtypeStruct((n, kvw), F32),
                   jax.ShapeDtypeStruct((n, IDX_DIM), F32),
                   jax.ShapeDtypeStruct((n, kvw), BF16),
                   jax.ShapeDtypeStruct((n, kvw), BF16),
                   jax.ShapeDtypeStruct((n, LANES), BF16)],
        compiler_params=_cparams("parallel"),
        name="kprep",
    )(p, p, p, k_gain)


def _attn_prompt_kernel(aq_ref, iq_ref, ikw_ref, qg_ref, knb_ref, vb_ref, ik2_ref, o_ref,
                        key_ref, msk_ref, s_ref, qt_ref, iqt_ref, acc_ref, cut_ref, *, topk, kb, qb):
    i = pl.program_id(1)
    nblk = (i * qb + qb + kb - 1) // kb
    row = _iota((kb, qb), 0)
    lane = _iota((kb, qb), 1)
    lane_q = _iota((qb, LANES), 1)
    own_half = [lane_q < A_HEAD_DIM, lane_q >= A_HEAD_DIM]
    n_pairs = A_KV_HEADS // 2
    hpp = A_HEADS // n_pairs
    nsub = kb // SUBLANES

    qn = _head_rmsnorm(aq_ref[...].astype(F32), qg_ref[...])
    for h in range(A_HEADS):
        src, dst = h % 2, (h // 2) % 2
        t = qn[h // 2] * (A_HEAD_DIM ** -0.5 * LOG2_E)
        if src != dst:
            t = pltpu.roll(t, A_HEAD_DIM, 1)
        t = jnp.where(own_half[dst], t, 0.0)
        qt_ref[h // hpp, :, (h % hpp) * qb:(h % hpp + 1) * qb] = t.T.astype(BF16)
    for h in range(IDX_HEADS):
        t = iq_ref[:, (h // 2) * LANES:(h // 2 + 1) * LANES].astype(F32) * (IDX_DIM ** -0.5)
        iqt_ref[:, h * qb:(h + 1) * qb] = jnp.where(own_half[h % 2], t, 0.0).T.astype(BF16)
    iw_t = ikw_ref[...].astype(F32).T[IDX_DIM:IDX_DIM + IDX_HEADS, :] * (IDX_HEADS ** -0.5)

    def score_blk(j, carry):
        ks = pl.multiple_of(j * kb, kb)
        lg = _dot(ik2_ref[pl.ds(ks, kb), :], iqt_ref[...])
        acc = jnp.zeros((kb, qb), F32)
        for h in range(IDX_HEADS):
            acc = acc + jnp.maximum(lg[:, h * qb:(h + 1) * qb], 0.0) * iw_t[h:h + 1, :]
        adm = (ks + row) <= (i * qb + lane)
        key_ref[j] = _to_key(jnp.where(adm, acc, MASK_VALUE))
        return carry

    lax.fori_loop(0, nblk, score_blk, 0)

    def count(pred):
        def body(j, c):
            return c + jnp.sum(jnp.where(pred(key_ref[j], j), 1, 0).reshape(nsub, SUBLANES, qb), axis=0)
        c = lax.fori_loop(0, nblk, body, jnp.zeros((SUBLANES, qb), I32))
        return jnp.sum(c, axis=0, keepdims=True)

    cut_ref[0:1, :] = jnp.full((1, qb), INT_MIN, I32)
    cut_ref[1:2, :] = jnp.full((1, qb), 2 ** 30, I32)

    @pl.when(i * qb + qb > topk)
    def _():
        thr = _kth_largest_key(lambda cand: count(lambda blk, j: blk >= cand), topk, (1, qb))
        cut_ref[0:1, :] = thr
        n_gt = count(lambda blk, j: blk > thr)
        n_eq = count(lambda blk, j: blk == thr)
        need = topk - n_gt

        @pl.when(jnp.max(n_eq - need) > 0)
        def _():
            def body(t, ans):
                cand = ans + jnp.left_shift(jnp.int32(1), 15 - t)
                upto = count(lambda blk, j: (blk == thr) & ((j * kb + row) <= cand - 1))
                return jnp.where(upto < need, cand, ans)

            cut_ref[1:2, :] = lax.fori_loop(0, 16, body, jnp.zeros((1, qb), I32))

    thr = cut_ref[0:1, :]
    last = cut_ref[1:2, :]
    lo = jnp.maximum(thr, VALID_KEY)
    tie_ok = thr > VALID_KEY

    def mask_blk(j, carry):
        blk = key_ref[j]
        sel = (blk > lo) | ((blk == thr) & tie_ok & ((j * kb + row) <= last))
        msk_ref[j] = jnp.where(sel, 0.0, MASK_VALUE)
        return carry

    lax.fori_loop(0, nblk, mask_blk, 0)

    def fold(t, op):
        return op(t.reshape(nsub, SUBLANES, qb), axis=0)

    def logits_blk(j, m8):
        ks = pl.multiple_of(j * kb, kb)
        mk = msk_ref[j]
        out = []
        for pr in range(n_pairs):
            s4 = _dot(knb_ref[pl.ds(ks, kb), pr * LANES:(pr + 1) * LANES], qt_ref[pr])
            for hh in range(hpp):
                h = pr * hpp + hh
                s = s4[:, hh * qb:(hh + 1) * qb] + mk
                s_ref[j, h] = s
                out.append(jnp.maximum(m8[h], fold(s, jnp.max)))
        return tuple(out)

    m8 = lax.fori_loop(0, nblk, logits_blk, tuple(jnp.full((SUBLANES, qb), MASK_VALUE, F32) for _ in range(A_HEADS)))
    m_row = [jnp.max(t, axis=0, keepdims=True) for t in m8]

    for pr in range(n_pairs):
        acc_ref[pr] = jnp.zeros((hpp * qb, LANES), F32)

    def values_blk(j, l8):
        ks = pl.multiple_of(j * kb, kb)
        out = []
        for pr in range(n_pairs):
            e_t = []
            for hh in range(hpp):
                h = pr * hpp + hh
                e = jnp.exp2(s_ref[j, h] - m_row[h])
                out.append(l8[h] + fold(e, jnp.sum))
                e_t.append(e.T.astype(BF16))
            lhs = jnp.concatenate(e_t, axis=0)
            acc_ref[pr] = acc_ref[pr] + _dot(lhs, vb_ref[pl.ds(ks, kb), pr * LANES:(pr + 1) * LANES])
        return tuple(out)

    l8 = lax.fori_loop(0, nblk, values_blk, tuple(jnp.zeros((SUBLANES, qb), F32) for _ in range(A_HEADS)))

    for h in range(A_HEADS):
        pr, hh, half = h // hpp, h % hpp, (h // 2) % 2
        l_row = jnp.sum(l8[h], axis=0, keepdims=True)
        l_col = jnp.broadcast_to(l_row, (SUBLANES, qb)).T[:, 0:1]
        o = acc_ref[pr, hh * qb:(hh + 1) * qb, half * A_HEAD_DIM:(half + 1) * A_HEAD_DIM] / l_col
        o_ref[:, h * A_HEAD_DIM:(h + 1) * A_HEAD_DIM] = o.astype(o_ref.dtype)


def _attn_prompt(p, knb, vb, ik2, q_gain, batch, seq, kb, qb):
    nqb = seq // qb
    nkb = seq // kb
    kvw = A_KV_HEADS * A_HEAD_DIM
    topk = min(TOPK_MAX, seq // 4)
    n_pairs = A_KV_HEADS // 2
    hpp = A_HEADS // n_pairs
    return pl.pallas_call(
        functools.partial(_attn_prompt_kernel, topk=topk, kb=kb, qb=qb),
        grid=(batch, nqb),
        in_specs=[pl.BlockSpec((qb, W), lambda b, i: (b * nqb + i, C_AQ // W)),
                  pl.BlockSpec((qb, W), lambda b, i: (b * nqb + i, C_IQ // W)),
                  pl.BlockSpec((qb, LANES), lambda b, i: (b * nqb + i, C_IKW // LANES)),
                  pl.BlockSpec((1, LANES), lambda b, i: (0, 0)),
                  pl.BlockSpec((seq, kvw), lambda b, i: (b, 0)),
                  pl.BlockSpec((seq, kvw), lambda b, i: (b, 0)),
                  pl.BlockSpec((seq, LANES), lambda b, i: (b, 0))],
        out_specs=pl.BlockSpec((qb, W), lambda b, i: (b * nqb + i, 0)),
        out_shape=jax.ShapeDtypeStruct((batch * seq, W), BF16),
        scratch_shapes=[pltpu.VMEM((nkb, kb, qb), I32),
                        pltpu.VMEM((nkb, kb, qb), F32),
                        pltpu.VMEM((nkb, A_HEADS, kb, qb), F32),
                        pltpu.VMEM((n_pairs, LANES, hpp * qb), BF16),
                        pltpu.VMEM((LANES, IDX_HEADS * qb), BF16),
                        pltpu.VMEM((n_pairs, hpp * qb, LANES), F32),
                        pltpu.VMEM((SUBLANES, qb), I32)],
        compiler_params=_cparams("parallel", "arbitrary"),
        name="attn_prompt",
    )(p, p, p, q_gain, knb, vb, ik2)


def _ret_kernel(q_ref, k_ref, v_ref, g_ref, cos_ref, sin_ref, s0_ref, o_ref, s_out_ref, st_ref,
                *, cp, c_true):
    tb = pl.program_id(1)

    @pl.when(tb == 0)
    def _():
        st_ref[...] = s0_ref[0]

    row = _iota((cp, 1), 0).astype(F32)
    diff = (_iota((cp, cp), 0) - _iota((cp, cp), 1)).astype(F32)

    def chunk(c, carry):
        r0 = pl.multiple_of(c * cp, cp)
        rows = pl.ds(r0, cp)
        cos = cos_ref[rows, :]
        sin = sin_ref[rows, :]
        for h in range(R_HEADS):
            lg = math.log1p(-2.0 ** (-5.0 - h))
            cols = slice(h * R_DK, (h + 1) * R_DK)
            q = q_ref[rows, cols].astype(F32)
            k = k_ref[rows, cols].astype(F32)
            v = v_ref[rows, cols].astype(F32)
            q = q * cos + pltpu.roll(q, R_DK // 2, 1) * sin
            k = (k * cos + pltpu.roll(k, R_DK // 2, 1) * sin) * (R_DK ** -0.5)
            decay = jnp.where(diff >= 0, jnp.exp(lg * jnp.maximum(diff, 0.0)), 0.0)
            scores = _nt(q.astype(BF16), k.astype(BF16)) * decay
            s = st_ref[h]
            o = _dot(scores.astype(BF16), v.astype(BF16))
            o = o + _dot((q * jnp.exp(lg * (row + 1.0))).astype(BF16), s.astype(BF16))
            kd = jnp.where(row < c_true, jnp.exp(lg * jnp.maximum(c_true - 1.0 - row, 0.0)), 0.0)
            st_ref[h] = math.exp(lg * c_true) * s + _tn((k * kd).astype(BF16), v.astype(BF16))
            mu = jnp.mean(o, axis=-1, keepdims=True)
            d = o - mu
            var = jnp.mean(d * d, axis=-1, keepdims=True)
            o_ref[rows, cols] = (_silu(g_ref[rows, cols].astype(F32)) * (d * lax.rsqrt(var + GN_EPS))).astype(o_ref.dtype)
        return carry

    n_chunks = q_ref.shape[0] // cp
    lax.fori_loop(0, n_chunks, chunk, 0, unroll=2 if n_chunks % 2 == 0 else 1)

    @pl.when(tb == pl.num_programs(1) - 1)
    def _():
        s_out_ref[0] = st_ref[...]


def _hgrn_kernel(q_ref, f_ref, i_ref, g_ref, lb_ref, ng_ref, s0_ref, o_ref, s_out_ref,
                 st_ref, qs_ref, ks_ref, gs_ref, *, cp, c_true, sb):
    tb = pl.program_id(1)

    @pl.when(tb == 0)
    def _():
        st_ref[...] = s0_ref[0]

    tri = _tri_incl(cp)
    rowi = _iota((cp, 1), 0)
    real = rowi < c_true
    sub_n = _iota((sb, 1), 0)
    ng = ng_ref[...]
    n_chunks = q_ref.shape[0] // cp
    unroll = 2 if n_chunks % 2 == 0 else 1
    n_sub = cp // sb

    def prepare(c, spread):
        rows = pl.ds(pl.multiple_of(c * cp, cp), cp)
        for h in range(H_HEADS):
            cols = slice(h * H_DK, (h + 1) * H_DK)
            lb = lb_ref[:, cols]
            f = f_ref[rows, cols].astype(F32)
            forget = lb + (1.0 - lb) * _sigmoid(f)
            logf = jnp.where(real, jnp.log(jnp.maximum(forget, MIN_FORGET)), 0.0)
            gc = _dot_exact_lhs(tri, logf)
            qs_ref[rows, cols] = _silu(q_ref[rows, cols].astype(F32)) * (H_DK ** -0.5)
            ks_ref[rows, cols] = jnp.where(real, (1.0 - lb) * _sigmoid(-f), 0.0)
            gs_ref[rows, cols] = gc
            for b in range(n_sub):
                spread = jnp.maximum(spread, gc[b * sb:b * sb + 1, :] - gc[b * sb + sb - 1:b * sb + sb, :])
        return spread

    spread = lax.fori_loop(0, n_chunks, prepare, jnp.zeros((1, H_DK), F32), unroll=unroll)
    safe = jnp.max(spread) < SAFE_EXP

    blk_of_row = rowi // sb
    causal = _iota((cp, cp), 0) >= _iota((cp, cp), 1)

    def intra_factored(q, k, gc):
        gref_rows = jnp.concatenate([jnp.broadcast_to(gc[b * sb:b * sb + 1, :], (sb, H_DK)) for b in range(n_sub)], axis=0)
        qd = q * jnp.exp(gc - gref_rows)
        lhs, rhs = [], []
        for b in range(n_sub):
            lhs.append(jnp.where(blk_of_row == b, qd, 0.0).astype(BF16))
            kd = k * jnp.exp(jnp.minimum(gc[b * sb:b * sb + 1, :] - gc, SAFE_EXP))
            rhs.append(jnp.where(rowi < (b + 1) * sb, kd, 0.0).astype(BF16))
        cat = (lambda t: t[0]) if n_sub == 1 else (lambda t: jnp.concatenate(t, axis=1))
        return jnp.where(causal, _nt(cat(lhs), cat(rhs)), 0.0)

    def intra_by_column(q, k, gc):
        rows_out = []
        for b in range(n_sub):
            b0 = b * sb
            gi = gc[b0:b0 + sb, :]
            qi = q[b0:b0 + sb, :]
            ki = k[b0:b0 + sb, :]
            lane_a = _iota((sb, cp), 1)
            if b0 > 0:
                gref = gc[b0:b0 + 1, :]
                kd = jnp.where(rowi < b0, k * jnp.exp(jnp.minimum(gref - gc, 0.0)), 0.0)
                a = _nt((qi * jnp.exp(gi - gref)).astype(BF16), kd.astype(BF16))
            else:
                a = jnp.zeros((sb, cp), F32)
            for m in range(sb):
                e = jnp.exp(jnp.where(sub_n >= m, gi - gi[m:m + 1, :], MASK_VALUE))
                col = jnp.sum(qi * e * ki[m:m + 1, :], axis=1, keepdims=True)
                a = jnp.where(lane_a == b0 + m, col, a)
            rows_out.append(a)
        return rows_out[0] if n_sub == 1 else jnp.concatenate(rows_out, axis=0)

    def make_chunk(intra):
        def chunk(c, carry):
            rows = pl.ds(pl.multiple_of(c * cp, cp), cp)
            for h in range(H_HEADS):
                cols = slice(h * H_DK, (h + 1) * H_DK)
                q = qs_ref[rows, cols]
                k = ks_ref[rows, cols]
                gc = gs_ref[rows, cols]
                vb = jnp.where(real, i_ref[rows, cols].astype(F32), 0.0).astype(BF16)
                sbf = st_ref[h].astype(BF16)
                qe = (q * jnp.exp(gc)).astype(BF16)
                am = intra(q, k, gc).astype(BF16)
                if cp % LANES == 0 or cp * 2 == LANES:
                    o = _dot(jnp.concatenate([qe, am], axis=1), jnp.concatenate([sbf, vb], axis=0))
                else:
                    o = _dot(qe, sbf) + _dot(am, vb)
                g_last = gc[cp - 1:cp, :]
                kd = k * jnp.exp(g_last - gc)
                eg_col = jnp.broadcast_to(jnp.exp(g_last), (SUBLANES, H_DK)).T[:, 0:1]
                st_ref[h] = eg_col * st_ref[h] + _tn(kd.astype(BF16), vb)
                ms = jnp.mean(o * o, axis=-1, keepdims=True)
                o = o * lax.rsqrt(ms + RMS_EPS) * ng
                o_ref[rows, cols] = (_silu(g_ref[rows, cols].astype(F32)) * o).astype(o_ref.dtype)
            return carry
        return chunk

    @pl.when(safe)
    def _():
        lax.fori_loop(0, n_chunks, make_chunk(intra_factored), 0, unroll=unroll)

    @pl.when(jnp.logical_not(safe))
    def _():
        lax.fori_loop(0, n_chunks, make_chunk(intra_by_column), 0)

    @pl.when(tb == pl.num_programs(1) - 1)
    def _():
        s_out_ref[0] = st_ref[...]


def _softplus(x):
    return jnp.maximum(x, 0.0) + jnp.log1p(jnp.exp(-jnp.abs(x)))


def _ssd_kernel(z_ref, xbc_ref, dt_ref, cw_ref, cb_ref, alog_ref, dtb_ref, dskip_ref, ng_ref,
                h0_ref, buf0_ref, o_ref, h_out_ref, buf_out_ref, st_ref, tail_ref, ext_ref,
                *, cp, c_true):
    tb = pl.program_id(1)

    @pl.when(tb == 0)
    def _():
        st_ref[...] = h0_ref[0]
        tail_ref[...] = buf0_ref[0]

    tri = _tri_incl(cp)
    rowi = _iota((cp, 1), 0)
    real = rowi < c_true
    causal = _iota((cp, cp), 0) >= _iota((cp, cp), 1)
    lane = _iota((1, LANES), 1)
    left = lane < M_HEAD_DIM
    top = _iota((LANES, 1), 0) < M_HEAD_DIM
    cw = cw_ref[...]
    cb = cb_ref[...]
    neg_a = -jnp.exp(alog_ref[...])
    gw = M_INNER // M_GROUPS

    def chunk(c, carry):
        r0 = pl.multiple_of(c * cp, cp)
        rows = pl.ds(r0, cp)
        u = xbc_ref[rows, :].astype(F32)
        ext_ref[0:SUBLANES, :] = tail_ref[...]
        ext_ref[SUBLANES:SUBLANES + cp, :] = u
        y = cb + cw[M_CONV - 1:M_CONV, :] * u
        for j in range(M_CONV - 1):
            y = y + cw[j:j + 1, :] * ext_ref[pl.ds(SUBLANES - (M_CONV - 1) + j, cp), :]
        tail_ref[...] = ext_ref[pl.ds(c_true, SUBLANES), :]
        xbc = _silu(y)
        dt = jnp.where(real, _softplus(dt_ref[rows, :].astype(F32) + dtb_ref[...]), 0.0)
        a = neg_a * dt
        cum = _dot_exact_lhs(tri, a)
        cum_t = cum.T
        dt_t = dt.T
        e_cum = jnp.exp(cum)
        cum_last = cum[cp - 1:cp, :]
        w_all = jnp.exp(cum_last - cum) * dt
        e_last = jnp.exp(cum_last)
        z = z_ref[rows, :].astype(F32)
        ys = []
        for pr in range(M_HEADS // 2):
            g = (2 * pr) // (M_HEADS // M_GROUPS)
            bm = xbc[:, M_INNER + g * M_STATE:M_INNER + (g + 1) * M_STATE].astype(BF16)
            cm = xbc[:, M_INNER + (M_GROUPS + g) * M_STATE:M_INNER + (M_GROUPS + g + 1) * M_STATE].astype(BF16)
            xp = xbc[:, pr * LANES:(pr + 1) * LANES]
            xpb = xp.astype(BF16)
            cbm = _nt(cm, bm)
            hp = st_ref[pr]
            cross = _nt(cm, hp.astype(BF16))
            intra = []
            for hh in range(2):
                h = 2 * pr + hh
                seg = jnp.exp(jnp.where(causal, cum[:, h:h + 1] - cum_t[h:h + 1, :], MASK_VALUE))
                mat = cbm * seg * dt_t[h:h + 1, :]
                intra.append(_dot(mat.astype(BF16), xpb))
            h0, h1 = 2 * pr, 2 * pr + 1
            y_pair = (jnp.where(left, intra[0], intra[1])
                      + cross * jnp.where(left, e_cum[:, h0:h0 + 1], e_cum[:, h1:h1 + 1]))
            wx = xp * jnp.where(left, w_all[:, h0:h0 + 1], w_all[:, h1:h1 + 1])
            decay = jnp.where(top, e_last[:, h0:h0 + 1], e_last[:, h1:h1 + 1])
            st_ref[pr] = decay * hp + _tn(wx.astype(BF16), bm)
            y_pair = y_pair + dskip_ref[:, pr * LANES:(pr + 1) * LANES] * xp
            ys.append(y_pair * _silu(z[:, pr * LANES:(pr + 1) * LANES]))
        for g in range(M_GROUPS):
            yg = jnp.concatenate(ys[g * 2:(g + 1) * 2], axis=1)
            ms = jnp.mean(yg * yg, axis=-1, keepdims=True)
            o_ref[rows, g * gw:(g + 1) * gw] = (yg * lax.rsqrt(ms + RMS_EPS)
                                                * ng_ref[:, g * gw:(g + 1) * gw]).astype(o_ref.dtype)
        return carry

    lax.fori_loop(0, z_ref.shape[0] // cp, chunk, 0)

    @pl.when(tb == pl.num_programs(1) - 1)
    def _():
        h_out_ref[0] = st_ref[...]
        buf_out_ref[0] = tail_ref[...]


def _row_blocks(batch, rows_per_seq, tb):
    n_tb = rows_per_seq // tb
    return n_tb, (lambda col: (lambda b, t: (b * n_tb + t, col)))


def _retention(p, cos2, sin2, s0, batch, rows_per_seq, tb, cp, c_true):
    n_tb, at = _row_blocks(batch, rows_per_seq, tb)
    st_spec = pl.BlockSpec((1, R_HEADS, R_DK, R_DV), lambda b, t: (b, 0, 0, 0))
    return pl.pallas_call(
        functools.partial(_ret_kernel, cp=cp, c_true=c_true),
        grid=(batch, n_tb),
        in_specs=[pl.BlockSpec((tb, W), at(C_RQ // W)), pl.BlockSpec((tb, W), at(C_RK // W)),
                  pl.BlockSpec((tb, W), at(C_RV // W)), pl.BlockSpec((tb, W), at(C_RG // W)),
                  pl.BlockSpec((tb, R_DK), lambda b, t: (t, 0)), pl.BlockSpec((tb, R_DK), lambda b, t: (t, 0)),
                  st_spec],
        out_specs=[pl.BlockSpec((tb, W), at(0)), st_spec],
        out_shape=[jax.ShapeDtypeStruct((batch * rows_per_seq, W), BF16),
                   jax.ShapeDtypeStruct((batch, R_HEADS, R_DK, R_DV), F32)],
        scratch_shapes=[pltpu.VMEM((R_HEADS, R_DK, R_DV), F32)],
        compiler_params=_cparams("parallel", "arbitrary"),
        name="retention",
    )(p, p, p, p, cos2, sin2, s0)


def _hgrn2(p, lb, norm_g, s0, batch, rows_per_seq, tb, cp, c_true, sb):
    n_tb, at = _row_blocks(batch, rows_per_seq, tb)
    st_spec = pl.BlockSpec((1, H_HEADS, H_DK, H_DV), lambda b, t: (b, 0, 0, 0))
    return pl.pallas_call(
        functools.partial(_hgrn_kernel, cp=cp, c_true=c_true, sb=sb),
        grid=(batch, n_tb),
        in_specs=[pl.BlockSpec((tb, W), at(C_CQ // W)), pl.BlockSpec((tb, W), at(C_CF // W)),
                  pl.BlockSpec((tb, W), at(C_CI // W)), pl.BlockSpec((tb, W), at(C_CG // W)),
                  pl.BlockSpec((1, W), lambda b, t: (0, 0)), pl.BlockSpec((1, H_DV), lambda b, t: (0, 0)),
                  st_spec],
        out_specs=[pl.BlockSpec((tb, W), at(0)), st_spec],
        out_shape=[jax.ShapeDtypeStruct((batch * rows_per_seq, W), BF16),
                   jax.ShapeDtypeStruct((batch, H_HEADS, H_DK, H_DV), F32)],
        scratch_shapes=[pltpu.VMEM((H_HEADS, H_DK, H_DV), F32)] + [pltpu.VMEM((tb, W), F32)] * 3,
        compiler_params=_cparams("parallel", "arbitrary"),
        name="hgrn2",
    )(p, p, p, p, lb, norm_g, s0)


def _ssd(p, conv_w, conv_b, a_log, dt_bias, d_skip, norm_g, h0, buf0, batch, rows_per_seq, tb, cp, c_true):
    n_tb, at = _row_blocks(batch, rows_per_seq, tb)
    npair = M_HEADS // 2
    st_spec = pl.BlockSpec((1, npair, 2 * M_HEAD_DIM, M_STATE), lambda b, t: (b, 0, 0, 0))
    buf_spec = pl.BlockSpec((1, SUBLANES, M_CONV_DIM), lambda b, t: (b, 0, 0))
    row = lambda wdt: pl.BlockSpec((1, wdt), lambda b, t: (0, 0))
    return pl.pallas_call(
        functools.partial(_ssd_kernel, cp=cp, c_true=c_true),
        grid=(batch, n_tb),
        in_specs=[pl.BlockSpec((tb, W), at(C_MZ // W)), pl.BlockSpec((tb, M_CONV_DIM), at(C_XBC // M_CONV_DIM)),
                  pl.BlockSpec((tb, LANES), at(C_DT // LANES)),
                  pl.BlockSpec((M_CONV, M_CONV_DIM), lambda b, t: (0, 0)), row(M_CONV_DIM),
                  row(LANES), row(LANES), row(W), row(W), st_spec, buf_spec],
        out_specs=[pl.BlockSpec((tb, W), at(0)), st_spec, buf_spec],
        out_shape=[jax.ShapeDtypeStruct((batch * rows_per_seq, W), BF16),
                   jax.ShapeDtypeStruct((batch, npair, 2 * M_HEAD_DIM, M_STATE), F32),
                   jax.ShapeDtypeStruct((batch, SUBLANES, M_CONV_DIM), F32)],
        scratch_shapes=[pltpu.VMEM((npair, 2 * M_HEAD_DIM, M_STATE), F32),
                        pltpu.VMEM((SUBLANES, M_CONV_DIM), F32),
                        pltpu.VMEM((cp + SUBLANES, M_CONV_DIM), F32)],
        compiler_params=_cparams("parallel", "arbitrary"),
        name="ssd",
    )(p, p, p, conv_w, conv_b, a_log, dt_bias, d_skip, norm_g, h0, buf0)


def _idx_sample_kernel(pt_ref, iq_ref, iw_ref, iknew_ref, *rest, pp, n_pages, topk, t_new):
    spg = SUBLANES // t_new
    page_refs = rest[:spg * pp]
    mask_ref = rest[spg * pp]
    key_ref = rest[spg * pp + 1]
    cut_ref = rest[spg * pp + 2]
    del pt_ref
    b = pl.program_id(0)
    c = pl.program_id(1)
    qs = SUBLANES
    nrow = mask_ref.shape[1]
    iq = iq_ref[0]
    wcol = iw_ref[0]
    r0 = pl.multiple_of(b * qs, qs)
    row = _iota((qs, LANES), 0)
    lane_q = _iota((qs, LANES), 1)

    def scores(keys_t):
        lg = _dot(iq, keys_t)
        sc = (jnp.maximum(lg, 0.0) * wcol).reshape(IDX_HEADS, qs, LANES).sum(axis=0)
        return sc + 0.0

    def own_rows(per_seq):
        out = per_seq[0]
        for g in range(1, spg):
            out = jnp.where(row >= g * t_new, per_seq[g], out)
        return out

    for j in range(pp):
        sc = own_rows([scores(page_refs[g * pp + j][0, 0].astype(BF16)) for g in range(spg)])
        key_ref[c * pp + j, pl.ds(r0, qs), :] = _to_key(sc)

    @pl.when(c == pl.num_programs(1) - 1)
    def _():
        sc = own_rows([scores(iknew_ref[g]) for g in range(spg)])
        sm = jnp.where(lane_q <= row % t_new, sc, MASK_VALUE)
        key_ref[n_pages, pl.ds(r0, qs), :] = jnp.where(lane_q < t_new, _to_key(sm), INT_MIN)

    @pl.when((c == pl.num_programs(1) - 1) & (b == pl.num_programs(0) - 1))
    def _():
        n_all = n_pages + 1
        lane = _iota((nrow, LANES), 1)

        def count(pred):
            def body(pg, acc):
                return acc + jnp.where(pred(key_ref[pg], pg), 1, 0)
            acc = lax.fori_loop(0, n_all, body, jnp.zeros((nrow, LANES), I32))
            return jnp.sum(acc, axis=1, keepdims=True)

        def count_ge(cand):
            cb = jnp.broadcast_to(cand, (nrow, LANES))
            return count(lambda blk, pg: blk >= cb)

        thr = jnp.broadcast_to(_kth_largest_key(count_ge, topk, (nrow, 1)), (nrow, LANES))
        n_gt = count(lambda blk, pg: blk > thr)
        n_eq = count(lambda blk, pg: blk == thr)
        need = topk - n_gt
        cut_ref[...] = jnp.full((nrow, LANES), 2 ** 30, I32)

        @pl.when(jnp.max(n_eq - need) > 0)
        def _():
            def body(t, ans):
                cand = ans + jnp.left_shift(jnp.int32(1), 15 - t)
                cb = jnp.broadcast_to(cand - 1, (nrow, LANES))
                upto = count(lambda blk, pg: (blk == thr) & ((pg * LANES + lane) <= cb))
                return jnp.where(upto < need, cand, ans)
            last = lax.fori_loop(0, 16, body, jnp.zeros((nrow, 1), I32))
            cut_ref[...] = jnp.broadcast_to(last, (nrow, LANES))

        last = cut_ref[...]
        lo = jnp.maximum(thr, VALID_KEY)
        tie_ok = thr > VALID_KEY

        def write(pg, carry):
            blk = key_ref[pg]
            sel = (blk > lo) | ((blk == thr) & tie_ok & ((pg * LANES + lane) <= last))
            mask_ref[pg] = jnp.where(sel, 0.0, MASK_VALUE)
            return carry

        lax.fori_loop(0, n_all, write, 0)


def _idx_sample(page_table, iq_rows, iw_col, ik_new_t, cache_kidx_t, layer, pp, t_new):
    nseq, n_pages = page_table.shape
    topk = min(TOPK_MAX, (n_pages * PAGE_SIZE + t_new) // 4)
    qs = SUBLANES
    spg = qs // t_new
    assert qs % t_new == 0 and nseq % spg == 0, (nseq, t_new)
    nrow = nseq * t_new

    def page_spec(g, j):
        return pl.BlockSpec((1, 1, IDX_DIM, PAGE_SIZE), lambda b, c, pt: (layer, pt[b * spg + g, c * pp + j], 0, 0))

    grid_spec = pltpu.PrefetchScalarGridSpec(
        num_scalar_prefetch=1,
        grid=(nseq // spg, n_pages // pp),
        in_specs=[pl.BlockSpec((1, qs * IDX_HEADS, IDX_DIM), lambda b, c, pt: (b, 0, 0)),
                  pl.BlockSpec((1, qs * IDX_HEADS, 1), lambda b, c, pt: (b, 0, 0)),
                  pl.BlockSpec((spg, IDX_DIM, PAGE_SIZE), lambda b, c, pt: (b, 0, 0))]
                 + [page_spec(g, j) for g in range(spg) for j in range(pp)],
        out_specs=pl.BlockSpec((n_pages + 1, nrow, LANES), lambda b, c, pt: (0, 0, 0)),
        scratch_shapes=[pltpu.VMEM((n_pages + 1, nrow, LANES), I32),
                        pltpu.VMEM((nrow, LANES), I32)],
    )
    return pl.pallas_call(
        functools.partial(_idx_sample_kernel, pp=pp, n_pages=n_pages, topk=topk, t_new=t_new),
        grid_spec=grid_spec,
        out_shape=jax.ShapeDtypeStruct((n_pages + 1, nrow, LANES), F32),
        compiler_params=_cparams("arbitrary", "arbitrary"),
        name="idx_sample",
    )(page_table, iq_rows, iw_col, ik_new_t, *([cache_kidx_t] * (spg * pp)))


def _attn_sample_kernel(pt_ref, q_ref, mask_ref, masknew_ref, knew_ref, vnew_ref, *rest, pp, t_new):
    k_refs = rest[:pp]
    v_refs = rest[pp:2 * pp]
    o_ref = rest[2 * pp]
    m_ref, l_ref, acc_ref = rest[2 * pp + 1:]
    del pt_ref
    c = pl.program_id(1)
    nrow = t_new * A_HEADS
    q = q_ref[0]

    @pl.when(c == 0)
    def _():
        m_ref[...] = jnp.full(m_ref.shape, MASK_VALUE, F32)
        l_ref[...] = jnp.zeros(l_ref.shape, F32)
        acc_ref[...] = jnp.zeros(acc_ref.shape, F32)

    def update(k_pages, v_pages, masks):
        s = []
        for kt, mk in zip(k_pages, masks):
            mrows = jnp.concatenate([jnp.broadcast_to(mk[t:t + 1, :], (A_HEADS, LANES)) for t in range(t_new)], axis=0)
            s.append(_dot(q, kt) + mrows)
        m_blk = s[0]
        for t in s[1:]:
            m_blk = jnp.maximum(m_blk, t)
        m_old = m_ref[...]
        m_new = jnp.maximum(m_old, jnp.max(m_blk, axis=1, keepdims=True))
        alpha = jnp.exp2(m_old - m_new)
        l_sum = jnp.zeros((nrow, LANES), F32)
        acc = alpha * acc_ref[...]
        for t, vt in zip(s, v_pages):
            e = jnp.exp2(t - m_new)
            l_sum = l_sum + e
            acc = acc + _nt(e.astype(BF16), vt)
        l_ref[...] = alpha * l_ref[...] + jnp.sum(l_sum, axis=1, keepdims=True)
        acc_ref[...] = acc
        m_ref[...] = m_new

    update([r[0, 0].astype(BF16) for r in k_refs], [r[0, 0].astype(BF16) for r in v_refs],
           [mask_ref[j] for j in range(pp)])

    @pl.when(c == pl.num_programs(1) - 1)
    def _():
        update([knew_ref[0]], [vnew_ref[0]], [masknew_ref[...]])
        o = acc_ref[...] / l_ref[...]
        grp = (_iota((nrow, 1), 0) % A_HEADS) // (A_HEADS // A_KV_HEADS)
        out = jnp.zeros((nrow, A_HEAD_DIM), F32)
        for g in range(A_KV_HEADS):
            out = out + jnp.where(grp == g, o[:, g * A_HEAD_DIM:(g + 1) * A_HEAD_DIM], 0.0)
        o_ref[0] = out


def _attn_sample(page_table, q_rows, mask, k_new_t, v_new_t, cache_k_t, cache_v_t, layer, pp, t_new):
    nseq, n_pages = page_table.shape
    kvw = A_KV_HEADS * A_HEAD_DIM
    nrow = t_new * A_HEADS
    mask4 = mask.reshape(n_pages + 1, nseq, t_new, LANES)

    def page_spec(j):
        return pl.BlockSpec((1, 1, kvw, PAGE_SIZE), lambda b, c, pt: (layer, pt[b, c * pp + j], 0, 0))

    grid_spec = pltpu.PrefetchScalarGridSpec(
        num_scalar_prefetch=1,
        grid=(nseq, n_pages // pp),
        in_specs=[pl.BlockSpec((1, nrow, kvw), lambda b, c, pt: (b, 0, 0)),
                  pl.BlockSpec((pp, None, t_new, LANES), lambda b, c, pt: (c, b, 0, 0)),
                  pl.BlockSpec((None, None, t_new, LANES), lambda b, c, pt: (n_pages, b, 0, 0)),
                  pl.BlockSpec((1, kvw, PAGE_SIZE), lambda b, c, pt: (b, 0, 0)),
                  pl.BlockSpec((1, kvw, PAGE_SIZE), lambda b, c, pt: (b, 0, 0))]
                 + [page_spec(j) for j in range(pp)] * 2,
        out_specs=pl.BlockSpec((1, nrow, A_HEAD_DIM), lambda b, c, pt: (b, 0, 0)),
        scratch_shapes=[pltpu.VMEM((nrow, 1), F32), pltpu.VMEM((nrow, 1), F32), pltpu.VMEM((nrow, kvw), F32)],
    )
    return pl.pallas_call(
        functools.partial(_attn_sample_kernel, pp=pp, t_new=t_new),
        grid_spec=grid_spec,
        out_shape=jax.ShapeDtypeStruct((nseq, nrow, A_HEAD_DIM), F32),
        compiler_params=_cparams("parallel", "arbitrary"),
        name="attn_sample",
    )(page_table, q_rows, mask4, mask4, k_new_t, v_new_t, *([cache_k_t] * pp), *([cache_v_t] * pp))


def _sprep_kernel(aq_ref, ak_ref, iq_ref, ikw_ref, qg_ref, kg_ref, qn_ref, kn_ref, iqs_ref, iw_ref):
    for c, t in enumerate(_head_rmsnorm(aq_ref[...], qg_ref[...])):
        qn_ref[:, c * LANES:(c + 1) * LANES] = (t * (A_HEAD_DIM ** -0.5 * LOG2_E)).astype(BF16)
    for c, t in enumerate(_head_rmsnorm(ak_ref[...], kg_ref[...])):
        kn_ref[:, c * LANES:(c + 1) * LANES] = t
    iqs_ref[...] = (iq_ref[...] * (IDX_DIM ** -0.5)).astype(BF16)
    iw_ref[...] = ikw_ref[...] * (IDX_HEADS ** -0.5)


def _sprep(p, q_gain, k_gain):
    n = p.shape[0]
    kvw = A_KV_HEADS * A_HEAD_DIM
    return pl.pallas_call(
        _sprep_kernel,
        grid=(1,),
        in_specs=[pl.BlockSpec((n, W), lambda i: (0, C_AQ // W)),
                  pl.BlockSpec((n, kvw), lambda i: (0, C_AK // kvw)),
                  pl.BlockSpec((n, W), lambda i: (0, C_IQ // W)),
                  pl.BlockSpec((n, LANES), lambda i: (0, C_IKW // LANES)),
                  pl.BlockSpec((1, LANES), lambda i: (0, 0)),
                  pl.BlockSpec((1, LANES), lambda i: (0, 0))],
        out_specs=[pl.BlockSpec((n, W), lambda i: (0, 0)), pl.BlockSpec((n, kvw), lambda i: (0, 0)),
                   pl.BlockSpec((n, W), lambda i: (0, 0)), pl.BlockSpec((n, LANES), lambda i: (0, 0))],
        out_shape=[jax.ShapeDtypeStruct((n, W), BF16), jax.ShapeDtypeStruct((n, kvw), F32),
                   jax.ShapeDtypeStruct((n, W), BF16), jax.ShapeDtypeStruct((n, LANES), F32)],
        compiler_params=_cparams("arbitrary"),
        name="sprep",
    )(p, p, p, p, q_gain, k_gain)


def _merge_kernel(oa_ref, ob_ref, oc_ref, od_ref, gates_ref, x_ref, wb_ref, wo_ref, n2_ref, x1_ref, h2_ref):
    merged = None
    for b, o_ref in enumerate((oa_ref, ob_ref, oc_ref, od_ref)):
        up = _dot(o_ref[...], wb_ref[b])
        t = _sigmoid(gates_ref[:, b * D_MODEL:(b + 1) * D_MODEL].astype(F32)) * up
        merged = t if merged is None else merged + t
    x1 = x_ref[...] + _dot(merged.astype(BF16), wo_ref[...])
    x1_ref[...] = x1
    ms = jnp.mean(x1 * x1, axis=-1, keepdims=True)
    h2_ref[...] = (x1 * lax.rsqrt(ms + RMS_EPS) * n2_ref[...]).astype(BF16)


def _merge(oa, ob, oc, od, p, x, w_branch, w_out, norm2, layer, tm):
    n = x.shape[0]
    row = lambda wdt: pl.BlockSpec((tm, wdt), lambda i: (i, 0))
    return pl.pallas_call(
        _merge_kernel,
        grid=(n // tm,),
        in_specs=[row(W), row(W), row(W), row(W),
                  pl.BlockSpec((tm, N_BRANCH * D_MODEL), lambda i: (i, C_GATES // (N_BRANCH * D_MODEL))),
                  row(D_MODEL),
                  pl.BlockSpec((None, N_BRANCH, W, D_MODEL), lambda i: (layer, 0, 0, 0)),
                  pl.BlockSpec((None, D_MODEL, D_MODEL), lambda i: (layer, 0, 0)),
                  pl.BlockSpec((1, D_MODEL), lambda i: (0, 0))],
        out_specs=[row(D_MODEL), row(D_MODEL)],
        out_shape=[jax.ShapeDtypeStruct((n, D_MODEL), F32), jax.ShapeDtypeStruct((n, D_MODEL), BF16)],
        compiler_params=_cparams("parallel"),
        name="merge",
    )(oa, ob, oc, od, p, x, w_branch, w_out, norm2)


def _ffn_kernel(h_ref, x_ref, wg_ref, wu_ref, wd_ref, o_ref, acc_ref):
    j = pl.program_id(1)

    @pl.when(j == 0)
    def _():
        acc_ref[...] = x_ref[...]

    h = h_ref[...]
    act = _silu(_dot(h, wg_ref[...])) * _dot(h, wu_ref[...])
    acc_ref[...] += _dot(act.astype(BF16), wd_ref[...])

    @pl.when(j == pl.num_programs(1) - 1)
    def _():
        o_ref[...] = acc_ref[...]


def _ffn(h2, x1, w_gu, w_down, layer, tm, fc):
    n = x1.shape[0]
    nf = D_FF // fc
    return pl.pallas_call(
        _ffn_kernel,
        grid=(n // tm, nf),
        in_specs=[pl.BlockSpec((tm, D_MODEL), lambda i, j: (i, 0)),
                  pl.BlockSpec((tm, D_MODEL), lambda i, j: (i, 0)),
                  pl.BlockSpec((None, D_MODEL, fc), lambda i, j: (layer, 0, j)),
                  pl.BlockSpec((None, D_MODEL, fc), lambda i, j: (layer, 0, nf + j)),
                  pl.BlockSpec((None, fc, D_MODEL), lambda i, j: (layer, j, 0))],
        out_specs=pl.BlockSpec((tm, D_MODEL), lambda i, j: (i, 0)),
        out_shape=jax.ShapeDtypeStruct((n, D_MODEL), F32),
        scratch_shapes=[pltpu.VMEM((tm, D_MODEL), F32)],
        compiler_params=_cparams("parallel", "arbitrary"),
        name="ffn",
    )(h2, x1, w_gu, w_gu, w_down)


def _pack_w_in(w_in):
    (a_q, a_k, a_v, a_iq, a_ik, a_iw, r_q, r_k, r_v, r_g, c_q, c_f, c_i, c_g,
     m_z, m_xbc, m_dt, gates) = jnp.split(w_in, SPLIT_POINTS, axis=-1)

    def pad_to_lanes(a):
        return jnp.pad(a, ((0, 0), (0, 0), (0, LANES - a.shape[-1])))

    ikw = pad_to_lanes(jnp.concatenate([a_ik, a_iw], axis=-1))
    packed = jnp.concatenate([gates, m_xbc, a_q, a_iq, r_q, r_k, r_v, r_g, c_q, c_f, c_i, c_g, m_z,
                              a_k, a_v, ikw, pad_to_lanes(m_dt)], axis=-1)
    return packed.astype(BF16)


def _rotary_tables(pos):
    half = R_DK // 2
    inv_freq = ROPE_BASE ** (-jnp.arange(half, dtype=F32) / half)
    ang = pos.astype(F32)[:, None] * inv_freq[None, :]
    c, s = jnp.cos(ang), jnp.sin(ang)
    return jnp.concatenate([c, c], axis=-1), jnp.concatenate([-s, s], axis=-1)


def _lower_bounds(lb_param):
    pr = jax.nn.softmax(lb_param.astype(F32), axis=0)
    return jnp.cumsum(pr, axis=0) - pr[0]


def _pad_lanes_row(v):
    return jnp.pad(v, (0, LANES - v.shape[0]))[None, :]


def kernel(x_prompt, x_sample, cache_k, cache_v, cache_kidx, state_ret, state_hgrn, state_ssm, state_conv,
           page_table, norm1_g, w_in, q_norm_g, k_norm_g, lb_param, hgrn_norm_g, A_log, dt_bias, D_skip,
           conv_w, conv_b, ssm_norm_g, w_branch, w_out, norm2_g, w_gu, w_down):
    bp, tp, _ = x_prompt.shape
    bs, ts, _ = x_sample.shape
    depth = w_in.shape[0]
    n_pool = cache_k.shape[1]
    n_pages = page_table.shape[1]
    kvw = A_KV_HEADS * A_HEAD_DIM
    tsp = SUBLANES
    npair = M_HEADS // 2

    w_in_p = _pack_w_in(w_in)
    w_branch_b = w_branch.astype(BF16)
    w_out_b = w_out.astype(BF16)
    w_gu_b = w_gu.astype(BF16)
    w_down_b = w_down.astype(BF16)
    lbs = _lower_bounds(lb_param)
    cos_p, sin_p = _rotary_tables(jnp.arange(tp, dtype=jnp.int32))
    pos_s = n_pages * PAGE_SIZE + jnp.arange(tsp, dtype=jnp.int32)
    cos_s, sin_s = _rotary_tables(pos_s)
    ck_t = jnp.transpose(cache_k, (0, 1, 3, 4, 2)).reshape(depth, n_pool, kvw, PAGE_SIZE)
    cv_t = jnp.transpose(cache_v, (0, 1, 3, 4, 2)).reshape(depth, n_pool, kvw, PAGE_SIZE)
    ckidx_t = jnp.transpose(cache_kidx, (0, 1, 3, 2))
    head_group = (jnp.arange(A_HEADS) // (A_HEADS // A_KV_HEADS))[:, None] == jnp.arange(A_KV_HEADS)[None, :]

    zero_ret = jnp.zeros((bp, R_HEADS, R_DK, R_DV), F32)
    zero_hgrn = jnp.zeros((bp, H_HEADS, H_DK, H_DV), F32)
    zero_ssm = jnp.zeros((bp, npair, 2 * M_HEAD_DIM, M_STATE), F32)
    zero_buf = jnp.zeros((bp, SUBLANES, M_CONV_DIM), F32)

    xp = x_prompt.reshape(bp * tp, D_MODEL)
    xs = x_sample.reshape(bs * ts, D_MODEL)
    outs_p = [[] for _ in range(7)]
    outs_s = [[] for _ in range(7)]
    for l in range(depth):
        n1 = norm1_g[l][None, :]
        n2 = norm2_g[l][None, :]
        qg = jnp.tile(q_norm_g[l], 2)[None, :]
        kg = jnp.tile(k_norm_g[l], 2)[None, :]
        lb = lbs[l][None, :]
        hng = hgrn_norm_g[l][None, :]
        alog = _pad_lanes_row(A_log[l])
        dtb = _pad_lanes_row(dt_bias[l])
        dsk = jnp.repeat(D_skip[l], M_HEAD_DIM)[None, :]
        sng = ssm_norm_g[l][None, :]
        cw = conv_w[l]
        cb = conv_b[l][None, :]

        p = _inproj(xp, n1, w_in_p, l, 1024, 1280, BF16)
        kn, vv, ik, knb, vb, ik2 = _kprep(p, kg, 512)
        oa = _attn_prompt(p, knb, vb, ik2, qg, bp, tp, 256, 256)
        ob, ret_new = _retention(p, cos_p, sin_p, zero_ret, bp, tp, 512, R_CHUNK, R_CHUNK)
        oc, hgrn_new = _hgrn2(p, lb, hng, zero_hgrn, bp, tp, 512, H_CHUNK, H_CHUNK, 16)
        od, ssm_new, buf_new = _ssd(p, cw, cb, alog, dtb, dsk, sng, zero_ssm, zero_buf, bp, tp, 512, M_CHUNK, M_CHUNK)
        x1, h2 = _merge(oa, ob, oc, od, p, xp, w_branch_b, w_out_b, n2, l, 512)
        xp = _ffn(h2, x1, w_gu_b, w_down_b, l, 512, D_FF // 2)
        for i, a in enumerate((kn.reshape(bp, tp, A_KV_HEADS, A_HEAD_DIM), vv.reshape(bp, tp, A_KV_HEADS, A_HEAD_DIM),
                               ik.reshape(bp, tp, IDX_DIM), ret_new, hgrn_new,
                               ssm_new.reshape(bp, M_HEADS, M_HEAD_DIM, M_STATE),
                               buf_new[:, SUBLANES - (M_CONV - 1):, :])):
            outs_p[i].append(a)

        ps = _inproj(xs, n1, w_in_p, l, bs * ts, 1280, F32)
        qn_s, kn_s, iq_s, iw_s = _sprep(ps, qg, kg)
        v_s = ps[:, C_AV:C_AV + kvw]
        ik_s = ps[:, C_IKW:C_IKW + IDX_DIM]
        pad_q = lambda a: jnp.pad(a, ((0, 0), (0, tsp - ts)) + ((0, 0),) * (a.ndim - 2))
        new_keys_t = lambda a: jnp.swapaxes(jnp.pad(a.reshape(bs, ts, -1), ((0, 0), (0, PAGE_SIZE - ts), (0, 0))), 1, 2)
        spg = tsp // ts
        head_major = lambda a: jnp.swapaxes(a.reshape(bs // spg, tsp, IDX_HEADS, -1), 1, 2).reshape(bs // spg, IDX_HEADS * tsp, -1)
        iq_rows = head_major(iq_s)
        iw_col = head_major(iw_s[:, IDX_DIM:IDX_DIM + IDX_HEADS])
        mask = _idx_sample(page_table, iq_rows, iw_col, new_keys_t(ik_s).astype(BF16), ckidx_t, l, min(16, n_pages), ts)
        q4 = qn_s.reshape(bs, ts, A_HEADS, 1, A_HEAD_DIM)
        q_rows = jnp.where(head_group[None, None, :, :, None], q4, jnp.zeros_like(q4)).reshape(bs, ts * A_HEADS, kvw)
        oa_s = _attn_sample(page_table, q_rows, mask, new_keys_t(kn_s).astype(BF16), new_keys_t(v_s).astype(BF16),
                            ck_t, cv_t, l, min(32, n_pages), ts)
        oa_s = oa_s.reshape(bs * ts, W).astype(BF16)
        psp = pad_q(ps.reshape(bs, ts, NP)).reshape(bs * tsp, NP)
        ssm0 = state_ssm[l].reshape(bs, npair, 2 * M_HEAD_DIM, M_STATE)
        buf0 = jnp.pad(state_conv[l], ((0, 0), (SUBLANES - (M_CONV - 1), 0), (0, 0)))
        ob_s, ret_s = _retention(psp, cos_s, sin_s, state_ret[l], bs, tsp, tsp, tsp, ts)
        oc_s, hgrn_s = _hgrn2(psp, lb, hng, state_hgrn[l], bs, tsp, tsp, tsp, ts, tsp)
        od_s, ssm_s, buf_s = _ssd(psp, cw, cb, alog, dtb, dsk, sng, ssm0, buf0, bs, tsp, tsp, tsp, ts)
        unpad = lambda a: a.reshape(bs, tsp, W)[:, :ts].reshape(bs * ts, W)
        x1s, h2s = _merge(oa_s, unpad(ob_s), unpad(oc_s), unpad(od_s), ps, xs, w_branch_b, w_out_b, n2, l, bs * ts)
        xs = _ffn(h2s, x1s, w_gu_b, w_down_b, l, bs * ts, D_FF // 2)
        for i, a in enumerate((kn_s.reshape(bs, ts, A_KV_HEADS, A_HEAD_DIM), v_s.reshape(bs, ts, A_KV_HEADS, A_HEAD_DIM),
                               ik_s.reshape(bs, ts, IDX_DIM), ret_s, hgrn_s,
                               ssm_s.reshape(bs, M_HEADS, M_HEAD_DIM, M_STATE),
                               buf_s[:, SUBLANES - (M_CONV - 1):, :])):
            outs_s[i].append(a)

    dtypes = (cache_k.dtype, cache_v.dtype, cache_kidx.dtype, state_ret.dtype,
              state_hgrn.dtype, state_ssm.dtype, state_conv.dtype)
    res_p = [jnp.stack(a).astype(d) for a, d in zip(outs_p, dtypes)]
    res_s = [jnp.stack(a).astype(d) for a, d in zip(outs_s, dtypes)]
    return (xp.reshape(bp, tp, D_MODEL), xs.reshape(bs, ts, D_MODEL), *res_p, *res_s)
```

```python
import functools
import math
import struct

import jax
import jax.numpy as jnp
from jax import lax
from jax.experimental import pallas as pl
from jax.experimental.pallas import tpu as pltpu

F32 = jnp.float32
BF16 = jnp.bfloat16
I32 = jnp.int32

D_MODEL = 1024
PAGE_SIZE = 128
BRANCH_WIDTH = D_MODEL // 2
N_BRANCH = 4
A_HEAD_DIM = 64
A_HEADS = BRANCH_WIDTH // A_HEAD_DIM
A_KV_HEADS = A_HEADS // 2
IDX_HEADS = 8
IDX_DIM = 64
TOPK_MAX = 256
Q_BLOCK = 128
R_HEADS = 4
R_DK = BRANCH_WIDTH // R_HEADS
R_DV = BRANCH_WIDTH // R_HEADS
R_CHUNK = 128
ROPE_BASE = 10000.0
H_HEADS = 4
H_DK = 128
H_DV = BRANCH_WIDTH // H_HEADS
H_CHUNK = 64
MIN_FORGET = 1e-30
SAFE_EXP = 60.0
M_HEAD_DIM = 64
M_HEADS = BRANCH_WIDTH // M_HEAD_DIM
M_INNER = M_HEADS * M_HEAD_DIM
M_GROUPS = 2
M_STATE = 128
M_CONV = 4
M_CHUNK = 128
M_CONV_DIM = M_INNER + 2 * M_GROUPS * M_STATE
D_FF = -(-8 * D_MODEL // (3 * 256)) * 256
RMS_EPS = 1e-6
GN_EPS = 1e-6
MASK_VALUE = -1e30

SPLIT_SIZES = (
    A_HEADS * A_HEAD_DIM, A_KV_HEADS * A_HEAD_DIM, A_KV_HEADS * A_HEAD_DIM,
    IDX_HEADS * IDX_DIM, IDX_DIM, IDX_HEADS,
    R_HEADS * R_DK, R_HEADS * R_DK, R_HEADS * R_DV, R_HEADS * R_DV,
    H_HEADS * H_DK, H_HEADS * H_DK, H_HEADS * H_DV, H_HEADS * H_DV,
    M_INNER, M_CONV_DIM, M_HEADS,
    N_BRANCH * D_MODEL,
)
SPLIT_POINTS = tuple(sum(SPLIT_SIZES[:i + 1]) for i in range(len(SPLIT_SIZES) - 1))

LANES = 128
SUBLANES = 8
VMEM_LIMIT = 56 * 1024 * 1024

C_GATES = 0
C_XBC = 4096
C_AQ = 5120
C_IQ = 5632
C_RQ = 6144
C_RK = 6656
C_RV = 7168
C_RG = 7680
C_CQ = 8192
C_CF = 8704
C_CI = 9216
C_CG = 9728
C_MZ = 10240
C_AK = 10752
C_AV = 11008
C_IKW = 11264
C_DT = 11392
NP = 11520
W = BRANCH_WIDTH


def _float_key(x):
    b = struct.unpack("<i", struct.pack("<f", x))[0]
    return b if b >= 0 else b ^ 0x7FFFFFFF


VALID_KEY = _float_key(0.5 * MASK_VALUE)
INT_MIN = -2 ** 31
LOG2_E = math.log2(math.e)


def _cparams(*sem):
    return pltpu.CompilerParams(dimension_semantics=sem, vmem_limit_bytes=VMEM_LIMIT)


def _nt(a, b):
    return lax.dot_general(a, b, (((1,), (1,)), ((), ())), preferred_element_type=F32)


def _tn(a, b):
    return lax.dot_general(a, b, (((0,), (0,)), ((), ())), preferred_element_type=F32)


def _dot(a, b):
    return jnp.dot(a, b, preferred_element_type=F32)


def _dot_exact_lhs(a01, x):
    a = a01.astype(BF16)
    x0 = x.astype(BF16)
    r1 = x - x0.astype(F32)
    x1 = r1.astype(BF16)
    x2 = (r1 - x1.astype(F32)).astype(BF16)
    return _dot(a, x0) + _dot(a, x1) + _dot(a, x2)


def _dot_exact_rhs(x, b01):
    b = b01.astype(BF16)
    x0 = x.astype(BF16)
    r1 = x - x0.astype(F32)
    x1 = r1.astype(BF16)
    x2 = (r1 - x1.astype(F32)).astype(BF16)
    return _dot(x0, b) + _dot(x1, b) + _dot(x2, b)


def _sigmoid(x):
    return 1.0 / (1.0 + jnp.exp(-x))


def _silu(x):
    return x * _sigmoid(x)


def _iota(shape, dim):
    return lax.broadcasted_iota(I32, shape, dim)


def _tri_incl(n):
    return (_iota((n, n), 0) >= _iota((n, n), 1)).astype(F32)


def _inproj_kernel(x_ref, g_ref, w_ref, o_ref, h_ref):
    @pl.when(pl.program_id(1) == 0)
    def _():
        x = x_ref[...]
        ms = jnp.mean(x * x, axis=-1, keepdims=True)
        h_ref[...] = (x * lax.rsqrt(ms + RMS_EPS) * g_ref[...]).astype(BF16)

    o_ref[...] = _dot(h_ref[...], w_ref[...]).astype(o_ref.dtype)


def _inproj(x, g, w, layer, tm, tn, out_dtype):
    n = x.shape[0]
    return pl.pallas_call(
        _inproj_kernel,
        grid=(n // tm, NP // tn),
        in_specs=[pl.BlockSpec((tm, D_MODEL), lambda i, j: (i, 0)),
                  pl.BlockSpec((1, D_MODEL), lambda i, j: (0, 0)),
                  pl.BlockSpec((None, D_MODEL, tn), lambda i, j: (layer, 0, j))],
        out_specs=pl.BlockSpec((tm, tn), lambda i, j: (i, j)),
        out_shape=jax.ShapeDtypeStruct((n, NP), out_dtype),
        scratch_shapes=[pltpu.VMEM((tm, D_MODEL), BF16)],
        compiler_params=_cparams("parallel", "arbitrary"),
        name="inproj",
    )(x, g, w)


def _head_rmsnorm(x, gain_row):
    pair = (_iota((LANES, LANES), 0) // A_HEAD_DIM == _iota((LANES, LANES), 1) // A_HEAD_DIM).astype(F32)
    outs = []
    for c in range(x.shape[1] // LANES):
        xc = x[:, c * LANES:(c + 1) * LANES]
        ms = _dot_exact_rhs(xc * xc, pair) * (1.0 / A_HEAD_DIM)
        outs.append(xc * lax.rsqrt(ms + RMS_EPS) * gain_row)
    return outs


def _kth_largest_key(count_ge, k, shape):
    nonneg = count_ge(jnp.zeros(shape, I32)) >= k
    prefix = jnp.where(nonneg, 0, INT_MIN).astype(I32)

    def body(t, prefix):
        cand = prefix + jnp.left_shift(jnp.int32(1), 30 - t)
        return jnp.where(count_ge(cand) >= k, cand, prefix)

    return lax.fori_loop(0, 31, body, prefix)


def _to_key(s):
    b = lax.bitcast_convert_type(s, I32)
    return jnp.where(b >= 0, b, b ^ 0x7FFFFFFF)


def _kprep_kernel(ak_ref, av_ref, ikw_ref, kg_ref, kn_ref, v_ref, ik_ref, knb_ref, vt_ref, ik2_ref):
    kn = _head_rmsnorm(ak_ref[...].astype(F32), kg_ref[...])
    for c, t in enumerate(kn):
        kn_ref[:, c * LANES:(c + 1) * LANES] = t
        knb_ref[:, c * LANES:(c + 1) * LANES] = t.astype(BF16)
    v = av_ref[...].astype(F32)
    v_ref[...] = v
    vt_ref[...] = v.T.astype(BF16)
    ikw = ikw_ref[...].astype(F32)
    ik_ref[...] = ikw[:, :IDX_DIM]
    lane = _iota(ikw.shape, 1)
    ik2_ref[...] = jnp.where(lane < IDX_DIM, ikw, pltpu.roll(ikw, IDX_DIM, 1)).astype(BF16)


def _kprep(p, k_gain, tm):
    n = p.shape[0]
    kvw = A_KV_HEADS * A_HEAD_DIM
    row = lambda wdt: pl.BlockSpec((tm, wdt), lambda i: (i, 0))
    return pl.pallas_call(
        _kprep_kernel,
        grid=(n // tm,),
        in_specs=[pl.BlockSpec((tm, kvw), lambda i: (i, C_AK // kvw)),
                  pl.BlockSpec((tm, kvw), lambda i: (i, C_AV // kvw)),
                  pl.BlockSpec((tm, LANES), lambda i: (i, C_IKW // LANES)),
                  pl.BlockSpec((1, LANES), lambda i: (0, 0))],
        out_specs=[row(kvw), row(kvw), row(IDX_DIM), row(kvw), pl.BlockSpec((kvw, tm), lambda i: (0, i)), row(LANES)],
        out_shape=[jax.ShapeDtypeStruct((n, kvw), F32),
                   jax.ShapeDtypeStruct((n, kvw), F32),
                   jax.ShapeDtypeStruct((n, IDX_DIM), F32),
                   jax.ShapeDtypeStruct((n, kvw), BF16),
                   jax.ShapeDtypeStruct((kvw, n), BF16),
                   jax.ShapeDtypeStruct((n, LANES), BF16)],
        compiler_params=_cparams("parallel"),
        name="kprep",
    )(p, p, p, k_gain)


def _attn_prompt_kernel(aq_ref, iq_ref, ikw_ref, qg_ref, knb_ref, vt_ref, ik2_ref, o_ref,
                        key_ref, msk_ref, s_ref, qt_ref, iqt_ref, acc_ref, cut_ref, *, topk, kb, qb):
    i = pl.program_id(1)
    nblk = (i * qb + qb + kb - 1) // kb
    row = _iota((kb, qb), 0)
    lane = _iota((kb, qb), 1)
    lane_q = _iota((qb, LANES), 1)
    own_half = [lane_q < A_HEAD_DIM, lane_q >= A_HEAD_DIM]
    n_pairs = A_KV_HEADS // 2
    hpp = A_HEADS // n_pairs
    nsub = kb // SUBLANES

    qn = _head_rmsnorm(aq_ref[...].astype(F32), qg_ref[...])
    for h in range(A_HEADS):
        src, dst = h % 2, (h // 2) % 2
        t = qn[h // 2] * (A_HEAD_DIM ** -0.5 * LOG2_E)
        if src != dst:
            t = pltpu.roll(t, A_HEAD_DIM, 1)
        t = jnp.where(own_half[dst], t, 0.0)
        qt_ref[h // hpp, :, (h % hpp) * qb:(h % hpp + 1) * qb] = t.T.astype(BF16)
    for h in range(IDX_HEADS):
        t = iq_ref[:, (h // 2) * LANES:(h // 2 + 1) * LANES].astype(F32) * (IDX_DIM ** -0.5)
        iqt_ref[:, h * qb:(h + 1) * qb] = jnp.where(own_half[h % 2], t, 0.0).T.astype(BF16)
    iw_t = ikw_ref[...].astype(F32).T[IDX_DIM:IDX_DIM + IDX_HEADS, :] * (IDX_HEADS ** -0.5)

    def score_blk(j, carry):
        ks = pl.multiple_of(j * kb, kb)
        lg = _dot(ik2_ref[pl.ds(ks, kb), :], iqt_ref[...])
        acc = jnp.zeros((kb, qb), F32)
        for h in range(IDX_HEADS):
            acc = acc + jnp.maximum(lg[:, h * qb:(h + 1) * qb], 0.0) * iw_t[h:h + 1, :]
        adm = (ks + row) <= (i * qb + lane)
        key_ref[j] = _to_key(jnp.where(adm, acc, MASK_VALUE))
        return carry

    lax.fori_loop(0, nblk, score_blk, 0)

    def count(pred):
        def body(j, c):
            return c + jnp.sum(jnp.where(pred(key_ref[j], j), 1, 0).reshape(nsub, SUBLANES, qb), axis=0)
        c = lax.fori_loop(0, nblk, body, jnp.zeros((SUBLANES, qb), I32))
        return jnp.sum(c, axis=0, keepdims=True)

    cut_ref[0:1, :] = jnp.full((1, qb), INT_MIN, I32)
    cut_ref[1:2, :] = jnp.full((1, qb), 2 ** 30, I32)

    @pl.when(i * qb + qb > topk)
    def _():
        thr = _kth_largest_key(lambda cand: count(lambda blk, j: blk >= cand), topk, (1, qb))
        cut_ref[0:1, :] = thr
        n_gt = count(lambda blk, j: blk > thr)
        n_eq = count(lambda blk, j: blk == thr)
        need = topk - n_gt

        @pl.when(jnp.max(n_eq - need) > 0)
        def _():
            def body(t, ans):
                cand = ans + jnp.left_shift(jnp.int32(1), 15 - t)
                upto = count(lambda blk, j: (blk == thr) & ((j * kb + row) <= cand - 1))
                return jnp.where(upto < need, cand, ans)

            cut_ref[1:2, :] = lax.fori_loop(0, 16, body, jnp.zeros((1, qb), I32))

    thr = cut_ref[0:1, :]
    last = cut_ref[1:2, :]
    lo = jnp.maximum(thr, VALID_KEY)
    tie_ok = thr > VALID_KEY

    def mask_blk(j, carry):
        blk = key_ref[j]
        sel = (blk > lo) | ((blk == thr) & tie_ok & ((j * kb + row) <= last))
        msk_ref[j] = jnp.where(sel, 0.0, MASK_VALUE)
        return carry

    lax.fori_loop(0, nblk, mask_blk, 0)

    def fold(t, op):
        return op(t.reshape(nsub, SUBLANES, qb), axis=0)

    def logits_blk(j, m8):
        ks = pl.multiple_of(j * kb, kb)
        mk = msk_ref[j]
        out = []
        for pr in range(n_pairs):
            s4 = _dot(knb_ref[pl.ds(ks, kb), pr * LANES:(pr + 1) * LANES], qt_ref[pr])
            for hh in range(hpp):
                h = pr * hpp + hh
                s = s4[:, hh * qb:(hh + 1) * qb] + mk
                s_ref[j, h] = s
                out.append(jnp.maximum(m8[h], fold(s, jnp.max)))
        return tuple(out)

    m8 = lax.fori_loop(0, nblk, logits_blk, tuple(jnp.full((SUBLANES, qb), MASK_VALUE, F32) for _ in range(A_HEADS)))
    m_row = [jnp.max(t, axis=0, keepdims=True) for t in m8]

    acc_ref[...] = jnp.zeros(acc_ref.shape, F32)

    def values_blk(j, l8):
        ks = pl.multiple_of(j * kb, kb)
        out = []
        for h in range(A_HEADS):
            g = h // (A_HEADS // A_KV_HEADS)
            e = jnp.exp2(s_ref[j, h] - m_row[h])
            out.append(l8[h] + fold(e, jnp.sum))
            vt = vt_ref[g * A_HEAD_DIM:(g + 1) * A_HEAD_DIM, pl.ds(ks, kb)]
            acc_ref[h] = acc_ref[h] + _dot(vt, e.astype(BF16))
        return tuple(out)

    l8 = lax.fori_loop(0, nblk, values_blk, tuple(jnp.zeros((SUBLANES, qb), F32) for _ in range(A_HEADS)))

    for h in range(A_HEADS):
        l_row = jnp.sum(l8[h], axis=0, keepdims=True)
        o_ref[:, h * A_HEAD_DIM:(h + 1) * A_HEAD_DIM] = (acc_ref[h] / l_row).T.astype(o_ref.dtype)


def _attn_prompt(p, knb, vt, ik2, q_gain, batch, seq, kb, qb):
    nqb = seq // qb
    nkb = seq // kb
    kvw = A_KV_HEADS * A_HEAD_DIM
    topk = min(TOPK_MAX, seq // 4)
    n_pairs = A_KV_HEADS // 2
    hpp = A_HEADS // n_pairs
    return pl.pallas_call(
        functools.partial(_attn_prompt_kernel, topk=topk, kb=kb, qb=qb),
        grid=(batch, nqb),
        in_specs=[pl.BlockSpec((qb, W), lambda b, i: (b * nqb + i, C_AQ // W)),
                  pl.BlockSpec((qb, W), lambda b, i: (b * nqb + i, C_IQ // W)),
                  pl.BlockSpec((qb, LANES), lambda b, i: (b * nqb + i, C_IKW // LANES)),
                  pl.BlockSpec((1, LANES), lambda b, i: (0, 0)),
                  pl.BlockSpec((seq, kvw), lambda b, i: (b, 0)),
                  pl.BlockSpec((kvw, seq), lambda b, i: (0, b)),
                  pl.BlockSpec((seq, LANES), lambda b, i: (b, 0))],
        out_specs=pl.BlockSpec((qb, W), lambda b, i: (b * nqb + i, 0)),
        out_shape=jax.ShapeDtypeStruct((batch * seq, W), BF16),
        scratch_shapes=[pltpu.VMEM((nkb, kb, qb), I32),
                        pltpu.VMEM((nkb, kb, qb), F32),
                        pltpu.VMEM((nkb, A_HEADS, kb, qb), F32),
                        pltpu.VMEM((n_pairs, LANES, hpp * qb), BF16),
                        pltpu.VMEM((LANES, IDX_HEADS * qb), BF16),
                        pltpu.VMEM((A_HEADS, A_HEAD_DIM, qb), F32),
                        pltpu.VMEM((SUBLANES, qb), I32)],
        compiler_params=_cparams("parallel", "arbitrary"),
        name="attn_prompt",
    )(p, p, p, q_gain, knb, vt, ik2)


def _ret_body(q_ref, k_ref, v_ref, g_ref, cos_ref, sin_ref, s0_ref, o_ref, s_out_ref, st_ref, cp, c_true):
    tb = pl.program_id(1)

    @pl.when(tb == 0)
    def _():
        st_ref[...] = s0_ref[...]

    row = _iota((cp, 1), 0).astype(F32)
    diff = (_iota((cp, cp), 0) - _iota((cp, cp), 1)).astype(F32)

    def chunk(c, carry):
        r0 = pl.multiple_of(c * cp, cp)
        rows = pl.ds(r0, cp)
        cos = cos_ref[rows, :]
        sin = sin_ref[rows, :]
        for h in range(R_HEADS):
            lg = math.log1p(-2.0 ** (-5.0 - h))
            cols = slice(h * R_DK, (h + 1) * R_DK)
            q = q_ref[rows, cols].astype(F32)
            k = k_ref[rows, cols].astype(F32)
            v = v_ref[rows, cols].astype(F32)
            q = q * cos + pltpu.roll(q, R_DK // 2, 1) * sin
            k = (k * cos + pltpu.roll(k, R_DK // 2, 1) * sin) * (R_DK ** -0.5)
            decay = jnp.where(diff >= 0, jnp.exp(lg * jnp.maximum(diff, 0.0)), 0.0)
            scores = _nt(q.astype(BF16), k.astype(BF16)) * decay
            s = st_ref[h]
            o = _dot(scores.astype(BF16), v.astype(BF16))
            o = o + _dot((q * jnp.exp(lg * (row + 1.0))).astype(BF16), s.astype(BF16))
            kd = jnp.where(row < c_true, jnp.exp(lg * jnp.maximum(c_true - 1.0 - row, 0.0)), 0.0)
            st_ref[h] = math.exp(lg * c_true) * s + _tn((k * kd).astype(BF16), v.astype(BF16))
            mu = jnp.mean(o, axis=-1, keepdims=True)
            d = o - mu
            var = jnp.mean(d * d, axis=-1, keepdims=True)
            o_ref[rows, cols] = (_silu(g_ref[rows, cols].astype(F32)) * (d * lax.rsqrt(var + GN_EPS))).astype(o_ref.dtype)
        return carry

    n_chunks = q_ref.shape[0] // cp
    lax.fori_loop(0, n_chunks, chunk, 0, unroll=2 if n_chunks % 2 == 0 else 1)

    @pl.when(tb == pl.num_programs(1) - 1)
    def _():
        s_out_ref[...] = st_ref[...]


def _seq_views(gseq, row_refs, seq_refs):
    rows = row_refs[0].shape[0] // gseq
    for g in range(gseq):
        yield [r.at[pl.ds(g * rows, rows)] for r in row_refs], [r.at[g] for r in seq_refs]


def _ret_kernel(q_ref, k_ref, v_ref, g_ref, cos_ref, sin_ref, s0_ref, o_ref, s_out_ref, st_ref,
                *, cp, c_true, gseq):
    for (q, k, v, g, o), (s0, s_out, st) in _seq_views(gseq, (q_ref, k_ref, v_ref, g_ref, o_ref), (s0_ref, s_out_ref, st_ref)):
        _ret_body(q, k, v, g, cos_ref, sin_ref, s0, o, s_out, st, cp, c_true)


def _hgrn_kernel(q_ref, f_ref, i_ref, g_ref, lb_ref, ng_ref, s0_ref, o_ref, s_out_ref,
                 st_ref, qs_ref, ks_ref, gs_ref, *, cp, c_true, sb, gseq):
    for (q, f, i, g, o, qs, ks, gs), (s0, s_out, st) in _seq_views(
            gseq, (q_ref, f_ref, i_ref, g_ref, o_ref, qs_ref, ks_ref, gs_ref), (s0_ref, s_out_ref, st_ref)):
        _hgrn_body(q, f, i, g, lb_ref, ng_ref, s0, o, s_out, st, qs, ks, gs, cp, c_true, sb)


def _hgrn_body(q_ref, f_ref, i_ref, g_ref, lb_ref, ng_ref, s0_ref, o_ref, s_out_ref,
               st_ref, qs_ref, ks_ref, gs_ref, cp, c_true, sb):
    tb = pl.program_id(1)

    @pl.when(tb == 0)
    def _():
        st_ref[...] = s0_ref[...]

    tri = _tri_incl(cp)
    rowi = _iota((cp, 1), 0)
    real = rowi < c_true
    sub_n = _iota((sb, 1), 0)
    ng = ng_ref[...]
    n_chunks = q_ref.shape[0] // cp
    unroll = 2 if n_chunks % 2 == 0 else 1
    n_sub = cp // sb

    def prepare(c, spread):
        rows = pl.ds(pl.multiple_of(c * cp, cp), cp)
        for h in range(H_HEADS):
            cols = slice(h * H_DK, (h + 1) * H_DK)
            lb = lb_ref[:, cols]
            f = f_ref[rows, cols].astype(F32)
            forget = lb + (1.0 - lb) * _sigmoid(f)
            logf = jnp.where(real, jnp.log(jnp.maximum(forget, MIN_FORGET)), 0.0)
            gc = _dot_exact_lhs(tri, logf)
            qs_ref[rows, cols] = _silu(q_ref[rows, cols].astype(F32)) * (H_DK ** -0.5)
            ks_ref[rows, cols] = jnp.where(real, (1.0 - lb) * _sigmoid(-f), 0.0)
            gs_ref[rows, cols] = gc
            for b in range(n_sub):
                spread = jnp.maximum(spread, gc[b * sb:b * sb + 1, :] - gc[b * sb + sb - 1:b * sb + sb, :])
        return spread

    spread = lax.fori_loop(0, n_chunks, prepare, jnp.zeros((1, H_DK), F32), unroll=unroll)
    safe = jnp.max(spread) < SAFE_EXP

    blk_of_row = rowi // sb
    causal = _iota((cp, cp), 0) >= _iota((cp, cp), 1)

    def intra_factored(q, k, gc):
        gref_rows = jnp.concatenate([jnp.broadcast_to(gc[b * sb:b * sb + 1, :], (sb, H_DK)) for b in range(n_sub)], axis=0)
        qd = q * jnp.exp(gc - gref_rows)
        lhs, rhs = [], []
        for b in range(n_sub):
            lhs.append(jnp.where(blk_of_row == b, qd, 0.0).astype(BF16))
            kd = k * jnp.exp(jnp.minimum(gc[b * sb:b * sb + 1, :] - gc, SAFE_EXP))
            rhs.append(jnp.where(rowi < (b + 1) * sb, kd, 0.0).astype(BF16))
        cat = (lambda t: t[0]) if n_sub == 1 else (lambda t: jnp.concatenate(t, axis=1))
        return jnp.where(causal, _nt(cat(lhs), cat(rhs)), 0.0)

    def intra_by_column(q, k, gc):
        rows_out = []
        for b in range(n_sub):
            b0 = b * sb
            gi = gc[b0:b0 + sb, :]
            qi = q[b0:b0 + sb, :]
            ki = k[b0:b0 + sb, :]
            lane_a = _iota((sb, cp), 1)
            if b0 > 0:
                gref = gc[b0:b0 + 1, :]
                kd = jnp.where(rowi < b0, k * jnp.exp(jnp.minimum(gref - gc, 0.0)), 0.0)
                a = _nt((qi * jnp.exp(gi - gref)).astype(BF16), kd.astype(BF16))
            else:
                a = jnp.zeros((sb, cp), F32)
            for m in range(sb):
                e = jnp.exp(jnp.where(sub_n >= m, gi - gi[m:m + 1, :], MASK_VALUE))
                col = jnp.sum(qi * e * ki[m:m + 1, :], axis=1, keepdims=True)
                a = jnp.where(lane_a == b0 + m, col, a)
            rows_out.append(a)
        return rows_out[0] if n_sub == 1 else jnp.concatenate(rows_out, axis=0)

    def make_chunk(intra):
        def chunk(c, carry):
            rows = pl.ds(pl.multiple_of(c * cp, cp), cp)
            for h in range(H_HEADS):
                cols = slice(h * H_DK, (h + 1) * H_DK)
                q = qs_ref[rows, cols]
                k = ks_ref[rows, cols]
                gc = gs_ref[rows, cols]
                vb = jnp.where(real, i_ref[rows, cols].astype(F32), 0.0).astype(BF16)
                sbf = st_ref[h].astype(BF16)
                qe = (q * jnp.exp(gc)).astype(BF16)
                am = intra(q, k, gc).astype(BF16)
                if cp % LANES == 0 or cp * 2 == LANES:
                    o = _dot(jnp.concatenate([qe, am], axis=1), jnp.concatenate([sbf, vb], axis=0))
                else:
                    o = _dot(qe, sbf) + _dot(am, vb)
                g_last = gc[cp - 1:cp, :]
                kd = k * jnp.exp(g_last - gc)
                eg_col = jnp.broadcast_to(jnp.exp(g_last), (SUBLANES, H_DK)).T[:, 0:1]
                st_ref[h] = eg_col * st_ref[h] + _tn(kd.astype(BF16), vb)
                ms = jnp.mean(o * o, axis=-1, keepdims=True)
                o = o * lax.rsqrt(ms + RMS_EPS) * ng
                o_ref[rows, cols] = (_silu(g_ref[rows, cols].astype(F32)) * o).astype(o_ref.dtype)
            return carry
        return chunk

    @pl.when(safe)
    def _():
        lax.fori_loop(0, n_chunks, make_chunk(intra_factored), 0, unroll=unroll)

    @pl.when(jnp.logical_not(safe))
    def _():
        lax.fori_loop(0, n_chunks, make_chunk(intra_by_column), 0)

    @pl.when(tb == pl.num_programs(1) - 1)
    def _():
        s_out_ref[...] = st_ref[...]


def _softplus(x):
    return jnp.maximum(x, 0.0) + jnp.log1p(jnp.exp(-jnp.abs(x)))


def _ssd_kernel(z_ref, xbc_ref, dt_ref, cw_ref, cb_ref, alog_ref, dtb_ref, dskip_ref, ng_ref,
                h0_ref, buf0_ref, o_ref, h_out_ref, buf_out_ref, st_ref, tail_ref, ext_ref,
                *, cp, c_true, gseq):
    for (z, xbc, dt, o), (h0, buf0, h_out, buf_out, st, tail, ext) in _seq_views(
            gseq, (z_ref, xbc_ref, dt_ref, o_ref), (h0_ref, buf0_ref, h_out_ref, buf_out_ref, st_ref, tail_ref, ext_ref)):
        _ssd_body(z, xbc, dt, cw_ref, cb_ref, alog_ref, dtb_ref, dskip_ref, ng_ref, h0, buf0, o, h_out, buf_out,
                  st, tail, ext, cp, c_true)


def _ssd_body(z_ref, xbc_ref, dt_ref, cw_ref, cb_ref, alog_ref, dtb_ref, dskip_ref, ng_ref,
              h0_ref, buf0_ref, o_ref, h_out_ref, buf_out_ref, st_ref, tail_ref, ext_ref, cp, c_true):
    tb = pl.program_id(1)

    @pl.when(tb == 0)
    def _():
        st_ref[...] = h0_ref[...]
        tail_ref[...] = buf0_ref[...]

    tri = _tri_incl(cp)
    rowi = _iota((cp, 1), 0)
    real = rowi < c_true
    causal = _iota((cp, cp), 0) >= _iota((cp, cp), 1)
    lane = _iota((1, LANES), 1)
    left = lane < M_HEAD_DIM
    top = _iota((LANES, 1), 0) < M_HEAD_DIM
    cw = cw_ref[...]
    cb = cb_ref[...]
    neg_a = -jnp.exp(alog_ref[...])
    gw = M_INNER // M_GROUPS

    def chunk(c, carry):
        r0 = pl.multiple_of(c * cp, cp)
        rows = pl.ds(r0, cp)
        u = xbc_ref[rows, :].astype(F32)
        ext_ref[0:SUBLANES, :] = tail_ref[...]
        ext_ref[SUBLANES:SUBLANES + cp, :] = u
        y = cb + cw[M_CONV - 1:M_CONV, :] * u
        for j in range(M_CONV - 1):
            y = y + cw[j:j + 1, :] * ext_ref[pl.ds(SUBLANES - (M_CONV - 1) + j, cp), :]
        tail_ref[...] = ext_ref[pl.ds(c_true, SUBLANES), :]
        xbc = _silu(y)
        dt = jnp.where(real, _softplus(dt_ref[rows, :].astype(F32) + dtb_ref[...]), 0.0)
        a = neg_a * dt
        cum = _dot_exact_lhs(tri, a)
        cum_t = cum.T
        dt_t = dt.T
        e_cum = jnp.exp(cum)
        cum_last = cum[cp - 1:cp, :]
        w_all = jnp.exp(cum_last - cum) * dt
        e_last = jnp.exp(cum_last)
        z = z_ref[rows, :].astype(F32)
        ys = []
        for pr in range(M_HEADS // 2):
            g = (2 * pr) // (M_HEADS // M_GROUPS)
            bm = xbc[:, M_INNER + g * M_STATE:M_INNER + (g + 1) * M_STATE].astype(BF16)
            cm = xbc[:, M_INNER + (M_GROUPS + g) * M_STATE:M_INNER + (M_GROUPS + g + 1) * M_STATE].astype(BF16)
            xp = xbc[:, pr * LANES:(pr + 1) * LANES]
            xpb = xp.astype(BF16)
            cbm = _nt(cm, bm)
            hp = st_ref[pr]
            cross = _nt(cm, hp.astype(BF16))
            intra = []
            for hh in range(2):
                h = 2 * pr + hh
                seg = jnp.exp(jnp.where(causal, cum[:, h:h + 1] - cum_t[h:h + 1, :], MASK_VALUE))
                mat = cbm * seg * dt_t[h:h + 1, :]
                intra.append(_dot(mat.astype(BF16), xpb))
            h0, h1 = 2 * pr, 2 * pr + 1
            y_pair = (jnp.where(left, intra[0], intra[1])
                      + cross * jnp.where(left, e_cum[:, h0:h0 + 1], e_cum[:, h1:h1 + 1]))
            wx = xp * jnp.where(left, w_all[:, h0:h0 + 1], w_all[:, h1:h1 + 1])
            decay = jnp.where(top, e_last[:, h0:h0 + 1], e_last[:, h1:h1 + 1])
            st_ref[pr] = decay * hp + _tn(wx.astype(BF16), bm)
            y_pair = y_pair + dskip_ref[:, pr * LANES:(pr + 1) * LANES] * xp
            ys.append(y_pair * _silu(z[:, pr * LANES:(pr + 1) * LANES]))
        for g in range(M_GROUPS):
            yg = jnp.concatenate(ys[g * 2:(g + 1) * 2], axis=1)
            ms = jnp.mean(yg * yg, axis=-1, keepdims=True)
            o_ref[rows, g * gw:(g + 1) * gw] = (yg * lax.rsqrt(ms + RMS_EPS)
                                                * ng_ref[:, g * gw:(g + 1) * gw]).astype(o_ref.dtype)
        return carry

    lax.fori_loop(0, z_ref.shape[0] // cp, chunk, 0)

    @pl.when(tb == pl.num_programs(1) - 1)
    def _():
        h_out_ref[...] = st_ref[...]
        buf_out_ref[...] = tail_ref[...]


def _row_blocks(batch, rows_per_seq, tb, gseq):
    n_tb = rows_per_seq // tb
    assert batch % gseq == 0 and (gseq == 1 or n_tb == 1), (batch, rows_per_seq, tb, gseq)
    return n_tb, (lambda col: (lambda b, t: (b * n_tb + t, col)))


def _retention(p, cos2, sin2, s0, batch, rows_per_seq, tb, cp, c_true, gseq=1):
    n_tb, at = _row_blocks(batch, rows_per_seq, tb, gseq)
    st_spec = pl.BlockSpec((gseq, R_HEADS, R_DK, R_DV), lambda b, t: (b, 0, 0, 0))
    rb = gseq * tb
    return pl.pallas_call(
        functools.partial(_ret_kernel, cp=cp, c_true=c_true, gseq=gseq),
        grid=(batch // gseq, n_tb),
        in_specs=[pl.BlockSpec((rb, W), at(C_RQ // W)), pl.BlockSpec((rb, W), at(C_RK // W)),
                  pl.BlockSpec((rb, W), at(C_RV // W)), pl.BlockSpec((rb, W), at(C_RG // W)),
                  pl.BlockSpec((tb, R_DK), lambda b, t: (t, 0)), pl.BlockSpec((tb, R_DK), lambda b, t: (t, 0)),
                  st_spec],
        out_specs=[pl.BlockSpec((rb, W), at(0)), st_spec],
        out_shape=[jax.ShapeDtypeStruct((batch * rows_per_seq, W), BF16),
                   jax.ShapeDtypeStruct((batch, R_HEADS, R_DK, R_DV), F32)],
        scratch_shapes=[pltpu.VMEM((gseq, R_HEADS, R_DK, R_DV), F32)],
        compiler_params=_cparams("parallel", "arbitrary"),
        name="retention",
    )(p, p, p, p, cos2, sin2, s0)


def _hgrn2(p, lb, norm_g, s0, batch, rows_per_seq, tb, cp, c_true, sb, gseq=1):
    n_tb, at = _row_blocks(batch, rows_per_seq, tb, gseq)
    st_spec = pl.BlockSpec((gseq, H_HEADS, H_DK, H_DV), lambda b, t: (b, 0, 0, 0))
    rb = gseq * tb
    return pl.pallas_call(
        functools.partial(_hgrn_kernel, cp=cp, c_true=c_true, sb=sb, gseq=gseq),
        grid=(batch // gseq, n_tb),
        in_specs=[pl.BlockSpec((rb, W), at(C_CQ // W)), pl.BlockSpec((rb, W), at(C_CF // W)),
                  pl.BlockSpec((rb, W), at(C_CI // W)), pl.BlockSpec((rb, W), at(C_CG // W)),
                  pl.BlockSpec((1, W), lambda b, t: (0, 0)), pl.BlockSpec((1, H_DV), lambda b, t: (0, 0)),
                  st_spec],
        out_specs=[pl.BlockSpec((rb, W), at(0)), st_spec],
        out_shape=[jax.ShapeDtypeStruct((batch * rows_per_seq, W), BF16),
                   jax.ShapeDtypeStruct((batch, H_HEADS, H_DK, H_DV), F32)],
        scratch_shapes=[pltpu.VMEM((gseq, H_HEADS, H_DK, H_DV), F32)] + [pltpu.VMEM((rb, W), F32)] * 3,
        compiler_params=_cparams("parallel", "arbitrary"),
        name="hgrn2",
    )(p, p, p, p, lb, norm_g, s0)


def _ssd(p, conv_w, conv_b, a_log, dt_bias, d_skip, norm_g, h0, buf0, batch, rows_per_seq, tb, cp, c_true, gseq=1):
    n_tb, at = _row_blocks(batch, rows_per_seq, tb, gseq)
    rb = gseq * tb
    npair = M_HEADS // 2
    st_spec = pl.BlockSpec((gseq, npair, 2 * M_HEAD_DIM, M_STATE), lambda b, t: (b, 0, 0, 0))
    buf_spec = pl.BlockSpec((gseq, SUBLANES, M_CONV_DIM), lambda b, t: (b, 0, 0))
    row = lambda wdt: pl.BlockSpec((1, wdt), lambda b, t: (0, 0))
    return pl.pallas_call(
        functools.partial(_ssd_kernel, cp=cp, c_true=c_true, gseq=gseq),
        grid=(batch // gseq, n_tb),
        in_specs=[pl.BlockSpec((rb, W), at(C_MZ // W)), pl.BlockSpec((rb, M_CONV_DIM), at(C_XBC // M_CONV_DIM)),
                  pl.BlockSpec((rb, LANES), at(C_DT // LANES)),
                  pl.BlockSpec((M_CONV, M_CONV_DIM), lambda b, t: (0, 0)), row(M_CONV_DIM),
                  row(LANES), row(LANES), row(W), row(W), st_spec, buf_spec],
        out_specs=[pl.BlockSpec((rb, W), at(0)), st_spec, buf_spec],
        out_shape=[jax.ShapeDtypeStruct((batch * rows_per_seq, W), BF16),
                   jax.ShapeDtypeStruct((batch, npair, 2 * M_HEAD_DIM, M_STATE), F32),
                   jax.ShapeDtypeStruct((batch, SUBLANES, M_CONV_DIM), F32)],
        scratch_shapes=[pltpu.VMEM((gseq, npair, 2 * M_HEAD_DIM, M_STATE), F32),
                        pltpu.VMEM((gseq, SUBLANES, M_CONV_DIM), F32),
                        pltpu.VMEM((gseq, cp + SUBLANES, M_CONV_DIM), F32)],
        compiler_params=_cparams("parallel", "arbitrary"),
        name="ssd",
    )(p, p, p, conv_w, conv_b, a_log, dt_bias, d_skip, norm_g, h0, buf0)


def _idx_sample_kernel(pt_ref, iq_ref, iw_ref, iknew_ref, *rest, pp, n_pages, topk, t_new):
    spg = SUBLANES // t_new
    page_refs = rest[:spg * pp]
    mask_ref = rest[spg * pp]
    key_ref = rest[spg * pp + 1]
    cut_ref = rest[spg * pp + 2]
    del pt_ref
    b = pl.program_id(0)
    c = pl.program_id(1)
    qs = SUBLANES
    nrow = mask_ref.shape[1]
    iq = iq_ref[0]
    wcol = iw_ref[0]
    r0 = pl.multiple_of(b * qs, qs)
    row = _iota((qs, LANES), 0)
    lane_q = _iota((qs, LANES), 1)

    def scores(keys_t):
        lg = _dot(iq, keys_t)
        sc = (jnp.maximum(lg, 0.0) * wcol).reshape(IDX_HEADS, qs, LANES).sum(axis=0)
        return sc + 0.0

    def own_rows(per_seq):
        out = per_seq[0]
        for g in range(1, spg):
            out = jnp.where(row >= g * t_new, per_seq[g], out)
        return out

    for j in range(pp):
        sc = own_rows([scores(page_refs[g * pp + j][0, 0].astype(BF16)) for g in range(spg)])
        key_ref[c * pp + j, pl.ds(r0, qs), :] = _to_key(sc)

    @pl.when(c == pl.num_programs(1) - 1)
    def _():
        sc = own_rows([scores(iknew_ref[g]) for g in range(spg)])
        sm = jnp.where(lane_q <= row % t_new, sc, MASK_VALUE)
        key_ref[n_pages, pl.ds(r0, qs), :] = jnp.where(lane_q < t_new, _to_key(sm), INT_MIN)

    @pl.when((c == pl.num_programs(1) - 1) & (b == pl.num_programs(0) - 1))
    def _():
        n_all = n_pages + 1
        lane = _iota((nrow, LANES), 1)

        def count(pred):
            def body(pg, acc):
                return acc + jnp.where(pred(key_ref[pg], pg), 1, 0)
            acc = lax.fori_loop(0, n_all, body, jnp.zeros((nrow, LANES), I32))
            return jnp.sum(acc, axis=1, keepdims=True)

        def count_ge(cand):
            cb = jnp.broadcast_to(cand, (nrow, LANES))
            return count(lambda blk, pg: blk >= cb)

        thr = jnp.broadcast_to(_kth_largest_key(count_ge, topk, (nrow, 1)), (nrow, LANES))
        n_gt = count(lambda blk, pg: blk > thr)
        n_eq = count(lambda blk, pg: blk == thr)
        need = topk - n_gt
        cut_ref[...] = jnp.full((nrow, LANES), 2 ** 30, I32)

        @pl.when(jnp.max(n_eq - need) > 0)
        def _():
            def body(t, ans):
                cand = ans + jnp.left_shift(jnp.int32(1), 15 - t)
                cb = jnp.broadcast_to(cand - 1, (nrow, LANES))
                upto = count(lambda blk, pg: (blk == thr) & ((pg * LANES + lane) <= cb))
                return jnp.where(upto < need, cand, ans)
            last = lax.fori_loop(0, 16, body, jnp.zeros((nrow, 1), I32))
            cut_ref[...] = jnp.broadcast_to(last, (nrow, LANES))

        last = cut_ref[...]
        lo = jnp.maximum(thr, VALID_KEY)
        tie_ok = thr > VALID_KEY

        def write(pg, carry):
            blk = key_ref[pg]
            sel = (blk > lo) | ((blk == thr) & tie_ok & ((pg * LANES + lane) <= last))
            mask_ref[pg] = jnp.where(sel, 0.0, MASK_VALUE)
            return carry

        lax.fori_loop(0, n_all, write, 0)


def _idx_sample(page_table, iq_rows, iw_col, ik_new_t, cache_kidx_t, layer, pp, t_new):
    nseq, n_pages = page_table.shape
    topk = min(TOPK_MAX, (n_pages * PAGE_SIZE + t_new) // 4)
    qs = SUBLANES
    spg = qs // t_new
    assert qs % t_new == 0 and nseq % spg == 0, (nseq, t_new)
    nrow = nseq * t_new

    def page_spec(g, j):
        return pl.BlockSpec((1, 1, IDX_DIM, PAGE_SIZE), lambda b, c, pt: (layer, pt[b * spg + g, c * pp + j], 0, 0))

    grid_spec = pltpu.PrefetchScalarGridSpec(
        num_scalar_prefetch=1,
        grid=(nseq // spg, n_pages // pp),
        in_specs=[pl.BlockSpec((1, qs * IDX_HEADS, IDX_DIM), lambda b, c, pt: (b, 0, 0)),
                  pl.BlockSpec((1, qs * IDX_HEADS, 1), lambda b, c, pt: (b, 0, 0)),
                  pl.BlockSpec((spg, IDX_DIM, PAGE_SIZE), lambda b, c, pt: (b, 0, 0))]
                 + [page_spec(g, j) for g in range(spg) for j in range(pp)],
        out_specs=pl.BlockSpec((n_pages + 1, nrow, LANES), lambda b, c, pt: (0, 0, 0)),
        scratch_shapes=[pltpu.VMEM((n_pages + 1, nrow, LANES), I32),
                        pltpu.VMEM((nrow, LANES), I32)],
    )
    return pl.pallas_call(
        functools.partial(_idx_sample_kernel, pp=pp, n_pages=n_pages, topk=topk, t_new=t_new),
        grid_spec=grid_spec,
        out_shape=jax.ShapeDtypeStruct((n_pages + 1, nrow, LANES), F32),
        compiler_params=_cparams("arbitrary", "arbitrary"),
        name="idx_sample",
    )(page_table, iq_rows, iw_col, ik_new_t, *([cache_kidx_t] * (spg * pp)))


def _attn_sample_kernel(pt_ref, q_ref, mask_ref, masknew_ref, knew_ref, vnew_ref, *rest, pp, t_new):
    k_refs = rest[:pp]
    v_refs = rest[pp:2 * pp]
    o_ref = rest[2 * pp]
    m_ref, l_ref, acc_ref = rest[2 * pp + 1:]
    del pt_ref
    c = pl.program_id(1)
    nrow = t_new * A_HEADS
    q = q_ref[0]

    @pl.when(c == 0)
    def _():
        m_ref[...] = jnp.full(m_ref.shape, MASK_VALUE, F32)
        l_ref[...] = jnp.zeros(l_ref.shape, F32)
        acc_ref[...] = jnp.zeros(acc_ref.shape, F32)

    def update(k_pages, v_pages, masks):
        s = []
        for kt, mk in zip(k_pages, masks):
            mrows = jnp.concatenate([jnp.broadcast_to(mk[t:t + 1, :], (A_HEADS, LANES)) for t in range(t_new)], axis=0)
            s.append(_dot(q, kt) + mrows)
        m_blk = s[0]
        for t in s[1:]:
            m_blk = jnp.maximum(m_blk, t)
        m_old = m_ref[...]
        m_new = jnp.maximum(m_old, jnp.max(m_blk, axis=1, keepdims=True))
        alpha = jnp.exp2(m_old - m_new)
        l_sum = jnp.zeros((nrow, LANES), F32)
        acc = alpha * acc_ref[...]
        for t, vt in zip(s, v_pages):
            e = jnp.exp2(t - m_new)
            l_sum = l_sum + e
            acc = acc + _nt(e.astype(BF16), vt)
        l_ref[...] = alpha * l_ref[...] + jnp.sum(l_sum, axis=1, keepdims=True)
        acc_ref[...] = acc
        m_ref[...] = m_new

    update([r[0, 0].astype(BF16) for r in k_refs], [r[0, 0].astype(BF16) for r in v_refs],
           [mask_ref[j] for j in range(pp)])

    @pl.when(c == pl.num_programs(1) - 1)
    def _():
        update([knew_ref[0]], [vnew_ref[0]], [masknew_ref[...]])
        o = acc_ref[...] / l_ref[...]
        grp = (_iota((nrow, 1), 0) % A_HEADS) // (A_HEADS // A_KV_HEADS)
        out = jnp.zeros((nrow, A_HEAD_DIM), F32)
        for g in range(A_KV_HEADS):
            out = out + jnp.where(grp == g, o[:, g * A_HEAD_DIM:(g + 1) * A_HEAD_DIM], 0.0)
        o_ref[0] = out


def _attn_sample(page_table, q_rows, mask, k_new_t, v_new_t, cache_k_t, cache_v_t, layer, pp, t_new):
    nseq, n_pages = page_table.shape
    kvw = A_KV_HEADS * A_HEAD_DIM
    nrow = t_new * A_HEADS
    mask4 = mask.reshape(n_pages + 1, nseq, t_new, LANES)

    def page_spec(j):
        return pl.BlockSpec((1, 1, kvw, PAGE_SIZE), lambda b, c, pt: (layer, pt[b, c * pp + j], 0, 0))

    grid_spec = pltpu.PrefetchScalarGridSpec(
        num_scalar_prefetch=1,
        grid=(nseq, n_pages // pp),
        in_specs=[pl.BlockSpec((1, nrow, kvw), lambda b, c, pt: (b, 0, 0)),
                  pl.BlockSpec((pp, None, t_new, LANES), lambda b, c, pt: (c, b, 0, 0)),
                  pl.BlockSpec((None, None, t_new, LANES), lambda b, c, pt: (n_pages, b, 0, 0)),
                  pl.BlockSpec((1, kvw, PAGE_SIZE), lambda b, c, pt: (b, 0, 0)),
                  pl.BlockSpec((1, kvw, PAGE_SIZE), lambda b, c, pt: (b, 0, 0))]
                 + [page_spec(j) for j in range(pp)] * 2,
        out_specs=pl.BlockSpec((1, nrow, A_HEAD_DIM), lambda b, c, pt: (b, 0, 0)),
        scratch_shapes=[pltpu.VMEM((nrow, 1), F32), pltpu.VMEM((nrow, 1), F32), pltpu.VMEM((nrow, kvw), F32)],
    )
    return pl.pallas_call(
        functools.partial(_attn_sample_kernel, pp=pp, t_new=t_new),
        grid_spec=grid_spec,
        out_shape=jax.ShapeDtypeStruct((nseq, nrow, A_HEAD_DIM), F32),
        compiler_params=_cparams("parallel", "arbitrary"),
        name="attn_sample",
    )(page_table, q_rows, mask4, mask4, k_new_t, v_new_t, *([cache_k_t] * pp), *([cache_v_t] * pp))


def _sprep_kernel(aq_ref, ak_ref, iq_ref, ikw_ref, qg_ref, kg_ref, qn_ref, kn_ref, iqs_ref, iw_ref):
    for c, t in enumerate(_head_rmsnorm(aq_ref[...], qg_ref[...])):
        qn_ref[:, c * LANES:(c + 1) * LANES] = (t * (A_HEAD_DIM ** -0.5 * LOG2_E)).astype(BF16)
    for c, t in enumerate(_head_rmsnorm(ak_ref[...], kg_ref[...])):
        kn_ref[:, c * LANES:(c + 1) * LANES] = t
    iqs_ref[...] = (iq_ref[...] * (IDX_DIM ** -0.5)).astype(BF16)
    iw_ref[...] = ikw_ref[...] * (IDX_HEADS ** -0.5)


def _sprep(p, q_gain, k_gain):
    n = p.shape[0]
    kvw = A_KV_HEADS * A_HEAD_DIM
    return pl.pallas_call(
        _sprep_kernel,
        grid=(1,),
        in_specs=[pl.BlockSpec((n, W), lambda i: (0, C_AQ // W)),
                  pl.BlockSpec((n, kvw), lambda i: (0, C_AK // kvw)),
                  pl.BlockSpec((n, W), lambda i: (0, C_IQ // W)),
                  pl.BlockSpec((n, LANES), lambda i: (0, C_IKW // LANES)),
                  pl.BlockSpec((1, LANES), lambda i: (0, 0)),
                  pl.BlockSpec((1, LANES), lambda i: (0, 0))],
        out_specs=[pl.BlockSpec((n, W), lambda i: (0, 0)), pl.BlockSpec((n, kvw), lambda i: (0, 0)),
                   pl.BlockSpec((n, W), lambda i: (0, 0)), pl.BlockSpec((n, LANES), lambda i: (0, 0))],
        out_shape=[jax.ShapeDtypeStruct((n, W), BF16), jax.ShapeDtypeStruct((n, kvw), F32),
                   jax.ShapeDtypeStruct((n, W), BF16), jax.ShapeDtypeStruct((n, LANES), F32)],
        compiler_params=_cparams("arbitrary"),
        name="sprep",
    )(p, p, p, p, q_gain, k_gain)


def _merge_kernel(oa_ref, ob_ref, oc_ref, od_ref, gates_ref, x_ref, wb_ref, wo_ref, n2_ref, x1_ref, h2_ref):
    merged = None
    for b, o_ref in enumerate((oa_ref, ob_ref, oc_ref, od_ref)):
        up = _dot(o_ref[...], wb_ref[b])
        t = _sigmoid(gates_ref[:, b * D_MODEL:(b + 1) * D_MODEL].astype(F32)) * up
        merged = t if merged is None else merged + t
    x1 = x_ref[...] + _dot(merged.astype(BF16), wo_ref[...])
    x1_ref[...] = x1
    ms = jnp.mean(x1 * x1, axis=-1, keepdims=True)
    h2_ref[...] = (x1 * lax.rsqrt(ms + RMS_EPS) * n2_ref[...]).astype(BF16)


def _merge(oa, ob, oc, od, p, x, w_branch, w_out, norm2, layer, tm):
    n = x.shape[0]
    row = lambda wdt: pl.BlockSpec((tm, wdt), lambda i: (i, 0))
    return pl.pallas_call(
        _merge_kernel,
        grid=(n // tm,),
        in_specs=[row(W), row(W), row(W), row(W),
                  pl.BlockSpec((tm, N_BRANCH * D_MODEL), lambda i: (i, C_GATES // (N_BRANCH * D_MODEL))),
                  row(D_MODEL),
                  pl.BlockSpec((None, N_BRANCH, W, D_MODEL), lambda i: (layer, 0, 0, 0)),
                  pl.BlockSpec((None, D_MODEL, D_MODEL), lambda i: (layer, 0, 0)),
                  pl.BlockSpec((1, D_MODEL), lambda i: (0, 0))],
        out_specs=[row(D_MODEL), row(D_MODEL)],
        out_shape=[jax.ShapeDtypeStruct((n, D_MODEL), F32), jax.ShapeDtypeStruct((n, D_MODEL), BF16)],
        compiler_params=_cparams("parallel"),
        name="merge",
    )(oa, ob, oc, od, p, x, w_branch, w_out, norm2)


def _ffn_kernel(h_ref, x_ref, wg_ref, wu_ref, wd_ref, o_ref, acc_ref):
    j = pl.program_id(1)

    @pl.when(j == 0)
    def _():
        acc_ref[...] = x_ref[...]

    h = h_ref[...]
    act = _silu(_dot(h, wg_ref[...])) * _dot(h, wu_ref[...])
    acc_ref[...] += _dot(act.astype(BF16), wd_ref[...])

    @pl.when(j == pl.num_programs(1) - 1)
    def _():
        o_ref[...] = acc_ref[...]


def _ffn(h2, x1, w_gu, w_down, layer, tm, fc):
    n = x1.shape[0]
    nf = D_FF // fc
    return pl.pallas_call(
        _ffn_kernel,
        grid=(n // tm, nf),
        in_specs=[pl.BlockSpec((tm, D_MODEL), lambda i, j: (i, 0)),
                  pl.BlockSpec((tm, D_MODEL), lambda i, j: (i, 0)),
                  pl.BlockSpec((None, D_MODEL, fc), lambda i, j: (layer, 0, j)),
                  pl.BlockSpec((None, D_MODEL, fc), lambda i, j: (layer, 0, nf + j)),
                  pl.BlockSpec((None, fc, D_MODEL), lambda i, j: (layer, j, 0))],
        out_specs=pl.BlockSpec((tm, D_MODEL), lambda i, j: (i, 0)),
        out_shape=jax.ShapeDtypeStruct((n, D_MODEL), F32),
        scratch_shapes=[pltpu.VMEM((tm, D_MODEL), F32)],
        compiler_params=_cparams("parallel", "arbitrary"),
        name="ffn",
    )(h2, x1, w_gu, w_gu, w_down)


def _pack_w_in(w_in):
    (a_q, a_k, a_v, a_iq, a_ik, a_iw, r_q, r_k, r_v, r_g, c_q, c_f, c_i, c_g,
     m_z, m_xbc, m_dt, gates) = jnp.split(w_in, SPLIT_POINTS, axis=-1)

    def pad_to_lanes(a):
        return jnp.pad(a, ((0, 0), (0, 0), (0, LANES - a.shape[-1])))

    ikw = pad_to_lanes(jnp.concatenate([a_ik, a_iw], axis=-1))
    packed = jnp.concatenate([gates, m_xbc, a_q, a_iq, r_q, r_k, r_v, r_g, c_q, c_f, c_i, c_g, m_z,
                              a_k, a_v, ikw, pad_to_lanes(m_dt)], axis=-1)
    return packed.astype(BF16)


def _rotary_tables(pos):
    half = R_DK // 2
    inv_freq = ROPE_BASE ** (-jnp.arange(half, dtype=F32) / half)
    ang = pos.astype(F32)[:, None] * inv_freq[None, :]
    c, s = jnp.cos(ang), jnp.sin(ang)
    return jnp.concatenate([c, c], axis=-1), jnp.concatenate([-s, s], axis=-1)


def _lower_bounds(lb_param):
    pr = jax.nn.softmax(lb_param.astype(F32), axis=0)
    return jnp.cumsum(pr, axis=0) - pr[0]


def _pad_lanes_row(v):
    return jnp.pad(v, (0, LANES - v.shape[0]))[None, :]


def kernel(x_prompt, x_sample, cache_k, cache_v, cache_kidx, state_ret, state_hgrn, state_ssm, state_conv,
           page_table, norm1_g, w_in, q_norm_g, k_norm_g, lb_param, hgrn_norm_g, A_log, dt_bias, D_skip,
           conv_w, conv_b, ssm_norm_g, w_branch, w_out, norm2_g, w_gu, w_down):
    bp, tp, _ = x_prompt.shape
    bs, ts, _ = x_sample.shape
    depth = w_in.shape[0]
    n_pool = cache_k.shape[1]
    n_pages = page_table.shape[1]
    kvw = A_KV_HEADS * A_HEAD_DIM
    tsp = SUBLANES
    npair = M_HEADS // 2
    gs = math.gcd(bs, 8)

    w_in_p = _pack_w_in(w_in)
    w_branch_b = w_branch.astype(BF16)
    w_out_b = w_out.astype(BF16)
    w_gu_b = w_gu.astype(BF16)
    w_down_b = w_down.astype(BF16)
    lbs = _lower_bounds(lb_param)
    cos_p, sin_p = _rotary_tables(jnp.arange(tp, dtype=jnp.int32))
    pos_s = n_pages * PAGE_SIZE + jnp.arange(tsp, dtype=jnp.int32)
    cos_s, sin_s = _rotary_tables(pos_s)
    ck_t = jnp.transpose(cache_k, (0, 1, 3, 4, 2)).reshape(depth, n_pool, kvw, PAGE_SIZE)
    cv_t = jnp.transpose(cache_v, (0, 1, 3, 4, 2)).reshape(depth, n_pool, kvw, PAGE_SIZE)
    ckidx_t = jnp.transpose(cache_kidx, (0, 1, 3, 2))
    head_group = (jnp.arange(A_HEADS) // (A_HEADS // A_KV_HEADS))[:, None] == jnp.arange(A_KV_HEADS)[None, :]

    zero_ret = jnp.zeros((bp, R_HEADS, R_DK, R_DV), F32)
    zero_hgrn = jnp.zeros((bp, H_HEADS, H_DK, H_DV), F32)
    zero_ssm = jnp.zeros((bp, npair, 2 * M_HEAD_DIM, M_STATE), F32)
    zero_buf = jnp.zeros((bp, SUBLANES, M_CONV_DIM), F32)

    xp = x_prompt.reshape(bp * tp, D_MODEL)
    xs = x_sample.reshape(bs * ts, D_MODEL)
    outs_p = [[] for _ in range(7)]
    outs_s = [[] for _ in range(7)]
    for l in range(depth):
        n1 = norm1_g[l][None, :]
        n2 = norm2_g[l][None, :]
        qg = jnp.tile(q_norm_g[l], 2)[None, :]
        kg = jnp.tile(k_norm_g[l], 2)[None, :]
        lb = lbs[l][None, :]
        hng = hgrn_norm_g[l][None, :]
        alog = _pad_lanes_row(A_log[l])
        dtb = _pad_lanes_row(dt_bias[l])
        dsk = jnp.repeat(D_skip[l], M_HEAD_DIM)[None, :]
        sng = ssm_norm_g[l][None, :]
        cw = conv_w[l]
        cb = conv_b[l][None, :]

        p = _inproj(xp, n1, w_in_p, l, 1024, 1280, BF16)
        kn, vv, ik, knb, vt, ik2 = _kprep(p, kg, 512)
        oa = _attn_prompt(p, knb, vt, ik2, qg, bp, tp, 256, 256)
        ob, ret_new = _retention(p, cos_p, sin_p, zero_ret, bp, tp, 512, R_CHUNK, R_CHUNK)
        oc, hgrn_new = _hgrn2(p, lb, hng, zero_hgrn, bp, tp, 512, H_CHUNK, H_CHUNK, 16)
        od, ssm_new, buf_new = _ssd(p, cw, cb, alog, dtb, dsk, sng, zero_ssm, zero_buf, bp, tp, 512, M_CHUNK, M_CHUNK)
        x1, h2 = _merge(oa, ob, oc, od, p, xp, w_branch_b, w_out_b, n2, l, 512)
        xp = _ffn(h2, x1, w_gu_b, w_down_b, l, 512, D_FF // 2)
        for i, a in enumerate((kn.reshape(bp, tp, A_KV_HEADS, A_HEAD_DIM), vv.reshape(bp, tp, A_KV_HEADS, A_HEAD_DIM),
                               ik.reshape(bp, tp, IDX_DIM), ret_new, hgrn_new,
                               ssm_new.reshape(bp, M_HEADS, M_HEAD_DIM, M_STATE),
                               buf_new[:, SUBLANES - (M_CONV - 1):, :])):
            outs_p[i].append(a)

        ps = _inproj(xs, n1, w_in_p, l, bs * ts, 1280, F32)
        qn_s, kn_s, iq_s, iw_s = _sprep(ps, qg, kg)
        v_s = ps[:, C_AV:C_AV + kvw]
        ik_s = ps[:, C_IKW:C_IKW + IDX_DIM]
        pad_q = lambda a: jnp.pad(a, ((0, 0), (0, tsp - ts)) + ((0, 0),) * (a.ndim - 2))
        new_keys_t = lambda a: jnp.swapaxes(jnp.pad(a.reshape(bs, ts, -1), ((0, 0), (0, PAGE_SIZE - ts), (0, 0))), 1, 2)
        spg = tsp // ts
        head_major = lambda a: jnp.swapaxes(a.reshape(bs // spg, tsp, IDX_HEADS, -1), 1, 2).reshape(bs // spg, IDX_HEADS * tsp, -1)
        iq_rows = head_major(iq_s)
        iw_col = head_major(iw_s[:, IDX_DIM:IDX_DIM + IDX_HEADS])
        mask = _idx_sample(page_table, iq_rows, iw_col, new_keys_t(ik_s).astype(BF16), ckidx_t, l, min(16, n_pages), ts)
        q4 = qn_s.reshape(bs, ts, A_HEADS, 1, A_HEAD_DIM)
        q_rows = jnp.where(head_group[None, None, :, :, None], q4, jnp.zeros_like(q4)).reshape(bs, ts * A_HEADS, kvw)
        oa_s = _attn_sample(page_table, q_rows, mask, new_keys_t(kn_s).astype(BF16), new_keys_t(v_s).astype(BF16),
                            ck_t, cv_t, l, min(32, n_pages), ts)
        oa_s = oa_s.reshape(bs * ts, W).astype(BF16)
        psp = pad_q(ps.reshape(bs, ts, NP)).reshape(bs * tsp, NP)
        ssm0 = state_ssm[l].reshape(bs, npair, 2 * M_HEAD_DIM, M_STATE)
        buf0 = jnp.pad(state_conv[l], ((0, 0), (SUBLANES - (M_CONV - 1), 0), (0, 0)))
        ob_s, ret_s = _retention(psp, cos_s, sin_s, state_ret[l], bs, tsp, tsp, tsp, ts, gs)
        oc_s, hgrn_s = _hgrn2(psp, lb, hng, state_hgrn[l], bs, tsp, tsp, tsp, ts, tsp, gs)
        od_s, ssm_s, buf_s = _ssd(psp, cw, cb, alog, dtb, dsk, sng, ssm0, buf0, bs, tsp, tsp, tsp, ts, gs)
        unpad = lambda a: a.reshape(bs, tsp, W)[:, :ts].reshape(bs * ts, W)
        x1s, h2s = _merge(oa_s, unpad(ob_s), unpad(oc_s), unpad(od_s), ps, xs, w_branch_b, w_out_b, n2, l, bs * ts)
        xs = _ffn(h2s, x1s, w_gu_b, w_down_b, l, bs * ts, D_FF // 2)
        for i, a in enumerate((kn_s.reshape(bs, ts, A_KV_HEADS, A_HEAD_DIM), v_s.reshape(bs, ts, A_KV_HEADS, A_HEAD_DIM),
                               ik_s.reshape(bs, ts, IDX_DIM), ret_s, hgrn_s,
                               ssm_s.reshape(bs, M_HEADS, M_HEAD_DIM, M_STATE),
                               buf_s[:, SUBLANES - (M_CONV - 1):, :])):
            outs_s[i].append(a)

    dtypes = (cache_k.dtype, cache_v.dtype, cache_kidx.dtype, state_ret.dtype,
              state_hgrn.dtype, state_ssm.dtype, state_conv.dtype)
    res_p = [jnp.stack(a).astype(d) for a, d in zip(outs_p, dtypes)]
    res_s = [jnp.stack(a).astype(d) for a, d in zip(outs_s, dtypes)]
    return (xp.reshape(bp, tp, D_MODEL), xs.reshape(bs, ts, D_MODEL), *res_p, *res_s)
```

```python
import functools
import math
import struct

import jax
import jax.numpy as jnp
from jax import lax
from jax.experimental import pallas as pl
from jax.experimental.pallas import tpu as pltpu

F32 = jnp.float32
BF16 = jnp.bfloat16
I32 = jnp.int32

D_MODEL = 1024
PAGE_SIZE = 128
BRANCH_WIDTH = D_MODEL // 2
N_BRANCH = 4
A_HEAD_DIM = 64
A_HEADS = BRANCH_WIDTH // A_HEAD_DIM
A_KV_HEADS = A_HEADS // 2
IDX_HEADS = 8
IDX_DIM = 64
TOPK_MAX = 256
Q_BLOCK = 128
R_HEADS = 4
R_DK = BRANCH_WIDTH // R_HEADS
R_DV = BRANCH_WIDTH // R_HEADS
R_CHUNK = 128
ROPE_BASE = 10000.0
H_HEADS = 4
H_DK = 128
H_DV = BRANCH_WIDTH // H_HEADS
H_CHUNK = 64
MIN_FORGET = 1e-30
SAFE_EXP = 60.0
M_HEAD_DIM = 64
M_HEADS = BRANCH_WIDTH // M_HEAD_DIM
M_INNER = M_HEADS * M_HEAD_DIM
M_GROUPS = 2
M_STATE = 128
M_CONV = 4
M_CHUNK = 128
M_CONV_DIM = M_INNER + 2 * M_GROUPS * M_STATE
D_FF = -(-8 * D_MODEL // (3 * 256)) * 256
RMS_EPS = 1e-6
GN_EPS = 1e-6
MASK_VALUE = -1e30

SPLIT_SIZES = (
    A_HEADS * A_HEAD_DIM, A_KV_HEADS * A_HEAD_DIM, A_KV_HEADS * A_HEAD_DIM,
    IDX_HEADS * IDX_DIM, IDX_DIM, IDX_HEADS,
    R_HEADS * R_DK, R_HEADS * R_DK, R_HEADS * R_DV, R_HEADS * R_DV,
    H_HEADS * H_DK, H_HEADS * H_DK, H_HEADS * H_DV, H_HEADS * H_DV,
    M_INNER, M_CONV_DIM, M_HEADS,
    N_BRANCH * D_MODEL,
)
SPLIT_POINTS = tuple(sum(SPLIT_SIZES[:i + 1]) for i in range(len(SPLIT_SIZES) - 1))

LANES = 128
SUBLANES = 8
VMEM_LIMIT = 56 * 1024 * 1024

C_GATES = 0
C_XBC = 4096
C_AQ = 5120
C_IQ = 5632
C_RQ = 6144
C_RK = 6656
C_RV = 7168
C_RG = 7680
C_CQ = 8192
C_CF = 8704
C_CI = 9216
C_CG = 9728
C_MZ = 10240
C_AK = 10752
C_AV = 11008
C_IKW = 11264
C_DT = 11392
NP = 11520
W = BRANCH_WIDTH


def _float_key(x):
    b = struct.unpack("<i", struct.pack("<f", x))[0]
    return b if b >= 0 else b ^ 0x7FFFFFFF


VALID_KEY = _float_key(0.5 * MASK_VALUE)
INT_MIN = -2 ** 31
LOG2_E = math.log2(math.e)


def _cparams(*sem):
    return pltpu.CompilerParams(dimension_semantics=sem, vmem_limit_bytes=VMEM_LIMIT)


def _nt(a, b):
    return lax.dot_general(a, b, (((1,), (1,)), ((), ())), preferred_element_type=F32)


def _tn(a, b):
    return lax.dot_general(a, b, (((0,), (0,)), ((), ())), preferred_element_type=F32)


def _dot(a, b):
    return jnp.dot(a, b, preferred_element_type=F32)


def _dot_exact_lhs(a01, x):
    a = a01.astype(BF16)
    x0 = x.astype(BF16)
    r1 = x - x0.astype(F32)
    x1 = r1.astype(BF16)
    x2 = (r1 - x1.astype(F32)).astype(BF16)
    return _dot(a, x0) + _dot(a, x1) + _dot(a, x2)


def _dot_exact_rhs(x, b01):
    b = b01.astype(BF16)
    x0 = x.astype(BF16)
    r1 = x - x0.astype(F32)
    x1 = r1.astype(BF16)
    x2 = (r1 - x1.astype(F32)).astype(BF16)
    return _dot(x0, b) + _dot(x1, b) + _dot(x2, b)


def _sigmoid(x):
    return 1.0 / (1.0 + jnp.exp(-x))


def _silu(x):
    return x * _sigmoid(x)


def _iota(shape, dim):
    return lax.broadcasted_iota(I32, shape, dim)


def _tri_incl(n):
    return (_iota((n, n), 0) >= _iota((n, n), 1)).astype(F32)


def _inproj_kernel(x_ref, g_ref, w_ref, o_ref, h_ref):
    @pl.when(pl.program_id(1) == 0)
    def _():
        x = x_ref[...]
        ms = jnp.mean(x * x, axis=-1, keepdims=True)
        h_ref[...] = (x * lax.rsqrt(ms + RMS_EPS) * g_ref[...]).astype(BF16)

    o_ref[...] = _dot(h_ref[...], w_ref[...]).astype(o_ref.dtype)


def _inproj(x, g, w, layer, tm, tn, out_dtype):
    n = x.shape[0]
    return pl.pallas_call(
        _inproj_kernel,
        grid=(n // tm, NP // tn),
        in_specs=[pl.BlockSpec((tm, D_MODEL), lambda i, j: (i, 0)),
                  pl.BlockSpec((1, D_MODEL), lambda i, j: (0, 0)),
                  pl.BlockSpec((None, D_MODEL, tn), lambda i, j: (layer, 0, j))],
        out_specs=pl.BlockSpec((tm, tn), lambda i, j: (i, j)),
        out_shape=jax.ShapeDtypeStruct((n, NP), out_dtype),
        scratch_shapes=[pltpu.VMEM((tm, D_MODEL), BF16)],
        compiler_params=_cparams("parallel", "arbitrary"),
        name="inproj",
    )(x, g, w)


def _head_rmsnorm(x, gain_row):
    pair = (_iota((LANES, LANES), 0) // A_HEAD_DIM == _iota((LANES, LANES), 1) // A_HEAD_DIM).astype(F32)
    outs = []
    for c in range(x.shape[1] // LANES):
        xc = x[:, c * LANES:(c + 1) * LANES]
        ms = _dot_exact_rhs(xc * xc, pair) * (1.0 / A_HEAD_DIM)
        outs.append(xc * lax.rsqrt(ms + RMS_EPS) * gain_row)
    return outs


def _kth_largest_key(count_ge, k, shape, n_keys):
    c0 = count_ge(jnp.zeros(shape, I32))
    nonneg = c0 >= k
    prefix = jnp.where(nonneg, 0, INT_MIN).astype(I32)
    c_prefix = jnp.where(nonneg, c0, n_keys).astype(I32)

    def body(t, carry):
        prefix, c_prefix = carry
        cand = prefix + jnp.left_shift(jnp.int32(1), 30 - t)
        c = count_ge(cand)
        take = c >= k
        return jnp.where(take, cand, prefix), jnp.where(take, c, c_prefix)

    return lax.fori_loop(0, 31, body, (prefix, c_prefix))


def _to_key(s):
    b = lax.bitcast_convert_type(s, I32)
    return jnp.where(b >= 0, b, b ^ 0x7FFFFFFF)


def _kprep_kernel(ak_ref, av_ref, ikw_ref, kg_ref, kn_ref, v_ref, ik_ref, knb_ref, vt_ref, ik2_ref):
    kn = _head_rmsnorm(ak_ref[...].astype(F32), kg_ref[...])
    for c, t in enumerate(kn):
        kn_ref[:, c * LANES:(c + 1) * LANES] = t
        knb_ref[:, c * LANES:(c + 1) * LANES] = t.astype(BF16)
    v = av_ref[...].astype(F32)
    v_ref[...] = v
    vt_ref[...] = v.T.astype(BF16)
    ikw = ikw_ref[...].astype(F32)
    ik_ref[...] = ikw[:, :IDX_DIM]
    lane = _iota(ikw.shape, 1)
    ik2_ref[...] = jnp.where(lane < IDX_DIM, ikw, pltpu.roll(ikw, IDX_DIM, 1)).astype(BF16)


def _kprep(p, k_gain, tm):
    n = p.shape[0]
    kvw = A_KV_HEADS * A_HEAD_DIM
    row = lambda wdt: pl.BlockSpec((tm, wdt), lambda i: (i, 0))
    return pl.pallas_call(
        _kprep_kernel,
        grid=(n // tm,),
        in_specs=[pl.BlockSpec((tm, kvw), lambda i: (i, C_AK // kvw)),
                  pl.BlockSpec((tm, kvw), lambda i: (i, C_AV // kvw)),
                  pl.BlockSpec((tm, LANES), lambda i: (i, C_IKW // LANES)),
                  pl.BlockSpec((1, LANES), lambda i: (0, 0))],
        out_specs=[row(kvw), row(kvw), row(IDX_DIM), row(kvw), pl.BlockSpec((kvw, tm), lambda i: (0, i)), row(LANES)],
        out_shape=[jax.ShapeDtypeStruct((n, kvw), F32),
                   jax.ShapeDtypeStruct((n, kvw), F32),
                   jax.ShapeDtypeStruct((n, IDX_DIM), F32),
                   jax.ShapeDtypeStruct((n, kvw), BF16),
                   jax.ShapeDtypeStruct((kvw, n), BF16),
                   jax.ShapeDtypeStruct((n, LANES), BF16)],
        compiler_params=_cparams("parallel"),
        name="kprep",
    )(p, p, p, k_gain)


def _attn_prompt_kernel(aq_ref, iq_ref, ikw_ref, qg_ref, knb_ref, vt_ref, ik2_ref, o_ref,
                        key_ref, msk_ref, s_ref, qt_ref, iqt_ref, acc_ref, cut_ref, *, topk, kb, qb):
    i = pl.program_id(1)
    nblk = (i * qb + qb + kb - 1) // kb
    row = _iota((kb, qb), 0)
    lane = _iota((kb, qb), 1)
    lane_q = _iota((qb, LANES), 1)
    own_half = [lane_q < A_HEAD_DIM, lane_q >= A_HEAD_DIM]
    n_pairs = A_KV_HEADS // 2
    hpp = A_HEADS // n_pairs
    nsub = kb // SUBLANES

    qn = _head_rmsnorm(aq_ref[...].astype(F32), qg_ref[...])
    for h in range(A_HEADS):
        src, dst = h % 2, (h // 2) % 2
        t = qn[h // 2] * (A_HEAD_DIM ** -0.5 * LOG2_E)
        if src != dst:
            t = pltpu.roll(t, A_HEAD_DIM, 1)
        t = jnp.where(own_half[dst], t, 0.0)
        qt_ref[h // hpp, :, (h % hpp) * qb:(h % hpp + 1) * qb] = t.T.astype(BF16)
    for h in range(IDX_HEADS):
        t = iq_ref[:, (h // 2) * LANES:(h // 2 + 1) * LANES].astype(F32) * (IDX_DIM ** -0.5)
        iqt_ref[:, h * qb:(h + 1) * qb] = jnp.where(own_half[h % 2], t, 0.0).T.astype(BF16)
    iw_t = ikw_ref[...].astype(F32).T[IDX_DIM:IDX_DIM + IDX_HEADS, :] * (IDX_HEADS ** -0.5)

    def score_blk(j, carry):
        ks = pl.multiple_of(j * kb, kb)
        lg = _dot(ik2_ref[pl.ds(ks, kb), :], iqt_ref[...])
        acc = jnp.zeros((kb, qb), F32)
        for h in range(IDX_HEADS):
            acc = acc + jnp.maximum(lg[:, h * qb:(h + 1) * qb], 0.0) * iw_t[h:h + 1, :]
        adm = (ks + row) <= (i * qb + lane)
        key_ref[j] = _to_key(jnp.where(adm, acc, MASK_VALUE))
        return carry

    lax.fori_loop(0, nblk, score_blk, 0)

    def count(pred):
        def body(j, c):
            return c + jnp.sum(jnp.where(pred(key_ref[j], j), 1, 0).reshape(nsub, SUBLANES, qb), axis=0)
        c = lax.fori_loop(0, nblk, body, jnp.zeros((SUBLANES, qb), I32))
        return jnp.sum(c, axis=0, keepdims=True)

    cut_ref[0:1, :] = jnp.full((1, qb), INT_MIN, I32)
    cut_ref[1:2, :] = jnp.full((1, qb), 2 ** 30, I32)

    @pl.when(i * qb + qb > topk)
    def _():
        thr, n_ge = _kth_largest_key(lambda cand: count(lambda blk, j: blk >= cand), topk, (1, qb), nblk * kb)
        cut_ref[0:1, :] = thr

        @pl.when(jnp.max(n_ge) > topk)
        def _():
            need = topk - count(lambda blk, j: blk > thr)

            def body(t, ans):
                cand = ans + jnp.left_shift(jnp.int32(1), 15 - t)
                upto = count(lambda blk, j: (blk == thr) & ((j * kb + row) <= cand - 1))
                return jnp.where(upto < need, cand, ans)

            cut_ref[1:2, :] = lax.fori_loop(0, 16, body, jnp.zeros((1, qb), I32))

    thr = cut_ref[0:1, :]
    last = cut_ref[1:2, :]
    lo = jnp.maximum(thr, VALID_KEY)
    tie_ok = thr > VALID_KEY

    def mask_blk(j, carry):
        blk = key_ref[j]
        sel = (blk > lo) | ((blk == thr) & tie_ok & ((j * kb + row) <= last))
        msk_ref[j] = jnp.where(sel, 0.0, MASK_VALUE)
        return carry

    lax.fori_loop(0, nblk, mask_blk, 0)

    def fold(t, op):
        return op(t.reshape(nsub, SUBLANES, qb), axis=0)

    def logits_blk(j, m8):
        ks = pl.multiple_of(j * kb, kb)
        mk = msk_ref[j]
        out = []
        for pr in range(n_pairs):
            s4 = _dot(knb_ref[pl.ds(ks, kb), pr * LANES:(pr + 1) * LANES], qt_ref[pr])
            for hh in range(hpp):
                h = pr * hpp + hh
                s = s4[:, hh * qb:(hh + 1) * qb] + mk
                s_ref[j, h] = s
                out.append(jnp.maximum(m8[h], fold(s, jnp.max)))
        return tuple(out)

    m8 = lax.fori_loop(0, nblk, logits_blk, tuple(jnp.full((SUBLANES, qb), MASK_VALUE, F32) for _ in range(A_HEADS)))
    m_row = [jnp.max(t, axis=0, keepdims=True) for t in m8]

    acc_ref[...] = jnp.zeros(acc_ref.shape, F32)

    def values_blk(j, l8):
        ks = pl.multiple_of(j * kb, kb)
        out = []
        for h in range(A_HEADS):
            g = h // (A_HEADS // A_KV_HEADS)
            e = jnp.exp2(s_ref[j, h] - m_row[h])
            out.append(l8[h] + fold(e, jnp.sum))
            vt = vt_ref[g * A_HEAD_DIM:(g + 1) * A_HEAD_DIM, pl.ds(ks, kb)]
            acc_ref[h] = acc_ref[h] + _dot(vt, e.astype(BF16))
        return tuple(out)

    l8 = lax.fori_loop(0, nblk, values_blk, tuple(jnp.zeros((SUBLANES, qb), F32) for _ in range(A_HEADS)))

    for h in range(A_HEADS):
        l_row = jnp.sum(l8[h], axis=0, keepdims=True)
        o_ref[:, h * A_HEAD_DIM:(h + 1) * A_HEAD_DIM] = (acc_ref[h] / l_row).T.astype(o_ref.dtype)


def _attn_prompt(p, knb, vt, ik2, q_gain, batch, seq, kb, qb):
    nqb = seq // qb
    nkb = seq // kb
    kvw = A_KV_HEADS * A_HEAD_DIM
    topk = min(TOPK_MAX, seq // 4)
    n_pairs = A_KV_HEADS // 2
    hpp = A_HEADS // n_pairs
    return pl.pallas_call(
        functools.partial(_attn_prompt_kernel, topk=topk, kb=kb, qb=qb),
        grid=(batch, nqb),
        in_specs=[pl.BlockSpec((qb, W), lambda b, i: (b * nqb + i, C_AQ // W)),
                  pl.BlockSpec((qb, W), lambda b, i: (b * nqb + i, C_IQ // W)),
                  pl.BlockSpec((qb, LANES), lambda b, i: (b * nqb + i, C_IKW // LANES)),
                  pl.BlockSpec((1, LANES), lambda b, i: (0, 0)),
                  pl.BlockSpec((seq, kvw), lambda b, i: (b, 0)),
                  pl.BlockSpec((kvw, seq), lambda b, i: (0, b)),
                  pl.BlockSpec((seq, LANES), lambda b, i: (b, 0))],
        out_specs=pl.BlockSpec((qb, W), lambda b, i: (b * nqb + i, 0)),
        out_shape=jax.ShapeDtypeStruct((batch * seq, W), BF16),
        scratch_shapes=[pltpu.VMEM((nkb, kb, qb), I32),
                        pltpu.VMEM((nkb, kb, qb), F32),
                        pltpu.VMEM((nkb, A_HEADS, kb, qb), F32),
                        pltpu.VMEM((n_pairs, LANES, hpp * qb), BF16),
                        pltpu.VMEM((LANES, IDX_HEADS * qb), BF16),
                        pltpu.VMEM((A_HEADS, A_HEAD_DIM, qb), F32),
                        pltpu.VMEM((SUBLANES, qb), I32)],
        compiler_params=_cparams("parallel", "arbitrary"),
        name="attn_prompt",
    )(p, p, p, q_gain, knb, vt, ik2)


def _ret_body(q_ref, k_ref, v_ref, g_ref, cos_ref, sin_ref, s0_ref, o_ref, s_out_ref, st_ref, cp, c_true):
    tb = pl.program_id(1)

    @pl.when(tb == 0)
    def _():
        st_ref[...] = s0_ref[...]

    row = _iota((cp, 1), 0).astype(F32)
    diff = (_iota((cp, cp), 0) - _iota((cp, cp), 1)).astype(F32)

    def chunk(c, carry):
        r0 = pl.multiple_of(c * cp, cp)
        rows = pl.ds(r0, cp)
        cos = cos_ref[rows, :]
        sin = sin_ref[rows, :]
        for h in range(R_HEADS):
            lg = math.log1p(-2.0 ** (-5.0 - h))
            cols = slice(h * R_DK, (h + 1) * R_DK)
            q = q_ref[rows, cols].astype(F32)
            k = k_ref[rows, cols].astype(F32)
            v = v_ref[rows, cols].astype(F32)
            q = q * cos + pltpu.roll(q, R_DK // 2, 1) * sin
            k = (k * cos + pltpu.roll(k, R_DK // 2, 1) * sin) * (R_DK ** -0.5)
            decay = jnp.where(diff >= 0, jnp.exp(lg * jnp.maximum(diff, 0.0)), 0.0)
            scores = _nt(q.astype(BF16), k.astype(BF16)) * decay
            s = st_ref[h]
            o = _dot(scores.astype(BF16), v.astype(BF16))
            o = o + _dot((q * jnp.exp(lg * (row + 1.0))).astype(BF16), s.astype(BF16))
            kd = jnp.where(row < c_true, jnp.exp(lg * jnp.maximum(c_true - 1.0 - row, 0.0)), 0.0)
            st_ref[h] = math.exp(lg * c_true) * s + _tn((k * kd).astype(BF16), v.astype(BF16))
            mu = jnp.mean(o, axis=-1, keepdims=True)
            d = o - mu
            var = jnp.mean(d * d, axis=-1, keepdims=True)
            o_ref[rows, cols] = (_silu(g_ref[rows, cols].astype(F32)) * (d * lax.rsqrt(var + GN_EPS))).astype(o_ref.dtype)
        return carry

    n_chunks = q_ref.shape[0] // cp
    lax.fori_loop(0, n_chunks, chunk, 0, unroll=2 if n_chunks % 2 == 0 else 1)

    @pl.when(tb == pl.num_programs(1) - 1)
    def _():
        s_out_ref[...] = st_ref[...]


def _seq_views(gseq, row_refs, seq_refs):
    rows = row_refs[0].shape[0] // gseq
    for g in range(gseq):
        yield [r.at[pl.ds(g * rows, rows)] for r in row_refs], [r.at[g] for r in seq_refs]


def _ret_kernel(q_ref, k_ref, v_ref, g_ref, cos_ref, sin_ref, s0_ref, o_ref, s_out_ref, st_ref,
                *, cp, c_true, gseq):
    for (q, k, v, g, o), (s0, s_out, st) in _seq_views(gseq, (q_ref, k_ref, v_ref, g_ref, o_ref), (s0_ref, s_out_ref, st_ref)):
        _ret_body(q, k, v, g, cos_ref, sin_ref, s0, o, s_out, st, cp, c_true)


def _hgrn_kernel(q_ref, f_ref, i_ref, g_ref, lb_ref, ng_ref, s0_ref, o_ref, s_out_ref,
                 st_ref, qs_ref, ks_ref, gs_ref, *, cp, c_true, sb, gseq):
    for (q, f, i, g, o, qs, ks, gs), (s0, s_out, st) in _seq_views(
            gseq, (q_ref, f_ref, i_ref, g_ref, o_ref, qs_ref, ks_ref, gs_ref), (s0_ref, s_out_ref, st_ref)):
        _hgrn_body(q, f, i, g, lb_ref, ng_ref, s0, o, s_out, st, qs, ks, gs, cp, c_true, sb)


def _hgrn_body(q_ref, f_ref, i_ref, g_ref, lb_ref, ng_ref, s0_ref, o_ref, s_out_ref,
               st_ref, qs_ref, ks_ref, gs_ref, cp, c_true, sb):
    tb = pl.program_id(1)

    @pl.when(tb == 0)
    def _():
        st_ref[...] = s0_ref[...]

    tri = _tri_incl(cp)
    rowi = _iota((cp, 1), 0)
    real = rowi < c_true
    sub_n = _iota((sb, 1), 0)
    ng = ng_ref[...]
    n_chunks = q_ref.shape[0] // cp
    unroll = 2 if n_chunks % 2 == 0 else 1
    n_sub = cp // sb

    def prepare(c, spread):
        rows = pl.ds(pl.multiple_of(c * cp, cp), cp)
        for h in range(H_HEADS):
            cols = slice(h * H_DK, (h + 1) * H_DK)
            lb = lb_ref[:, cols]
            f = f_ref[rows, cols].astype(F32)
            forget = lb + (1.0 - lb) * _sigmoid(f)
            logf = jnp.where(real, jnp.log(jnp.maximum(forget, MIN_FORGET)), 0.0)
            gc = _dot_exact_lhs(tri, logf)
            qs_ref[rows, cols] = _silu(q_ref[rows, cols].astype(F32)) * (H_DK ** -0.5)
            ks_ref[rows, cols] = jnp.where(real, (1.0 - lb) * _sigmoid(-f), 0.0)
            gs_ref[rows, cols] = gc
            for b in range(n_sub):
                spread = jnp.maximum(spread, gc[b * sb:b * sb + 1, :] - gc[b * sb + sb - 1:b * sb + sb, :])
        return spread

    spread = lax.fori_loop(0, n_chunks, prepare, jnp.zeros((1, H_DK), F32), unroll=unroll)
    safe = jnp.max(spread) < SAFE_EXP

    blk_of_row = rowi // sb
    causal = _iota((cp, cp), 0) >= _iota((cp, cp), 1)

    def intra_factored(q, k, gc):
        gref_rows = jnp.concatenate([jnp.broadcast_to(gc[b * sb:b * sb + 1, :], (sb, H_DK)) for b in range(n_sub)], axis=0)
        qd = q * jnp.exp(gc - gref_rows)
        lhs, rhs = [], []
        for b in range(n_sub):
            lhs.append(jnp.where(blk_of_row == b, qd, 0.0).astype(BF16))
            kd = k * jnp.exp(jnp.minimum(gc[b * sb:b * sb + 1, :] - gc, SAFE_EXP))
            rhs.append(jnp.where(rowi < (b + 1) * sb, kd, 0.0).astype(BF16))
        cat = (lambda t: t[0]) if n_sub == 1 else (lambda t: jnp.concatenate(t, axis=1))
        return jnp.where(causal, _nt(cat(lhs), cat(rhs)), 0.0)

    def intra_by_column(q, k, gc):
        rows_out = []
        for b in range(n_sub):
            b0 = b * sb
            gi = gc[b0:b0 + sb, :]
            qi = q[b0:b0 + sb, :]
            ki = k[b0:b0 + sb, :]
            lane_a = _iota((sb, cp), 1)
            if b0 > 0:
                gref = gc[b0:b0 + 1, :]
                kd = jnp.where(rowi < b0, k * jnp.exp(jnp.minimum(gref - gc, 0.0)), 0.0)
                a = _nt((qi * jnp.exp(gi - gref)).astype(BF16), kd.astype(BF16))
            else:
                a = jnp.zeros((sb, cp), F32)
            for m in range(sb):
                e = jnp.exp(jnp.where(sub_n >= m, gi - gi[m:m + 1, :], MASK_VALUE))
                col = jnp.sum(qi * e * ki[m:m + 1, :], axis=1, keepdims=True)
                a = jnp.where(lane_a == b0 + m, col, a)
            rows_out.append(a)
        return rows_out[0] if n_sub == 1 else jnp.concatenate(rows_out, axis=0)

    def make_chunk(intra):
        def chunk(c, carry):
            rows = pl.ds(pl.multiple_of(c * cp, cp), cp)
            for h in range(H_HEADS):
                cols = slice(h * H_DK, (h + 1) * H_DK)
                q = qs_ref[rows, cols]
                k = ks_ref[rows, cols]
                gc = gs_ref[rows, cols]
                vb = jnp.where(real, i_ref[rows, cols].astype(F32), 0.0).astype(BF16)
                sbf = st_ref[h].astype(BF16)
                qe = (q * jnp.exp(gc)).astype(BF16)
                am = intra(q, k, gc).astype(BF16)
                if cp % LANES == 0 or cp * 2 == LANES:
                    o = _dot(jnp.concatenate([qe, am], axis=1), jnp.concatenate([sbf, vb], axis=0))
                else:
                    o = _dot(qe, sbf) + _dot(am, vb)
                g_last = gc[cp - 1:cp, :]
                kd = k * jnp.exp(g_last - gc)
                eg_col = jnp.broadcast_to(jnp.exp(g_last), (SUBLANES, H_DK)).T[:, 0:1]
                st_ref[h] = eg_col * st_ref[h] + _tn(kd.astype(BF16), vb)
                ms = jnp.mean(o * o, axis=-1, keepdims=True)
                o = o * lax.rsqrt(ms + RMS_EPS) * ng
                o_ref[rows, cols] = (_silu(g_ref[rows, cols].astype(F32)) * o).astype(o_ref.dtype)
            return carry
        return chunk

    @pl.when(safe)
    def _():
        lax.fori_loop(0, n_chunks, make_chunk(intra_factored), 0, unroll=unroll)

    @pl.when(jnp.logical_not(safe))
    def _():
        lax.fori_loop(0, n_chunks, make_chunk(intra_by_column), 0)

    @pl.when(tb == pl.num_programs(1) - 1)
    def _():
        s_out_ref[...] = st_ref[...]


def _softplus(x):
    return jnp.maximum(x, 0.0) + jnp.log1p(jnp.exp(-jnp.abs(x)))


def _ssd_kernel(z_ref, xbc_ref, dt_ref, cw_ref, cb_ref, alog_ref, dtb_ref, dskip_ref, ng_ref,
                h0_ref, buf0_ref, o_ref, h_out_ref, buf_out_ref, st_ref, tail_ref, ext_ref,
                *, cp, c_true, gseq):
    for (z, xbc, dt, o), (h0, buf0, h_out, buf_out, st, tail, ext) in _seq_views(
            gseq, (z_ref, xbc_ref, dt_ref, o_ref), (h0_ref, buf0_ref, h_out_ref, buf_out_ref, st_ref, tail_ref, ext_ref)):
        _ssd_body(z, xbc, dt, cw_ref, cb_ref, alog_ref, dtb_ref, dskip_ref, ng_ref, h0, buf0, o, h_out, buf_out,
                  st, tail, ext, cp, c_true)


def _ssd_body(z_ref, xbc_ref, dt_ref, cw_ref, cb_ref, alog_ref, dtb_ref, dskip_ref, ng_ref,
              h0_ref, buf0_ref, o_ref, h_out_ref, buf_out_ref, st_ref, tail_ref, ext_ref, cp, c_true):
    tb = pl.program_id(1)

    @pl.when(tb == 0)
    def _():
        st_ref[...] = h0_ref[...]
        tail_ref[...] = buf0_ref[...]

    tri = _tri_incl(cp)
    rowi = _iota((cp, 1), 0)
    real = rowi < c_true
    causal = _iota((cp, cp), 0) >= _iota((cp, cp), 1)
    lane = _iota((1, LANES), 1)
    left = lane < M_HEAD_DIM
    top = _iota((LANES, 1), 0) < M_HEAD_DIM
    cw = cw_ref[...]
    cb = cb_ref[...]
    neg_a = -jnp.exp(alog_ref[...])
    gw = M_INNER // M_GROUPS

    def chunk(c, carry):
        r0 = pl.multiple_of(c * cp, cp)
        rows = pl.ds(r0, cp)
        u = xbc_ref[rows, :].astype(F32)
        ext_ref[0:SUBLANES, :] = tail_ref[...]
        ext_ref[SUBLANES:SUBLANES + cp, :] = u
        y = cb + cw[M_CONV - 1:M_CONV, :] * u
        for j in range(M_CONV - 1):
            y = y + cw[j:j + 1, :] * ext_ref[pl.ds(SUBLANES - (M_CONV - 1) + j, cp), :]
        tail_ref[...] = ext_ref[pl.ds(c_true, SUBLANES), :]
        xbc = _silu(y)
        dt = jnp.where(real, _softplus(dt_ref[rows, :].astype(F32) + dtb_ref[...]), 0.0)
        a = neg_a * dt
        cum = _dot_exact_lhs(tri, a)
        cum_t = cum.T
        dt_t = dt.T
        e_cum = jnp.exp(cum)
        cum_last = cum[cp - 1:cp, :]
        w_all = jnp.exp(cum_last - cum) * dt
        e_last = jnp.exp(cum_last)
        z = z_ref[rows, :].astype(F32)
        ys = []
        for pr in range(M_HEADS // 2):
            g = (2 * pr) // (M_HEADS // M_GROUPS)
            bm = xbc[:, M_INNER + g * M_STATE:M_INNER + (g + 1) * M_STATE].astype(BF16)
            cm = xbc[:, M_INNER + (M_GROUPS + g) * M_STATE:M_INNER + (M_GROUPS + g + 1) * M_STATE].astype(BF16)
            xp = xbc[:, pr * LANES:(pr + 1) * LANES]
            xpb = xp.astype(BF16)
            cbm = _nt(cm, bm)
            hp = st_ref[pr]
            cross = _nt(cm, hp.astype(BF16))
            intra = []
            for hh in range(2):
                h = 2 * pr + hh
                seg = jnp.exp(jnp.where(causal, cum[:, h:h + 1] - cum_t[h:h + 1, :], MASK_VALUE))
                mat = cbm * seg * dt_t[h:h + 1, :]
                intra.append(_dot(mat.astype(BF16), xpb))
            h0, h1 = 2 * pr, 2 * pr + 1
            y_pair = (jnp.where(left, intra[0], intra[1])
                      + cross * jnp.where(left, e_cum[:, h0:h0 + 1], e_cum[:, h1:h1 + 1]))
            wx = xp * jnp.where(left, w_all[:, h0:h0 + 1], w_all[:, h1:h1 + 1])
            decay = jnp.where(top, e_last[:, h0:h0 + 1], e_last[:, h1:h1 + 1])
            st_ref[pr] = decay * hp + _tn(wx.astype(BF16), bm)
            y_pair = y_pair + dskip_ref[:, pr * LANES:(pr + 1) * LANES] * xp
            ys.append(y_pair * _silu(z[:, pr * LANES:(pr + 1) * LANES]))
        for g in range(M_GROUPS):
            yg = jnp.concatenate(ys[g * 2:(g + 1) * 2], axis=1)
            ms = jnp.mean(yg * yg, axis=-1, keepdims=True)
            o_ref[rows, g * gw:(g + 1) * gw] = (yg * lax.rsqrt(ms + RMS_EPS)
                                                * ng_ref[:, g * gw:(g + 1) * gw]).astype(o_ref.dtype)
        return carry

    lax.fori_loop(0, z_ref.shape[0] // cp, chunk, 0)

    @pl.when(tb == pl.num_programs(1) - 1)
    def _():
        h_out_ref[...] = st_ref[...]
        buf_out_ref[...] = tail_ref[...]


def _row_blocks(batch, rows_per_seq, tb, gseq):
    n_tb = rows_per_seq // tb
    assert batch % gseq == 0 and (gseq == 1 or n_tb == 1), (batch, rows_per_seq, tb, gseq)
    return n_tb, (lambda col: (lambda b, t: (b * n_tb + t, col)))


def _state_specs(shape, gseq, layer):
    zeros = (0,) * len(shape)
    out = pl.BlockSpec((gseq, *shape), lambda b, t: (b, *zeros))
    if layer is None:
        return out, out
    return pl.BlockSpec((None, gseq, *shape), lambda b, t: (layer, b, *zeros)), out


def _retention(p, cos2, sin2, s0, batch, rows_per_seq, tb, cp, c_true, gseq=1, layer=None):
    n_tb, at = _row_blocks(batch, rows_per_seq, tb, gseq)
    st_in, st_spec = _state_specs((R_HEADS, R_DK, R_DV), gseq, layer)
    rb = gseq * tb
    return pl.pallas_call(
        functools.partial(_ret_kernel, cp=cp, c_true=c_true, gseq=gseq),
        grid=(batch // gseq, n_tb),
        in_specs=[pl.BlockSpec((rb, W), at(C_RQ // W)), pl.BlockSpec((rb, W), at(C_RK // W)),
                  pl.BlockSpec((rb, W), at(C_RV // W)), pl.BlockSpec((rb, W), at(C_RG // W)),
                  pl.BlockSpec((tb, R_DK), lambda b, t: (t, 0)), pl.BlockSpec((tb, R_DK), lambda b, t: (t, 0)),
                  st_in],
        out_specs=[pl.BlockSpec((rb, W), at(0)), st_spec],
        out_shape=[jax.ShapeDtypeStruct((batch * rows_per_seq, W), BF16),
                   jax.ShapeDtypeStruct((batch, R_HEADS, R_DK, R_DV), F32)],
        scratch_shapes=[pltpu.VMEM((gseq, R_HEADS, R_DK, R_DV), F32)],
        compiler_params=_cparams("parallel", "arbitrary"),
        name="retention",
    )(p, p, p, p, cos2, sin2, s0)


def _hgrn2(p, lb, norm_g, s0, batch, rows_per_seq, tb, cp, c_true, sb, gseq=1, layer=None):
    n_tb, at = _row_blocks(batch, rows_per_seq, tb, gseq)
    st_in, st_spec = _state_specs((H_HEADS, H_DK, H_DV), gseq, layer)
    rb = gseq * tb
    return pl.pallas_call(
        functools.partial(_hgrn_kernel, cp=cp, c_true=c_true, sb=sb, gseq=gseq),
        grid=(batch // gseq, n_tb),
        in_specs=[pl.BlockSpec((rb, W), at(C_CQ // W)), pl.BlockSpec((rb, W), at(C_CF // W)),
                  pl.BlockSpec((rb, W), at(C_CI // W)), pl.BlockSpec((rb, W), at(C_CG // W)),
                  pl.BlockSpec((1, W), lambda b, t: (0, 0)), pl.BlockSpec((1, H_DV), lambda b, t: (0, 0)),
                  st_in],
        out_specs=[pl.BlockSpec((rb, W), at(0)), st_spec],
        out_shape=[jax.ShapeDtypeStruct((batch * rows_per_seq, W), BF16),
                   jax.ShapeDtypeStruct((batch, H_HEADS, H_DK, H_DV), F32)],
        scratch_shapes=[pltpu.VMEM((gseq, H_HEADS, H_DK, H_DV), F32)] + [pltpu.VMEM((rb, W), F32)] * 3,
        compiler_params=_cparams("parallel", "arbitrary"),
        name="hgrn2",
    )(p, p, p, p, lb, norm_g, s0)


def _ssd(p, conv_w, conv_b, a_log, dt_bias, d_skip, norm_g, h0, buf0, batch, rows_per_seq, tb, cp, c_true, gseq=1,
         layer=None):
    n_tb, at = _row_blocks(batch, rows_per_seq, tb, gseq)
    rb = gseq * tb
    npair = M_HEADS // 2
    st_in, st_spec = _state_specs((npair, 2 * M_HEAD_DIM, M_STATE), gseq, layer)
    buf_in, buf_spec = _state_specs((SUBLANES, M_CONV_DIM), gseq, layer)
    row = lambda wdt: pl.BlockSpec((1, wdt), lambda b, t: (0, 0))
    return pl.pallas_call(
        functools.partial(_ssd_kernel, cp=cp, c_true=c_true, gseq=gseq),
        grid=(batch // gseq, n_tb),
        in_specs=[pl.BlockSpec((rb, W), at(C_MZ // W)), pl.BlockSpec((rb, M_CONV_DIM), at(C_XBC // M_CONV_DIM)),
                  pl.BlockSpec((rb, LANES), at(C_DT // LANES)),
                  pl.BlockSpec((M_CONV, M_CONV_DIM), lambda b, t: (0, 0)), row(M_CONV_DIM),
                  row(LANES), row(LANES), row(W), row(W), st_in, buf_in],
        out_specs=[pl.BlockSpec((rb, W), at(0)), st_spec, buf_spec],
        out_shape=[jax.ShapeDtypeStruct((batch * rows_per_seq, W), BF16),
                   jax.ShapeDtypeStruct((batch, npair, 2 * M_HEAD_DIM, M_STATE), F32),
                   jax.ShapeDtypeStruct((batch, SUBLANES, M_CONV_DIM), F32)],
        scratch_shapes=[pltpu.VMEM((gseq, npair, 2 * M_HEAD_DIM, M_STATE), F32),
                        pltpu.VMEM((gseq, SUBLANES, M_CONV_DIM), F32),
                        pltpu.VMEM((gseq, cp + SUBLANES, M_CONV_DIM), F32)],
        compiler_params=_cparams("parallel", "arbitrary"),
        name="ssd",
    )(p, p, p, conv_w, conv_b, a_log, dt_bias, d_skip, norm_g, h0, buf0)


def _idx_sample_kernel(pt_ref, iq_ref, iw_ref, iknew_ref, *rest, pp, n_pages, topk, t_new):
    spg = SUBLANES // t_new
    page_refs = rest[:spg * pp]
    mask_ref = rest[spg * pp]
    key_ref = rest[spg * pp + 1]
    cut_ref = rest[spg * pp + 2]
    del pt_ref
    b = pl.program_id(0)
    c = pl.program_id(1)
    qs = SUBLANES
    nrow = mask_ref.shape[1]
    iq = iq_ref[0]
    wcol = iw_ref[0]
    r0 = pl.multiple_of(b * qs, qs)
    row = _iota((qs, LANES), 0)
    lane_q = _iota((qs, LANES), 1)

    def scores(keys_t):
        lg = _dot(iq, keys_t)
        sc = (jnp.maximum(lg, 0.0) * wcol).reshape(IDX_HEADS, qs, LANES).sum(axis=0)
        return sc + 0.0

    def own_rows(per_seq):
        out = per_seq[0]
        for g in range(1, spg):
            out = jnp.where(row >= g * t_new, per_seq[g], out)
        return out

    for j in range(pp):
        sc = own_rows([scores(page_refs[g * pp + j][0, 0].astype(BF16)) for g in range(spg)])
        key_ref[c * pp + j, pl.ds(r0, qs), :] = _to_key(sc)

    @pl.when(c == pl.num_programs(1) - 1)
    def _():
        sc = own_rows([scores(iknew_ref[g]) for g in range(spg)])
        sm = jnp.where(lane_q <= row % t_new, sc, MASK_VALUE)
        key_ref[n_pages, pl.ds(r0, qs), :] = jnp.where(lane_q < t_new, _to_key(sm), INT_MIN)

    @pl.when((c == pl.num_programs(1) - 1) & (b == pl.num_programs(0) - 1))
    def _():
        n_all = n_pages + 1
        lane = _iota((nrow, LANES), 1)

        def count(pred):
            def body(pg, acc):
                return acc + jnp.where(pred(key_ref[pg], pg), 1, 0)
            acc = lax.fori_loop(0, n_all, body, jnp.zeros((nrow, LANES), I32))
            return jnp.sum(acc, axis=1, keepdims=True)

        def count_ge(cand):
            cb = jnp.broadcast_to(cand, (nrow, LANES))
            return count(lambda blk, pg: blk >= cb)

        thr, n_ge = _kth_largest_key(count_ge, topk, (nrow, 1), n_all * LANES)
        thr = jnp.broadcast_to(thr, (nrow, LANES))
        cut_ref[...] = jnp.full((nrow, LANES), 2 ** 30, I32)

        @pl.when(jnp.max(n_ge) > topk)
        def _():
            need = topk - count(lambda blk, pg: blk > thr)

            def body(t, ans):
                cand = ans + jnp.left_shift(jnp.int32(1), 15 - t)
                cb = jnp.broadcast_to(cand - 1, (nrow, LANES))
                upto = count(lambda blk, pg: (blk == thr) & ((pg * LANES + lane) <= cb))
                return jnp.where(upto < need, cand, ans)
            last = lax.fori_loop(0, 16, body, jnp.zeros((nrow, 1), I32))
            cut_ref[...] = jnp.broadcast_to(last, (nrow, LANES))

        last = cut_ref[...]
        lo = jnp.maximum(thr, VALID_KEY)
        tie_ok = thr > VALID_KEY

        def write(pg, carry):
            blk = key_ref[pg]
            sel = (blk > lo) | ((blk == thr) & tie_ok & ((pg * LANES + lane) <= last))
            mask_ref[pg] = jnp.where(sel, 0.0, MASK_VALUE)
            return carry

        lax.fori_loop(0, n_all, write, 0)


def _idx_sample(page_table, iq_rows, iw_col, ik_new_t, cache_kidx_t, layer, pp, t_new):
    nseq, n_pages = page_table.shape
    topk = min(TOPK_MAX, (n_pages * PAGE_SIZE + t_new) // 4)
    qs = SUBLANES
    spg = qs // t_new
    assert qs % t_new == 0 and nseq % spg == 0, (nseq, t_new)
    nrow = nseq * t_new

    def page_spec(g, j):
        return pl.BlockSpec((1, 1, IDX_DIM, PAGE_SIZE), lambda b, c, pt: (layer, pt[b * spg + g, c * pp + j], 0, 0))

    grid_spec = pltpu.PrefetchScalarGridSpec(
        num_scalar_prefetch=1,
        grid=(nseq // spg, n_pages // pp),
        in_specs=[pl.BlockSpec((1, qs * IDX_HEADS, IDX_DIM), lambda b, c, pt: (b, 0, 0)),
                  pl.BlockSpec((1, qs * IDX_HEADS, 1), lambda b, c, pt: (b, 0, 0)),
                  pl.BlockSpec((spg, IDX_DIM, PAGE_SIZE), lambda b, c, pt: (b, 0, 0))]
                 + [page_spec(g, j) for g in range(spg) for j in range(pp)],
        out_specs=pl.BlockSpec((n_pages + 1, nrow, LANES), lambda b, c, pt: (0, 0, 0)),
        scratch_shapes=[pltpu.VMEM((n_pages + 1, nrow, LANES), I32),
                        pltpu.VMEM((nrow, LANES), I32)],
    )
    return pl.pallas_call(
        functools.partial(_idx_sample_kernel, pp=pp, n_pages=n_pages, topk=topk, t_new=t_new),
        grid_spec=grid_spec,
        out_shape=jax.ShapeDtypeStruct((n_pages + 1, nrow, LANES), F32),
        compiler_params=_cparams("arbitrary", "arbitrary"),
        name="idx_sample",
    )(page_table, iq_rows, iw_col, ik_new_t, *([cache_kidx_t] * (spg * pp)))


def _attn_sample_kernel(pt_ref, q_ref, mask_ref, masknew_ref, knew_ref, vnew_ref, *rest, pp, t_new):
    k_refs = rest[:pp]
    v_refs = rest[pp:2 * pp]
    o_ref = rest[2 * pp]
    m_ref, l_ref, acc_ref = rest[2 * pp + 1:]
    del pt_ref
    c = pl.program_id(1)
    nrow = t_new * A_HEADS
    q = q_ref[0]

    @pl.when(c == 0)
    def _():
        m_ref[...] = jnp.full(m_ref.shape, MASK_VALUE, F32)
        l_ref[...] = jnp.zeros(l_ref.shape, F32)
        acc_ref[...] = jnp.zeros(acc_ref.shape, F32)

    def update(k_pages, v_pages, masks):
        s = []
        for kt, mk in zip(k_pages, masks):
            mrows = jnp.concatenate([jnp.broadcast_to(mk[t:t + 1, :], (A_HEADS, LANES)) for t in range(t_new)], axis=0)
            s.append(_dot(q, kt) + mrows)
        m_blk = s[0]
        for t in s[1:]:
            m_blk = jnp.maximum(m_blk, t)
        m_old = m_ref[...]
        m_new = jnp.maximum(m_old, jnp.max(m_blk, axis=1, keepdims=True))
        alpha = jnp.exp2(m_old - m_new)
        l_sum = jnp.zeros((nrow, LANES), F32)
        acc = alpha * acc_ref[...]
        for t, vt in zip(s, v_pages):
            e = jnp.exp2(t - m_new)
            l_sum = l_sum + e
            acc = acc + _nt(e.astype(BF16), vt)
        l_ref[...] = alpha * l_ref[...] + jnp.sum(l_sum, axis=1, keepdims=True)
        acc_ref[...] = acc
        m_ref[...] = m_new

    update([r[0, 0].astype(BF16) for r in k_refs], [r[0, 0].astype(BF16) for r in v_refs],
           [mask_ref[j] for j in range(pp)])

    @pl.when(c == pl.num_programs(1) - 1)
    def _():
        update([knew_ref[0]], [vnew_ref[0]], [masknew_ref[...]])
        o = acc_ref[...] / l_ref[...]
        grp = (_iota((nrow, 1), 0) % A_HEADS) // (A_HEADS // A_KV_HEADS)
        out = jnp.zeros((nrow, A_HEAD_DIM), F32)
        for g in range(A_KV_HEADS):
            out = out + jnp.where(grp == g, o[:, g * A_HEAD_DIM:(g + 1) * A_HEAD_DIM], 0.0)
        o_ref[0] = out


def _attn_sample(page_table, q_rows, mask, k_new_t, v_new_t, cache_k_t, cache_v_t, layer, pp, t_new):
    nseq, n_pages = page_table.shape
    kvw = A_KV_HEADS * A_HEAD_DIM
    nrow = t_new * A_HEADS
    mask4 = mask.reshape(n_pages + 1, nseq, t_new, LANES)

    def page_spec(j):
        return pl.BlockSpec((1, 1, kvw, PAGE_SIZE), lambda b, c, pt: (layer, pt[b, c * pp + j], 0, 0))

    grid_spec = pltpu.PrefetchScalarGridSpec(
        num_scalar_prefetch=1,
        grid=(nseq, n_pages // pp),
        in_specs=[pl.BlockSpec((1, nrow, kvw), lambda b, c, pt: (b, 0, 0)),
                  pl.BlockSpec((pp, None, t_new, LANES), lambda b, c, pt: (c, b, 0, 0)),
                  pl.BlockSpec((None, None, t_new, LANES), lambda b, c, pt: (n_pages, b, 0, 0)),
                  pl.BlockSpec((1, kvw, PAGE_SIZE), lambda b, c, pt: (b, 0, 0)),
                  pl.BlockSpec((1, kvw, PAGE_SIZE), lambda b, c, pt: (b, 0, 0))]
                 + [page_spec(j) for j in range(pp)] * 2,
        out_specs=pl.BlockSpec((1, nrow, A_HEAD_DIM), lambda b, c, pt: (b, 0, 0)),
        scratch_shapes=[pltpu.VMEM((nrow, 1), F32), pltpu.VMEM((nrow, 1), F32), pltpu.VMEM((nrow, kvw), F32)],
    )
    return pl.pallas_call(
        functools.partial(_attn_sample_kernel, pp=pp, t_new=t_new),
        grid_spec=grid_spec,
        out_shape=jax.ShapeDtypeStruct((nseq, nrow, A_HEAD_DIM), F32),
        compiler_params=_cparams("parallel", "arbitrary"),
        name="attn_sample",
    )(page_table, q_rows, mask4, mask4, k_new_t, v_new_t, *([cache_k_t] * pp), *([cache_v_t] * pp))


def _sprep_kernel(aq_ref, ak_ref, iq_ref, ikw_ref, qg_ref, kg_ref, qn_ref, kn_ref, iqs_ref, iw_ref):
    for c, t in enumerate(_head_rmsnorm(aq_ref[...], qg_ref[...])):
        qn_ref[:, c * LANES:(c + 1) * LANES] = (t * (A_HEAD_DIM ** -0.5 * LOG2_E)).astype(BF16)
    for c, t in enumerate(_head_rmsnorm(ak_ref[...], kg_ref[...])):
        kn_ref[:, c * LANES:(c + 1) * LANES] = t
    iqs_ref[...] = (iq_ref[...] * (IDX_DIM ** -0.5)).astype(BF16)
    iw_ref[...] = ikw_ref[...] * (IDX_HEADS ** -0.5)


def _sprep(p, q_gain, k_gain):
    n = p.shape[0]
    kvw = A_KV_HEADS * A_HEAD_DIM
    return pl.pallas_call(
        _sprep_kernel,
        grid=(1,),
        in_specs=[pl.BlockSpec((n, W), lambda i: (0, C_AQ // W)),
                  pl.BlockSpec((n, kvw), lambda i: (0, C_AK // kvw)),
                  pl.BlockSpec((n, W), lambda i: (0, C_IQ // W)),
                  pl.BlockSpec((n, LANES), lambda i: (0, C_IKW // LANES)),
                  pl.BlockSpec((1, LANES), lambda i: (0, 0)),
                  pl.BlockSpec((1, LANES), lambda i: (0, 0))],
        out_specs=[pl.BlockSpec((n, W), lambda i: (0, 0)), pl.BlockSpec((n, kvw), lambda i: (0, 0)),
                   pl.BlockSpec((n, W), lambda i: (0, 0)), pl.BlockSpec((n, LANES), lambda i: (0, 0))],
        out_shape=[jax.ShapeDtypeStruct((n, W), BF16), jax.ShapeDtypeStruct((n, kvw), F32),
                   jax.ShapeDtypeStruct((n, W), BF16), jax.ShapeDtypeStruct((n, LANES), F32)],
        compiler_params=_cparams("arbitrary"),
        name="sprep",
    )(p, p, p, p, q_gain, k_gain)


def _merge_kernel(oa_ref, ob_ref, oc_ref, od_ref, gates_ref, x_ref, wb_ref, wo_ref, n2_ref, x1_ref, h2_ref):
    merged = None
    for b, o_ref in enumerate((oa_ref, ob_ref, oc_ref, od_ref)):
        up = _dot(o_ref[...], wb_ref[b])
        t = _sigmoid(gates_ref[:, b * D_MODEL:(b + 1) * D_MODEL].astype(F32)) * up
        merged = t if merged is None else merged + t
    x1 = x_ref[...] + _dot(merged.astype(BF16), wo_ref[...])
    x1_ref[...] = x1
    ms = jnp.mean(x1 * x1, axis=-1, keepdims=True)
    h2_ref[...] = (x1 * lax.rsqrt(ms + RMS_EPS) * n2_ref[...]).astype(BF16)


def _merge(oa, ob, oc, od, p, x, w_branch, w_out, norm2, layer, tm):
    n = x.shape[0]
    row = lambda wdt: pl.BlockSpec((tm, wdt), lambda i: (i, 0))
    return pl.pallas_call(
        _merge_kernel,
        grid=(n // tm,),
        in_specs=[row(W), row(W), row(W), row(W),
                  pl.BlockSpec((tm, N_BRANCH * D_MODEL), lambda i: (i, C_GATES // (N_BRANCH * D_MODEL))),
                  row(D_MODEL),
                  pl.BlockSpec((None, N_BRANCH, W, D_MODEL), lambda i: (layer, 0, 0, 0)),
                  pl.BlockSpec((None, D_MODEL, D_MODEL), lambda i: (layer, 0, 0)),
                  pl.BlockSpec((1, D_MODEL), lambda i: (0, 0))],
        out_specs=[row(D_MODEL), row(D_MODEL)],
        out_shape=[jax.ShapeDtypeStruct((n, D_MODEL), F32), jax.ShapeDtypeStruct((n, D_MODEL), BF16)],
        compiler_params=_cparams("parallel"),
        name="merge",
    )(oa, ob, oc, od, p, x, w_branch, w_out, norm2)


def _ffn_kernel(h_ref, x_ref, wg_ref, wu_ref, wd_ref, o_ref, acc_ref):
    j = pl.program_id(1)

    @pl.when(j == 0)
    def _():
        acc_ref[...] = x_ref[...]

    h = h_ref[...]
    act = _silu(_dot(h, wg_ref[...])) * _dot(h, wu_ref[...])
    acc_ref[...] += _dot(act.astype(BF16), wd_ref[...])

    @pl.when(j == pl.num_programs(1) - 1)
    def _():
        o_ref[...] = acc_ref[...]


def _ffn(h2, x1, w_gu, w_down, layer, tm, fc):
    n = x1.shape[0]
    nf = D_FF // fc
    return pl.pallas_call(
        _ffn_kernel,
        grid=(n // tm, nf),
        in_specs=[pl.BlockSpec((tm, D_MODEL), lambda i, j: (i, 0)),
                  pl.BlockSpec((tm, D_MODEL), lambda i, j: (i, 0)),
                  pl.BlockSpec((None, D_MODEL, fc), lambda i, j: (layer, 0, j)),
                  pl.BlockSpec((None, D_MODEL, fc), lambda i, j: (layer, 0, nf + j)),
                  pl.BlockSpec((None, fc, D_MODEL), lambda i, j: (layer, j, 0))],
        out_specs=pl.BlockSpec((tm, D_MODEL), lambda i, j: (i, 0)),
        out_shape=jax.ShapeDtypeStruct((n, D_MODEL), F32),
        scratch_shapes=[pltpu.VMEM((tm, D_MODEL), F32)],
        compiler_params=_cparams("parallel", "arbitrary"),
        name="ffn",
    )(h2, x1, w_gu, w_gu, w_down)


def _pack_w_in(w_in):
    (a_q, a_k, a_v, a_iq, a_ik, a_iw, r_q, r_k, r_v, r_g, c_q, c_f, c_i, c_g,
     m_z, m_xbc, m_dt, gates) = jnp.split(w_in, SPLIT_POINTS, axis=-1)

    def pad_to_lanes(a):
        return jnp.pad(a, ((0, 0), (0, 0), (0, LANES - a.shape[-1])))

    ikw = pad_to_lanes(jnp.concatenate([a_ik, a_iw], axis=-1))
    packed = jnp.concatenate([gates, m_xbc, a_q, a_iq, r_q, r_k, r_v, r_g, c_q, c_f, c_i, c_g, m_z,
                              a_k, a_v, ikw, pad_to_lanes(m_dt)], axis=-1)
    return packed.astype(BF16)


def _rotary_tables(pos):
    half = R_DK // 2
    inv_freq = ROPE_BASE ** (-jnp.arange(half, dtype=F32) / half)
    ang = pos.astype(F32)[:, None] * inv_freq[None, :]
    c, s = jnp.cos(ang), jnp.sin(ang)
    return jnp.concatenate([c, c], axis=-1), jnp.concatenate([-s, s], axis=-1)


def _lower_bounds(lb_param):
    pr = jax.nn.softmax(lb_param.astype(F32), axis=0)
    return jnp.cumsum(pr, axis=0) - pr[0]


def _pad_lanes_row(v):
    return jnp.pad(v, (0, LANES - v.shape[0]))[None, :]


def kernel(x_prompt, x_sample, cache_k, cache_v, cache_kidx, state_ret, state_hgrn, state_ssm, state_conv,
           page_table, norm1_g, w_in, q_norm_g, k_norm_g, lb_param, hgrn_norm_g, A_log, dt_bias, D_skip,
           conv_w, conv_b, ssm_norm_g, w_branch, w_out, norm2_g, w_gu, w_down):
    bp, tp, _ = x_prompt.shape
    bs, ts, _ = x_sample.shape
    depth = w_in.shape[0]
    n_pool = cache_k.shape[1]
    n_pages = page_table.shape[1]
    kvw = A_KV_HEADS * A_HEAD_DIM
    tsp = SUBLANES
    npair = M_HEADS // 2
    gs = math.gcd(bs, 8)

    w_in_p = _pack_w_in(w_in)
    w_branch_b = w_branch.astype(BF16)
    w_out_b = w_out.astype(BF16)
    w_gu_b = w_gu.astype(BF16)
    w_down_b = w_down.astype(BF16)
    lbs = _lower_bounds(lb_param)
    cos_p, sin_p = _rotary_tables(jnp.arange(tp, dtype=jnp.int32))
    pos_s = n_pages * PAGE_SIZE + jnp.arange(tsp, dtype=jnp.int32)
    cos_s, sin_s = _rotary_tables(pos_s)
    ck_t = jnp.transpose(cache_k, (0, 1, 3, 4, 2)).reshape(depth, n_pool, kvw, PAGE_SIZE)
    cv_t = jnp.transpose(cache_v, (0, 1, 3, 4, 2)).reshape(depth, n_pool, kvw, PAGE_SIZE)
    ckidx_t = jnp.transpose(cache_kidx, (0, 1, 3, 2))
    head_group = (jnp.arange(A_HEADS) // (A_HEADS // A_KV_HEADS))[:, None] == jnp.arange(A_KV_HEADS)[None, :]
    ssm0 = state_ssm.reshape(depth, bs, npair, 2 * M_HEAD_DIM, M_STATE)
    buf0 = jnp.pad(state_conv, ((0, 0), (0, 0), (SUBLANES - (M_CONV - 1), 0), (0, 0)))

    zero_ret = jnp.zeros((bp, R_HEADS, R_DK, R_DV), F32)
    zero_hgrn = jnp.zeros((bp, H_HEADS, H_DK, H_DV), F32)
    zero_ssm = jnp.zeros((bp, npair, 2 * M_HEAD_DIM, M_STATE), F32)
    zero_buf = jnp.zeros((bp, SUBLANES, M_CONV_DIM), F32)

    xp = x_prompt.reshape(bp * tp, D_MODEL)
    xs = x_sample.reshape(bs * ts, D_MODEL)
    outs_p = [[] for _ in range(7)]
    outs_s = [[] for _ in range(7)]
    for l in range(depth):
        n1 = norm1_g[l][None, :]
        n2 = norm2_g[l][None, :]
        qg = jnp.tile(q_norm_g[l], 2)[None, :]
        kg = jnp.tile(k_norm_g[l], 2)[None, :]
        lb = lbs[l][None, :]
        hng = hgrn_norm_g[l][None, :]
        alog = _pad_lanes_row(A_log[l])
        dtb = _pad_lanes_row(dt_bias[l])
        dsk = jnp.repeat(D_skip[l], M_HEAD_DIM)[None, :]
        sng = ssm_norm_g[l][None, :]
        cw = conv_w[l]
        cb = conv_b[l][None, :]

        p = _inproj(xp, n1, w_in_p, l, 1024, 1280, BF16)
        kn, vv, ik, knb, vt, ik2 = _kprep(p, kg, 512)
        oa = _attn_prompt(p, knb, vt, ik2, qg, bp, tp, 256, 256)
        ob, ret_new = _retention(p, cos_p, sin_p, zero_ret, bp, tp, 512, R_CHUNK, R_CHUNK)
        oc, hgrn_new = _hgrn2(p, lb, hng, zero_hgrn, bp, tp, 512, H_CHUNK, H_CHUNK, 16)
        od, ssm_new, buf_new = _ssd(p, cw, cb, alog, dtb, dsk, sng, zero_ssm, zero_buf, bp, tp, 512, M_CHUNK, M_CHUNK)
        x1, h2 = _merge(oa, ob, oc, od, p, xp, w_branch_b, w_out_b, n2, l, 512)
        xp = _ffn(h2, x1, w_gu_b, w_down_b, l, 512, D_FF // 2)
        for i, a in enumerate((kn.reshape(bp, tp, A_KV_HEADS, A_HEAD_DIM), vv.reshape(bp, tp, A_KV_HEADS, A_HEAD_DIM),
                               ik.reshape(bp, tp, IDX_DIM), ret_new, hgrn_new,
                               ssm_new.reshape(bp, M_HEADS, M_HEAD_DIM, M_STATE),
                               buf_new[:, SUBLANES - (M_CONV - 1):, :])):
            outs_p[i].append(a)

        ps = _inproj(xs, n1, w_in_p, l, bs * ts, 1280, F32)
        qn_s, kn_s, iq_s, iw_s = _sprep(ps, qg, kg)
        v_s = ps[:, C_AV:C_AV + kvw]
        ik_s = ps[:, C_IKW:C_IKW + IDX_DIM]
        pad_q = lambda a: jnp.pad(a, ((0, 0), (0, tsp - ts)) + ((0, 0),) * (a.ndim - 2))
        new_keys_t = lambda a: jnp.swapaxes(jnp.pad(a.reshape(bs, ts, -1), ((0, 0), (0, PAGE_SIZE - ts), (0, 0))), 1, 2)
        spg = tsp // ts
        head_major = lambda a: jnp.swapaxes(a.reshape(bs // spg, tsp, IDX_HEADS, -1), 1, 2).reshape(bs // spg, IDX_HEADS * tsp, -1)
        iq_rows = head_major(iq_s)
        iw_col = head_major(iw_s[:, IDX_DIM:IDX_DIM + IDX_HEADS])
        mask = _idx_sample(page_table, iq_rows, iw_col, new_keys_t(ik_s).astype(BF16), ckidx_t, l, min(16, n_pages), ts)
        q4 = qn_s.reshape(bs, ts, A_HEADS, 1, A_HEAD_DIM)
        q_rows = jnp.where(head_group[None, None, :, :, None], q4, jnp.zeros_like(q4)).reshape(bs, ts * A_HEADS, kvw)
        oa_s = _attn_sample(page_table, q_rows, mask, new_keys_t(kn_s).astype(BF16), new_keys_t(v_s).astype(BF16),
                            ck_t, cv_t, l, min(32, n_pages), ts)
        oa_s = oa_s.reshape(bs * ts, W).astype(BF16)
        psp = pad_q(ps.reshape(bs, ts, NP)).reshape(bs * tsp, NP)
        ob_s, ret_s = _retention(psp, cos_s, sin_s, state_ret, bs, tsp, tsp, tsp, ts, gs, l)
        oc_s, hgrn_s = _hgrn2(psp, lb, hng, state_hgrn, bs, tsp, tsp, tsp, ts, tsp, gs, l)
        od_s, ssm_s, buf_s = _ssd(psp, cw, cb, alog, dtb, dsk, sng, ssm0, buf0, bs, tsp, tsp, tsp, ts, gs, l)
        unpad = lambda a: a.reshape(bs, tsp, W)[:, :ts].reshape(bs * ts, W)
        x1s, h2s = _merge(oa_s, unpad(ob_s), unpad(oc_s), unpad(od_s), ps, xs, w_branch_b, w_out_b, n2, l, bs * ts)
        xs = _ffn(h2s, x1s, w_gu_b, w_down_b, l, bs * ts, D_FF // 2)
        for i, a in enumerate((kn_s.reshape(bs, ts, A_KV_HEADS, A_HEAD_DIM), v_s.reshape(bs, ts, A_KV_HEADS, A_HEAD_DIM),
                               ik_s.reshape(bs, ts, IDX_DIM), ret_s, hgrn_s,
                               ssm_s.reshape(bs, M_HEADS, M_HEAD_DIM, M_STATE),
                               buf_s[:, SUBLANES - (M_CONV - 1):, :])):
            outs_s[i].append(a)

    dtypes = (cache_k.dtype, cache_v.dtype, cache_kidx.dtype, state_ret.dtype,
              state_hgrn.dtype, state_ssm.dtype, state_conv.dtype)
    res_p = [jnp.stack(a).astype(d) for a, d in zip(outs_p, dtypes)]
    res_s = [jnp.stack(a).astype(d) for a, d in zip(outs_s, dtypes)]
    return (xp.reshape(bp, tp, D_MODEL), xs.reshape(bs, ts, D_MODEL), *res_p, *res_s)
```

```python
import functools
import math
import struct

import jax
import jax.numpy as jnp
from jax import lax
from jax.experimental import pallas as pl
from jax.experimental.pallas import tpu as pltpu

F32 = jnp.float32
BF16 = jnp.bfloat16
I32 = jnp.int32

D_MODEL = 1024
PAGE_SIZE = 128
BRANCH_WIDTH = D_MODEL // 2
N_BRANCH = 4
A_HEAD_DIM = 64
A_HEADS = BRANCH_WIDTH // A_HEAD_DIM
A_KV_HEADS = A_HEADS // 2
IDX_HEADS = 8
IDX_DIM = 64
TOPK_MAX = 256
Q_BLOCK = 128
R_HEADS = 4
R_DK = BRANCH_WIDTH // R_HEADS
R_DV = BRANCH_WIDTH // R_HEADS
R_CHUNK = 128
ROPE_BASE = 10000.0
H_HEADS = 4
H_DK = 128
H_DV = BRANCH_WIDTH // H_HEADS
H_CHUNK = 64
MIN_FORGET = 1e-30
SAFE_EXP = 60.0
M_HEAD_DIM = 64
M_HEADS = BRANCH_WIDTH // M_HEAD_DIM
M_INNER = M_HEADS * M_HEAD_DIM
M_GROUPS = 2
M_STATE = 128
M_CONV = 4
M_CHUNK = 128
M_CONV_DIM = M_INNER + 2 * M_GROUPS * M_STATE
D_FF = -(-8 * D_MODEL // (3 * 256)) * 256
RMS_EPS = 1e-6
GN_EPS = 1e-6
MASK_VALUE = -1e30

SPLIT_SIZES = (
    A_HEADS * A_HEAD_DIM, A_KV_HEADS * A_HEAD_DIM, A_KV_HEADS * A_HEAD_DIM,
    IDX_HEADS * IDX_DIM, IDX_DIM, IDX_HEADS,
    R_HEADS * R_DK, R_HEADS * R_DK, R_HEADS * R_DV, R_HEADS * R_DV,
    H_HEADS * H_DK, H_HEADS * H_DK, H_HEADS * H_DV, H_HEADS * H_DV,
    M_INNER, M_CONV_DIM, M_HEADS,
    N_BRANCH * D_MODEL,
)
SPLIT_POINTS = tuple(sum(SPLIT_SIZES[:i + 1]) for i in range(len(SPLIT_SIZES) - 1))

LANES = 128
SUBLANES = 8
VMEM_LIMIT = 56 * 1024 * 1024

C_GATES = 0
C_XBC = 4096
C_AQ = 5120
C_IQ = 5632
C_RQ = 6144
C_RK = 6656
C_RV = 7168
C_RG = 7680
C_CQ = 8192
C_CF = 8704
C_CI = 9216
C_CG = 9728
C_MZ = 10240
C_AK = 10752
C_AV = 11008
C_IKW = 11264
C_DT = 11392
NP = 11520
W = BRANCH_WIDTH


def _float_key(x):
    b = struct.unpack("<i", struct.pack("<f", x))[0]
    return b if b >= 0 else b ^ 0x7FFFFFFF


VALID_KEY = _float_key(0.5 * MASK_VALUE)
INT_MIN = -2 ** 31
LOG2_E = math.log2(math.e)


def _cparams(*sem):
    return pltpu.CompilerParams(dimension_semantics=sem, vmem_limit_bytes=VMEM_LIMIT)


def _nt(a, b):
    return lax.dot_general(a, b, (((1,), (1,)), ((), ())), preferred_element_type=F32)


def _tn(a, b):
    return lax.dot_general(a, b, (((0,), (0,)), ((), ())), preferred_element_type=F32)


def _dot(a, b):
    return jnp.dot(a, b, preferred_element_type=F32)


def _dot_exact_lhs(a01, x):
    a = a01.astype(BF16)
    x0 = x.astype(BF16)
    r1 = x - x0.astype(F32)
    x1 = r1.astype(BF16)
    x2 = (r1 - x1.astype(F32)).astype(BF16)
    return _dot(a, x0) + _dot(a, x1) + _dot(a, x2)


def _dot_exact_rhs(x, b01):
    b = b01.astype(BF16)
    x0 = x.astype(BF16)
    r1 = x - x0.astype(F32)
    x1 = r1.astype(BF16)
    x2 = (r1 - x1.astype(F32)).astype(BF16)
    return _dot(x0, b) + _dot(x1, b) + _dot(x2, b)


def _sigmoid(x):
    return 1.0 / (1.0 + jnp.exp(-x))


def _silu(x):
    return x * _sigmoid(x)


def _iota(shape, dim):
    return lax.broadcasted_iota(I32, shape, dim)


def _tri_incl(n):
    return (_iota((n, n), 0) >= _iota((n, n), 1)).astype(F32)


def _inproj_kernel(x_ref, g_ref, w_ref, o_ref, h_ref):
    @pl.when(pl.program_id(1) == 0)
    def _():
        x = x_ref[...]
        ms = jnp.mean(x * x, axis=-1, keepdims=True)
        h_ref[...] = (x * lax.rsqrt(ms + RMS_EPS) * g_ref[...]).astype(BF16)

    o_ref[...] = _dot(h_ref[...], w_ref[...]).astype(o_ref.dtype)


def _inproj(x, g, w, layer, tm, tn, out_dtype):
    n = x.shape[0]
    return pl.pallas_call(
        _inproj_kernel,
        grid=(n // tm, NP // tn),
        in_specs=[pl.BlockSpec((tm, D_MODEL), lambda i, j: (i, 0)),
                  pl.BlockSpec((1, D_MODEL), lambda i, j: (0, 0)),
                  pl.BlockSpec((None, D_MODEL, tn), lambda i, j: (layer, 0, j))],
        out_specs=pl.BlockSpec((tm, tn), lambda i, j: (i, j)),
        out_shape=jax.ShapeDtypeStruct((n, NP), out_dtype),
        scratch_shapes=[pltpu.VMEM((tm, D_MODEL), BF16)],
        compiler_params=_cparams("parallel", "arbitrary"),
        name="inproj",
    )(x, g, w)


def _head_rmsnorm(x, gain_row):
    pair = (_iota((LANES, LANES), 0) // A_HEAD_DIM == _iota((LANES, LANES), 1) // A_HEAD_DIM).astype(F32)
    outs = []
    for c in range(x.shape[1] // LANES):
        xc = x[:, c * LANES:(c + 1) * LANES]
        ms = _dot_exact_rhs(xc * xc, pair) * (1.0 / A_HEAD_DIM)
        outs.append(xc * lax.rsqrt(ms + RMS_EPS) * gain_row)
    return outs


def _kth_largest_key(count_ge, k, shape, n_keys):
    c0 = count_ge(jnp.zeros(shape, I32))
    nonneg = c0 >= k
    prefix = jnp.where(nonneg, 0, INT_MIN).astype(I32)
    c_prefix = jnp.where(nonneg, c0, n_keys).astype(I32)

    def body(t, carry):
        prefix, c_prefix = carry
        cand = prefix + jnp.left_shift(jnp.int32(1), 30 - t)
        c = count_ge(cand)
        take = c >= k
        return jnp.where(take, cand, prefix), jnp.where(take, c, c_prefix)

    return lax.fori_loop(0, 31, body, (prefix, c_prefix))


def _to_key(s):
    b = lax.bitcast_convert_type(s, I32)
    return jnp.where(b >= 0, b, b ^ 0x7FFFFFFF)


def _kprep_kernel(ak_ref, av_ref, ikw_ref, kg_ref, kn_ref, v_ref, ik_ref, knb_ref, vt_ref, ik2_ref):
    kn = _head_rmsnorm(ak_ref[...].astype(F32), kg_ref[...])
    for c, t in enumerate(kn):
        kn_ref[:, c * LANES:(c + 1) * LANES] = t
        knb_ref[:, c * LANES:(c + 1) * LANES] = t.astype(BF16)
    v = av_ref[...].astype(F32)
    v_ref[...] = v
    vt_ref[...] = v.T.astype(BF16)
    ikw = ikw_ref[...].astype(F32)
    ik_ref[...] = ikw[:, :IDX_DIM]
    lane = _iota(ikw.shape, 1)
    ik2_ref[...] = jnp.where(lane < IDX_DIM, ikw, pltpu.roll(ikw, IDX_DIM, 1)).astype(BF16)


def _kprep(p, k_gain, tm):
    n = p.shape[0]
    kvw = A_KV_HEADS * A_HEAD_DIM
    row = lambda wdt: pl.BlockSpec((tm, wdt), lambda i: (i, 0))
    return pl.pallas_call(
        _kprep_kernel,
        grid=(n // tm,),
        in_specs=[pl.BlockSpec((tm, kvw), lambda i: (i, C_AK // kvw)),
                  pl.BlockSpec((tm, kvw), lambda i: (i, C_AV // kvw)),
                  pl.BlockSpec((tm, LANES), lambda i: (i, C_IKW // LANES)),
                  pl.BlockSpec((1, LANES), lambda i: (0, 0))],
        out_specs=[row(kvw), row(kvw), row(IDX_DIM), row(kvw), pl.BlockSpec((kvw, tm), lambda i: (0, i)), row(LANES)],
        out_shape=[jax.ShapeDtypeStruct((n, kvw), F32),
                   jax.ShapeDtypeStruct((n, kvw), F32),
                   jax.ShapeDtypeStruct((n, IDX_DIM), F32),
                   jax.ShapeDtypeStruct((n, kvw), BF16),
                   jax.ShapeDtypeStruct((kvw, n), BF16),
                   jax.ShapeDtypeStruct((n, LANES), BF16)],
        compiler_params=_cparams("parallel"),
        name="kprep",
    )(p, p, p, k_gain)


def _attn_prompt_kernel(aq_ref, iq_ref, ikw_ref, qg_ref, knb_ref, vt_ref, ik2_ref, o_ref,
                        key_ref, msk_ref, s_ref, qt_ref, iqt_ref, acc_ref, cut_ref, *, topk, kb, qb):
    i = pl.program_id(1)
    nblk = (i * qb + qb + kb - 1) // kb
    row = _iota((kb, qb), 0)
    lane = _iota((kb, qb), 1)
    lane_q = _iota((qb, LANES), 1)
    own_half = [lane_q < A_HEAD_DIM, lane_q >= A_HEAD_DIM]
    n_pairs = A_KV_HEADS // 2
    hpp = A_HEADS // n_pairs
    nsub = kb // SUBLANES

    qn = _head_rmsnorm(aq_ref[...].astype(F32), qg_ref[...])
    for h in range(A_HEADS):
        src, dst = h % 2, (h // 2) % 2
        t = qn[h // 2] * (A_HEAD_DIM ** -0.5 * LOG2_E)
        if src != dst:
            t = pltpu.roll(t, A_HEAD_DIM, 1)
        t = jnp.where(own_half[dst], t, 0.0)
        qt_ref[h // hpp, :, (h % hpp) * qb:(h % hpp + 1) * qb] = t.T.astype(BF16)
    for h in range(IDX_HEADS):
        t = iq_ref[:, (h // 2) * LANES:(h // 2 + 1) * LANES].astype(F32) * (IDX_DIM ** -0.5)
        iqt_ref[:, h * qb:(h + 1) * qb] = jnp.where(own_half[h % 2], t, 0.0).T.astype(BF16)
    iw_t = ikw_ref[...].astype(F32).T[IDX_DIM:IDX_DIM + IDX_HEADS, :] * (IDX_HEADS ** -0.5)

    def score_blk(j, carry):
        ks = pl.multiple_of(j * kb, kb)
        lg = _dot(ik2_ref[pl.ds(ks, kb), :], iqt_ref[...])
        acc = jnp.zeros((kb, qb), F32)
        for h in range(IDX_HEADS):
            acc = acc + jnp.maximum(lg[:, h * qb:(h + 1) * qb], 0.0) * iw_t[h:h + 1, :]
        adm = (ks + row) <= (i * qb + lane)
        key_ref[j] = _to_key(jnp.where(adm, acc, MASK_VALUE))
        return carry

    lax.fori_loop(0, nblk, score_blk, 0)

    def count(pred):
        def body(j, c):
            return c + jnp.sum(jnp.where(pred(key_ref[j], j), 1, 0).reshape(nsub, SUBLANES, qb), axis=0)
        c = lax.fori_loop(0, nblk, body, jnp.zeros((SUBLANES, qb), I32))
        return jnp.sum(c, axis=0, keepdims=True)

    cut_ref[0:1, :] = jnp.full((1, qb), INT_MIN, I32)
    cut_ref[1:2, :] = jnp.full((1, qb), 2 ** 30, I32)

    @pl.when(i * qb + qb > topk)
    def _():
        thr, n_ge = _kth_largest_key(lambda cand: count(lambda blk, j: blk >= cand), topk, (1, qb), nblk * kb)
        cut_ref[0:1, :] = thr

        @pl.when(jnp.max(n_ge) > topk)
        def _():
            need = topk - count(lambda blk, j: blk > thr)

            def body(t, ans):
                cand = ans + jnp.left_shift(jnp.int32(1), 15 - t)
                upto = count(lambda blk, j: (blk == thr) & ((j * kb + row) <= cand - 1))
                return jnp.where(upto < need, cand, ans)

            cut_ref[1:2, :] = lax.fori_loop(0, 16, body, jnp.zeros((1, qb), I32))

    thr = cut_ref[0:1, :]
    last = cut_ref[1:2, :]
    lo = jnp.maximum(thr, VALID_KEY)
    tie_ok = thr > VALID_KEY

    def mask_blk(j, carry):
        blk = key_ref[j]
        sel = (blk > lo) | ((blk == thr) & tie_ok & ((j * kb + row) <= last))
        msk_ref[j] = jnp.where(sel, 0.0, MASK_VALUE)
        return carry

    lax.fori_loop(0, nblk, mask_blk, 0)

    def fold(t, op):
        return op(t.reshape(nsub, SUBLANES, qb), axis=0)

    def logits_blk(j, m8):
        ks = pl.multiple_of(j * kb, kb)
        mk = msk_ref[j]
        out = []
        for pr in range(n_pairs):
            s4 = _dot(knb_ref[pl.ds(ks, kb), pr * LANES:(pr + 1) * LANES], qt_ref[pr])
            for hh in range(hpp):
                h = pr * hpp + hh
                s = s4[:, hh * qb:(hh + 1) * qb] + mk
                s_ref[j, h] = s
                out.append(jnp.maximum(m8[h], fold(s, jnp.max)))
        return tuple(out)

    m8 = lax.fori_loop(0, nblk, logits_blk, tuple(jnp.full((SUBLANES, qb), MASK_VALUE, F32) for _ in range(A_HEADS)))
    m_row = [jnp.max(t, axis=0, keepdims=True) for t in m8]

    acc_ref[...] = jnp.zeros(acc_ref.shape, F32)

    def values_blk(j, l8):
        ks = pl.multiple_of(j * kb, kb)
        out = []
        for h in range(A_HEADS):
            g = h // (A_HEADS // A_KV_HEADS)
            e = jnp.exp2(s_ref[j, h] - m_row[h])
            out.append(l8[h] + fold(e, jnp.sum))
            vt = vt_ref[g * A_HEAD_DIM:(g + 1) * A_HEAD_DIM, pl.ds(ks, kb)]
            acc_ref[h] = acc_ref[h] + _dot(vt, e.astype(BF16))
        return tuple(out)

    l8 = lax.fori_loop(0, nblk, values_blk, tuple(jnp.zeros((SUBLANES, qb), F32) for _ in range(A_HEADS)))

    for h in range(A_HEADS):
        l_row = jnp.sum(l8[h], axis=0, keepdims=True)
        o_ref[:, h * A_HEAD_DIM:(h + 1) * A_HEAD_DIM] = (acc_ref[h] / l_row).T.astype(o_ref.dtype)


def _attn_prompt(p, knb, vt, ik2, q_gain, batch, seq, kb, qb):
    nqb = seq // qb
    nkb = seq // kb
    kvw = A_KV_HEADS * A_HEAD_DIM
    topk = min(TOPK_MAX, seq // 4)
    n_pairs = A_KV_HEADS // 2
    hpp = A_HEADS // n_pairs
    return pl.pallas_call(
        functools.partial(_attn_prompt_kernel, topk=topk, kb=kb, qb=qb),
        grid=(batch, nqb),
        in_specs=[pl.BlockSpec((qb, W), lambda b, i: (b * nqb + i, C_AQ // W)),
                  pl.BlockSpec((qb, W), lambda b, i: (b * nqb + i, C_IQ // W)),
                  pl.BlockSpec((qb, LANES), lambda b, i: (b * nqb + i, C_IKW // LANES)),
                  pl.BlockSpec((1, LANES), lambda b, i: (0, 0)),
                  pl.BlockSpec((seq, kvw), lambda b, i: (b, 0)),
                  pl.BlockSpec((kvw, seq), lambda b, i: (0, b)),
                  pl.BlockSpec((seq, LANES), lambda b, i: (b, 0))],
        out_specs=pl.BlockSpec((qb, W), lambda b, i: (b * nqb + i, 0)),
        out_shape=jax.ShapeDtypeStruct((batch * seq, W), BF16),
        scratch_shapes=[pltpu.VMEM((nkb, kb, qb), I32),
                        pltpu.VMEM((nkb, kb, qb), F32),
                        pltpu.VMEM((nkb, A_HEADS, kb, qb), F32),
                        pltpu.VMEM((n_pairs, LANES, hpp * qb), BF16),
                        pltpu.VMEM((LANES, IDX_HEADS * qb), BF16),
                        pltpu.VMEM((A_HEADS, A_HEAD_DIM, qb), F32),
                        pltpu.VMEM((SUBLANES, qb), I32)],
        compiler_params=_cparams("parallel", "arbitrary"),
        name="attn_prompt",
    )(p, p, p, q_gain, knb, vt, ik2)


def _ret_body(q_ref, k_ref, v_ref, g_ref, cos_ref, sin_ref, s0_ref, o_ref, s_out_ref, st_ref, cp, c_true):
    tb = pl.program_id(1)

    @pl.when(tb == 0)
    def _():
        st_ref[...] = s0_ref[...]

    row = _iota((cp, 1), 0).astype(F32)
    diff = (_iota((cp, cp), 0) - _iota((cp, cp), 1)).astype(F32)

    def chunk(c, carry):
        r0 = pl.multiple_of(c * cp, cp)
        rows = pl.ds(r0, cp)
        cos = cos_ref[rows, :]
        sin = sin_ref[rows, :]
        for h in range(R_HEADS):
            lg = math.log1p(-2.0 ** (-5.0 - h))
            cols = slice(h * R_DK, (h + 1) * R_DK)
            q = q_ref[rows, cols].astype(F32)
            k = k_ref[rows, cols].astype(F32)
            v = v_ref[rows, cols].astype(F32)
            q = q * cos + pltpu.roll(q, R_DK // 2, 1) * sin
            k = (k * cos + pltpu.roll(k, R_DK // 2, 1) * sin) * (R_DK ** -0.5)
            decay = jnp.where(diff >= 0, jnp.exp(lg * jnp.maximum(diff, 0.0)), 0.0)
            scores = _nt(q.astype(BF16), k.astype(BF16)) * decay
            s = st_ref[h]
            o = _dot(scores.astype(BF16), v.astype(BF16))
            o = o + _dot((q * jnp.exp(lg * (row + 1.0))).astype(BF16), s.astype(BF16))
            kd = jnp.where(row < c_true, jnp.exp(lg * jnp.maximum(c_true - 1.0 - row, 0.0)), 0.0)
            st_ref[h] = math.exp(lg * c_true) * s + _tn((k * kd).astype(BF16), v.astype(BF16))
            mu = jnp.mean(o, axis=-1, keepdims=True)
            d = o - mu
            var = jnp.mean(d * d, axis=-1, keepdims=True)
            o_ref[rows, cols] = (_silu(g_ref[rows, cols].astype(F32)) * (d * lax.rsqrt(var + GN_EPS))).astype(o_ref.dtype)
        return carry

    n_chunks = q_ref.shape[0] // cp
    lax.fori_loop(0, n_chunks, chunk, 0, unroll=2 if n_chunks % 2 == 0 else 1)

    @pl.when(tb == pl.num_programs(1) - 1)
    def _():
        s_out_ref[...] = st_ref[...]


def _seq_views(gseq, row_refs, seq_refs):
    rows = row_refs[0].shape[0] // gseq
    for g in range(gseq):
        yield [r.at[pl.ds(g * rows, rows)] for r in row_refs], [r.at[g] for r in seq_refs]


def _ret_kernel(q_ref, k_ref, v_ref, g_ref, cos_ref, sin_ref, s0_ref, o_ref, s_out_ref, st_ref,
                *, cp, c_true, gseq):
    for (q, k, v, g, o), (s0, s_out, st) in _seq_views(gseq, (q_ref, k_ref, v_ref, g_ref, o_ref), (s0_ref, s_out_ref, st_ref)):
        _ret_body(q, k, v, g, cos_ref, sin_ref, s0, o, s_out, st, cp, c_true)


def _hgrn_kernel(q_ref, f_ref, i_ref, g_ref, lb_ref, ng_ref, s0_ref, o_ref, s_out_ref,
                 st_ref, qs_ref, ks_ref, gs_ref, *, cp, c_true, sb, gseq):
    for (q, f, i, g, o, qs, ks, gs), (s0, s_out, st) in _seq_views(
            gseq, (q_ref, f_ref, i_ref, g_ref, o_ref, qs_ref, ks_ref, gs_ref), (s0_ref, s_out_ref, st_ref)):
        _hgrn_body(q, f, i, g, lb_ref, ng_ref, s0, o, s_out, st, qs, ks, gs, cp, c_true, sb)


def _hgrn_body(q_ref, f_ref, i_ref, g_ref, lb_ref, ng_ref, s0_ref, o_ref, s_out_ref,
               st_ref, qs_ref, ks_ref, gs_ref, cp, c_true, sb):
    tb = pl.program_id(1)

    @pl.when(tb == 0)
    def _():
        st_ref[...] = s0_ref[...]

    tri = _tri_incl(cp)
    rowi = _iota((cp, 1), 0)
    real = rowi < c_true
    sub_n = _iota((sb, 1), 0)
    ng = ng_ref[...]
    n_chunks = q_ref.shape[0] // cp
    unroll = 2 if n_chunks % 2 == 0 else 1
    n_sub = cp // sb

    def prepare(c, spread):
        rows = pl.ds(pl.multiple_of(c * cp, cp), cp)
        for h in range(H_HEADS):
            cols = slice(h * H_DK, (h + 1) * H_DK)
            lb = lb_ref[:, cols]
            f = f_ref[rows, cols].astype(F32)
            forget = lb + (1.0 - lb) * _sigmoid(f)
            logf = jnp.where(real, jnp.log(jnp.maximum(forget, MIN_FORGET)), 0.0)
            gc = _dot_exact_lhs(tri, logf)
            qs_ref[rows, cols] = _silu(q_ref[rows, cols].astype(F32)) * (H_DK ** -0.5)
            ks_ref[rows, cols] = jnp.where(real, (1.0 - lb) * _sigmoid(-f), 0.0)
            gs_ref[rows, cols] = gc
            for b in range(n_sub):
                spread = jnp.maximum(spread, gc[b * sb:b * sb + 1, :] - gc[b * sb + sb - 1:b * sb + sb, :])
        return spread

    spread = lax.fori_loop(0, n_chunks, prepare, jnp.zeros((1, H_DK), F32), unroll=unroll)
    safe = jnp.max(spread) < SAFE_EXP

    blk_of_row = rowi // sb
    causal = _iota((cp, cp), 0) >= _iota((cp, cp), 1)

    def intra_factored(q, k, gc):
        gref_rows = jnp.concatenate([jnp.broadcast_to(gc[b * sb:b * sb + 1, :], (sb, H_DK)) for b in range(n_sub)], axis=0)
        qd = q * jnp.exp(gc - gref_rows)
        lhs, rhs = [], []
        for b in range(n_sub):
            lhs.append(jnp.where(blk_of_row == b, qd, 0.0).astype(BF16))
            kd = k * jnp.exp(jnp.minimum(gc[b * sb:b * sb + 1, :] - gc, SAFE_EXP))
            rhs.append(jnp.where(rowi < (b + 1) * sb, kd, 0.0).astype(BF16))
        cat = (lambda t: t[0]) if n_sub == 1 else (lambda t: jnp.concatenate(t, axis=1))
        return jnp.where(causal, _nt(cat(lhs), cat(rhs)), 0.0)

    def intra_by_column(q, k, gc):
        rows_out = []
        for b in range(n_sub):
            b0 = b * sb
            gi = gc[b0:b0 + sb, :]
            qi = q[b0:b0 + sb, :]
            ki = k[b0:b0 + sb, :]
            lane_a = _iota((sb, cp), 1)
            if b0 > 0:
                gref = gc[b0:b0 + 1, :]
                kd = jnp.where(rowi < b0, k * jnp.exp(jnp.minimum(gref - gc, 0.0)), 0.0)
                a = _nt((qi * jnp.exp(gi - gref)).astype(BF16), kd.astype(BF16))
            else:
                a = jnp.zeros((sb, cp), F32)
            for m in range(sb):
                e = jnp.exp(jnp.where(sub_n >= m, gi - gi[m:m + 1, :], MASK_VALUE))
                col = jnp.sum(qi * e * ki[m:m + 1, :], axis=1, keepdims=True)
                a = jnp.where(lane_a == b0 + m, col, a)
            rows_out.append(a)
        return rows_out[0] if n_sub == 1 else jnp.concatenate(rows_out, axis=0)

    def make_chunk(intra):
        def chunk(c, carry):
            rows = pl.ds(pl.multiple_of(c * cp, cp), cp)
            for h in range(H_HEADS):
                cols = slice(h * H_DK, (h + 1) * H_DK)
                q = qs_ref[rows, cols]
                k = ks_ref[rows, cols]
                gc = gs_ref[rows, cols]
                vb = jnp.where(real, i_ref[rows, cols].astype(F32), 0.0).astype(BF16)
                sbf = st_ref[h].astype(BF16)
                qe = (q * jnp.exp(gc)).astype(BF16)
                am = intra(q, k, gc).astype(BF16)
                if cp % LANES == 0 or cp * 2 == LANES:
                    o = _dot(jnp.concatenate([qe, am], axis=1), jnp.concatenate([sbf, vb], axis=0))
                else:
                    o = _dot(qe, sbf) + _dot(am, vb)
                g_last = gc[cp - 1:cp, :]
                kd = k * jnp.exp(g_last - gc)
                eg_col = jnp.broadcast_to(jnp.exp(g_last), (SUBLANES, H_DK)).T[:, 0:1]
                st_ref[h] = eg_col * st_ref[h] + _tn(kd.astype(BF16), vb)
                ms = jnp.mean(o * o, axis=-1, keepdims=True)
                o = o * lax.rsqrt(ms + RMS_EPS) * ng
                o_ref[rows, cols] = (_silu(g_ref[rows, cols].astype(F32)) * o).astype(o_ref.dtype)
            return carry
        return chunk

    @pl.when(safe)
    def _():
        lax.fori_loop(0, n_chunks, make_chunk(intra_factored), 0, unroll=unroll)

    @pl.when(jnp.logical_not(safe))
    def _():
        lax.fori_loop(0, n_chunks, make_chunk(intra_by_column), 0)

    @pl.when(tb == pl.num_programs(1) - 1)
    def _():
        s_out_ref[...] = st_ref[...]


def _softplus(x):
    return jnp.maximum(x, 0.0) + jnp.log1p(jnp.exp(-jnp.abs(x)))


def _ssd_kernel(z_ref, xbc_ref, dt_ref, cw_ref, cb_ref, alog_ref, dtb_ref, dskip_ref, ng_ref,
                h0_ref, buf0_ref, o_ref, h_out_ref, buf_out_ref, st_ref, tail_ref, ext_ref,
                *, cp, c_true, gseq):
    for (z, xbc, dt, o), (h0, buf0, h_out, buf_out, st, tail, ext) in _seq_views(
            gseq, (z_ref, xbc_ref, dt_ref, o_ref), (h0_ref, buf0_ref, h_out_ref, buf_out_ref, st_ref, tail_ref, ext_ref)):
        _ssd_body(z, xbc, dt, cw_ref, cb_ref, alog_ref, dtb_ref, dskip_ref, ng_ref, h0, buf0, o, h_out, buf_out,
                  st, tail, ext, cp, c_true)


def _ssd_body(z_ref, xbc_ref, dt_ref, cw_ref, cb_ref, alog_ref, dtb_ref, dskip_ref, ng_ref,
              h0_ref, buf0_ref, o_ref, h_out_ref, buf_out_ref, st_ref, tail_ref, ext_ref, cp, c_true):
    tb = pl.program_id(1)

    @pl.when(tb == 0)
    def _():
        st_ref[...] = h0_ref[...]
        tail_ref[...] = buf0_ref[...]

    tri = _tri_incl(cp)
    rowi = _iota((cp, 1), 0)
    real = rowi < c_true
    causal = _iota((cp, cp), 0) >= _iota((cp, cp), 1)
    lane = _iota((1, LANES), 1)
    left = lane < M_HEAD_DIM
    top = _iota((LANES, 1), 0) < M_HEAD_DIM
    cw = cw_ref[...]
    cb = cb_ref[...]
    neg_a = -jnp.exp(alog_ref[...])
    gw = M_INNER // M_GROUPS

    def chunk(c, carry):
        r0 = pl.multiple_of(c * cp, cp)
        rows = pl.ds(r0, cp)
        u = xbc_ref[rows, :].astype(F32)
        ext_ref[0:SUBLANES, :] = tail_ref[...]
        ext_ref[SUBLANES:SUBLANES + cp, :] = u
        y = cb + cw[M_CONV - 1:M_CONV, :] * u
        for j in range(M_CONV - 1):
            y = y + cw[j:j + 1, :] * ext_ref[pl.ds(SUBLANES - (M_CONV - 1) + j, cp), :]
        tail_ref[...] = ext_ref[pl.ds(c_true, SUBLANES), :]
        xbc = _silu(y)
        dt = jnp.where(real, _softplus(dt_ref[rows, :].astype(F32) + dtb_ref[...]), 0.0)
        a = neg_a * dt
        cum = _dot_exact_lhs(tri, a)
        cum_t = cum.T
        dt_t = dt.T
        e_cum = jnp.exp(cum)
        cum_last = cum[cp - 1:cp, :]
        w_all = jnp.exp(cum_last - cum) * dt
        e_last = jnp.exp(cum_last)
        z = z_ref[rows, :].astype(F32)
        ys = []
        for pr in range(M_HEADS // 2):
            g = (2 * pr) // (M_HEADS // M_GROUPS)
            bm = xbc[:, M_INNER + g * M_STATE:M_INNER + (g + 1) * M_STATE].astype(BF16)
            cm = xbc[:, M_INNER + (M_GROUPS + g) * M_STATE:M_INNER + (M_GROUPS + g + 1) * M_STATE].astype(BF16)
            xp = xbc[:, pr * LANES:(pr + 1) * LANES]
            xpb = xp.astype(BF16)
            cbm = _nt(cm, bm)
            hp = st_ref[pr]
            cross = _nt(cm, hp.astype(BF16))
            intra = []
            for hh in range(2):
                h = 2 * pr + hh
                seg = jnp.exp(jnp.where(causal, cum[:, h:h + 1] - cum_t[h:h + 1, :], MASK_VALUE))
                mat = cbm * seg * dt_t[h:h + 1, :]
                intra.append(_dot(mat.astype(BF16), xpb))
            h0, h1 = 2 * pr, 2 * pr + 1
            y_pair = (jnp.where(left, intra[0], intra[1])
                      + cross * jnp.where(left, e_cum[:, h0:h0 + 1], e_cum[:, h1:h1 + 1]))
            wx = xp * jnp.where(left, w_all[:, h0:h0 + 1], w_all[:, h1:h1 + 1])
            decay = jnp.where(top, e_last[:, h0:h0 + 1], e_last[:, h1:h1 + 1])
            st_ref[pr] = decay * hp + _tn(wx.astype(BF16), bm)
            y_pair = y_pair + dskip_ref[:, pr * LANES:(pr + 1) * LANES] * xp
            ys.append(y_pair * _silu(z[:, pr * LANES:(pr + 1) * LANES]))
        for g in range(M_GROUPS):
            yg = jnp.concatenate(ys[g * 2:(g + 1) * 2], axis=1)
            ms = jnp.mean(yg * yg, axis=-1, keepdims=True)
            o_ref[rows, g * gw:(g + 1) * gw] = (yg * lax.rsqrt(ms + RMS_EPS)
                                                * ng_ref[:, g * gw:(g + 1) * gw]).astype(o_ref.dtype)
        return carry

    lax.fori_loop(0, z_ref.shape[0] // cp, chunk, 0)

    @pl.when(tb == pl.num_programs(1) - 1)
    def _():
        h_out_ref[...] = st_ref[...]
        buf_out_ref[...] = tail_ref[...]


def _row_blocks(batch, rows_per_seq, tb, gseq):
    n_tb = rows_per_seq // tb
    assert batch % gseq == 0 and (gseq == 1 or n_tb == 1), (batch, rows_per_seq, tb, gseq)
    return n_tb, (lambda col: (lambda b, t: (b * n_tb + t, col)))


def _state_specs(shape, gseq, layer):
    zeros = (0,) * len(shape)
    out = pl.BlockSpec((gseq, *shape), lambda b, t: (b, *zeros))
    if layer is None:
        return out, out
    return pl.BlockSpec((None, gseq, *shape), lambda b, t: (layer, b, *zeros)), out


def _retention(p, cos2, sin2, s0, batch, rows_per_seq, tb, cp, c_true, gseq=1, layer=None):
    n_tb, at = _row_blocks(batch, rows_per_seq, tb, gseq)
    st_in, st_spec = _state_specs((R_HEADS, R_DK, R_DV), gseq, layer)
    rb = gseq * tb
    return pl.pallas_call(
        functools.partial(_ret_kernel, cp=cp, c_true=c_true, gseq=gseq),
        grid=(batch // gseq, n_tb),
        in_specs=[pl.BlockSpec((rb, W), at(C_RQ // W)), pl.BlockSpec((rb, W), at(C_RK // W)),
                  pl.BlockSpec((rb, W), at(C_RV // W)), pl.BlockSpec((rb, W), at(C_RG // W)),
                  pl.BlockSpec((tb, R_DK), lambda b, t: (t, 0)), pl.BlockSpec((tb, R_DK), lambda b, t: (t, 0)),
                  st_in],
        out_specs=[pl.BlockSpec((rb, W), at(0)), st_spec],
        out_shape=[jax.ShapeDtypeStruct((batch * rows_per_seq, W), BF16),
                   jax.ShapeDtypeStruct((batch, R_HEADS, R_DK, R_DV), F32)],
        scratch_shapes=[pltpu.VMEM((gseq, R_HEADS, R_DK, R_DV), F32)],
        compiler_params=_cparams("parallel", "arbitrary"),
        name="retention",
    )(p, p, p, p, cos2, sin2, s0)


def _hgrn2(p, lb, norm_g, s0, batch, rows_per_seq, tb, cp, c_true, sb, gseq=1, layer=None):
    n_tb, at = _row_blocks(batch, rows_per_seq, tb, gseq)
    st_in, st_spec = _state_specs((H_HEADS, H_DK, H_DV), gseq, layer)
    rb = gseq * tb
    return pl.pallas_call(
        functools.partial(_hgrn_kernel, cp=cp, c_true=c_true, sb=sb, gseq=gseq),
        grid=(batch // gseq, n_tb),
        in_specs=[pl.BlockSpec((rb, W), at(C_CQ // W)), pl.BlockSpec((rb, W), at(C_CF // W)),
                  pl.BlockSpec((rb, W), at(C_CI // W)), pl.BlockSpec((rb, W), at(C_CG // W)),
                  pl.BlockSpec((1, W), lambda b, t: (0, 0)), pl.BlockSpec((1, H_DV), lambda b, t: (0, 0)),
                  st_in],
        out_specs=[pl.BlockSpec((rb, W), at(0)), st_spec],
        out_shape=[jax.ShapeDtypeStruct((batch * rows_per_seq, W), BF16),
                   jax.ShapeDtypeStruct((batch, H_HEADS, H_DK, H_DV), F32)],
        scratch_shapes=[pltpu.VMEM((gseq, H_HEADS, H_DK, H_DV), F32)] + [pltpu.VMEM((rb, W), F32)] * 3,
        compiler_params=_cparams("parallel", "arbitrary"),
        name="hgrn2",
    )(p, p, p, p, lb, norm_g, s0)


def _ssd(p, conv_w, conv_b, a_log, dt_bias, d_skip, norm_g, h0, buf0, batch, rows_per_seq, tb, cp, c_true, gseq=1,
         layer=None):
    n_tb, at = _row_blocks(batch, rows_per_seq, tb, gseq)
    rb = gseq * tb
    npair = M_HEADS // 2
    st_in, st_spec = _state_specs((npair, 2 * M_HEAD_DIM, M_STATE), gseq, layer)
    buf_in, buf_spec = _state_specs((SUBLANES, M_CONV_DIM), gseq, layer)
    row = lambda wdt: pl.BlockSpec((1, wdt), lambda b, t: (0, 0))
    return pl.pallas_call(
        functools.partial(_ssd_kernel, cp=cp, c_true=c_true, gseq=gseq),
        grid=(batch // gseq, n_tb),
        in_specs=[pl.BlockSpec((rb, W), at(C_MZ // W)), pl.BlockSpec((rb, M_CONV_DIM), at(C_XBC // M_CONV_DIM)),
                  pl.BlockSpec((rb, LANES), at(C_DT // LANES)),
                  pl.BlockSpec((M_CONV, M_CONV_DIM), lambda b, t: (0, 0)), row(M_CONV_DIM),
                  row(LANES), row(LANES), row(W), row(W), st_in, buf_in],
        out_specs=[pl.BlockSpec((rb, W), at(0)), st_spec, buf_spec],
        out_shape=[jax.ShapeDtypeStruct((batch * rows_per_seq, W), BF16),
                   jax.ShapeDtypeStruct((batch, npair, 2 * M_HEAD_DIM, M_STATE), F32),
                   jax.ShapeDtypeStruct((batch, SUBLANES, M_CONV_DIM), F32)],
        scratch_shapes=[pltpu.VMEM((gseq, npair, 2 * M_HEAD_DIM, M_STATE), F32),
                        pltpu.VMEM((gseq, SUBLANES, M_CONV_DIM), F32),
                        pltpu.VMEM((gseq, cp + SUBLANES, M_CONV_DIM), F32)],
        compiler_params=_cparams("parallel", "arbitrary"),
        name="ssd",
    )(p, p, p, conv_w, conv_b, a_log, dt_bias, d_skip, norm_g, h0, buf0)


def _idx_sample_kernel(pt_ref, iq_ref, iw_ref, iknew_ref, *rest, pp, n_pages, topk, t_new):
    spg = SUBLANES // t_new
    page_refs = rest[:spg * pp]
    mask_ref = rest[spg * pp]
    key_ref = rest[spg * pp + 1]
    cut_ref = rest[spg * pp + 2]
    del pt_ref
    b = pl.program_id(0)
    c = pl.program_id(1)
    qs = SUBLANES
    nrow = mask_ref.shape[1]
    iq = iq_ref[0]
    wcol = iw_ref[0]
    r0 = pl.multiple_of(b * qs, qs)
    row = _iota((qs, LANES), 0)
    lane_q = _iota((qs, LANES), 1)

    def scores(keys_t):
        lg = _dot(iq, keys_t)
        sc = (jnp.maximum(lg, 0.0) * wcol).reshape(IDX_HEADS, qs, LANES).sum(axis=0)
        return sc + 0.0

    def own_rows(per_seq):
        out = per_seq[0]
        for g in range(1, spg):
            out = jnp.where(row >= g * t_new, per_seq[g], out)
        return out

    for j in range(pp):
        sc = own_rows([scores(page_refs[g * pp + j][0, 0].astype(BF16)) for g in range(spg)])
        key_ref[c * pp + j, pl.ds(r0, qs), :] = _to_key(sc)

    @pl.when(c == pl.num_programs(1) - 1)
    def _():
        sc = own_rows([scores(iknew_ref[g]) for g in range(spg)])
        sm = jnp.where(lane_q <= row % t_new, sc, MASK_VALUE)
        key_ref[n_pages, pl.ds(r0, qs), :] = jnp.where(lane_q < t_new, _to_key(sm), INT_MIN)

    @pl.when((c == pl.num_programs(1) - 1) & (b == pl.num_programs(0) - 1))
    def _():
        n_all = n_pages + 1
        lane = _iota((nrow, LANES), 1)

        def count(pred):
            def body(pg, acc):
                return acc + jnp.where(pred(key_ref[pg], pg), 1, 0)
            acc = lax.fori_loop(0, n_all, body, jnp.zeros((nrow, LANES), I32))
            return jnp.sum(acc, axis=1, keepdims=True)

        def count_ge(cand):
            cb = jnp.broadcast_to(cand, (nrow, LANES))
            return count(lambda blk, pg: blk >= cb)

        thr, n_ge = _kth_largest_key(count_ge, topk, (nrow, 1), n_all * LANES)
        thr = jnp.broadcast_to(thr, (nrow, LANES))
        cut_ref[...] = jnp.full((nrow, LANES), 2 ** 30, I32)

        @pl.when(jnp.max(n_ge) > topk)
        def _():
            need = topk - count(lambda blk, pg: blk > thr)

            def body(t, ans):
                cand = ans + jnp.left_shift(jnp.int32(1), 15 - t)
                cb = jnp.broadcast_to(cand - 1, (nrow, LANES))
                upto = count(lambda blk, pg: (blk == thr) & ((pg * LANES + lane) <= cb))
                return jnp.where(upto < need, cand, ans)
            last = lax.fori_loop(0, 16, body, jnp.zeros((nrow, 1), I32))
            cut_ref[...] = jnp.broadcast_to(last, (nrow, LANES))

        last = cut_ref[...]
        lo = jnp.maximum(thr, VALID_KEY)
        tie_ok = thr > VALID_KEY

        def write(pg, carry):
            blk = key_ref[pg]
            sel = (blk > lo) | ((blk == thr) & tie_ok & ((pg * LANES + lane) <= last))
            mask_ref[pg] = jnp.where(sel, 0.0, MASK_VALUE)
            return carry

        lax.fori_loop(0, n_all, write, 0)


def _idx_sample(page_table, iq_rows, iw_col, ik_new_t, cache_kidx_t, layer, pp, t_new):
    nseq, n_pages = page_table.shape
    topk = min(TOPK_MAX, (n_pages * PAGE_SIZE + t_new) // 4)
    qs = SUBLANES
    spg = qs // t_new
    assert qs % t_new == 0 and nseq % spg == 0, (nseq, t_new)
    nrow = nseq * t_new

    def page_spec(g, j):
        return pl.BlockSpec((1, 1, IDX_DIM, PAGE_SIZE), lambda b, c, pt: (layer, pt[b * spg + g, c * pp + j], 0, 0))

    grid_spec = pltpu.PrefetchScalarGridSpec(
        num_scalar_prefetch=1,
        grid=(nseq // spg, n_pages // pp),
        in_specs=[pl.BlockSpec((1, qs * IDX_HEADS, IDX_DIM), lambda b, c, pt: (b, 0, 0)),
                  pl.BlockSpec((1, qs * IDX_HEADS, 1), lambda b, c, pt: (b, 0, 0)),
                  pl.BlockSpec((spg, IDX_DIM, PAGE_SIZE), lambda b, c, pt: (b, 0, 0))]
                 + [page_spec(g, j) for g in range(spg) for j in range(pp)],
        out_specs=pl.BlockSpec((n_pages + 1, nrow, LANES), lambda b, c, pt: (0, 0, 0)),
        scratch_shapes=[pltpu.VMEM((n_pages + 1, nrow, LANES), I32),
                        pltpu.VMEM((nrow, LANES), I32)],
    )
    return pl.pallas_call(
        functools.partial(_idx_sample_kernel, pp=pp, n_pages=n_pages, topk=topk, t_new=t_new),
        grid_spec=grid_spec,
        out_shape=jax.ShapeDtypeStruct((n_pages + 1, nrow, LANES), F32),
        compiler_params=_cparams("arbitrary", "arbitrary"),
        name="idx_sample",
    )(page_table, iq_rows, iw_col, ik_new_t, *([cache_kidx_t] * (spg * pp)))


def _attn_sample_kernel(pt_ref, q_ref, mask_ref, masknew_ref, knew_ref, vnew_ref, *rest, pp, t_new):
    k_refs = rest[:pp]
    v_refs = rest[pp:2 * pp]
    o_ref = rest[2 * pp]
    m_ref, l_ref, acc_ref = rest[2 * pp + 1:]
    del pt_ref
    c = pl.program_id(1)
    nrow = t_new * A_HEADS
    q = q_ref[0]

    @pl.when(c == 0)
    def _():
        m_ref[...] = jnp.full(m_ref.shape, MASK_VALUE, F32)
        l_ref[...] = jnp.zeros(l_ref.shape, F32)
        acc_ref[...] = jnp.zeros(acc_ref.shape, F32)

    def update(k_pages, v_pages, masks):
        s = []
        for kt, mk in zip(k_pages, masks):
            mrows = jnp.concatenate([jnp.broadcast_to(mk[t:t + 1, :], (A_HEADS, LANES)) for t in range(t_new)], axis=0)
            s.append(_dot(q, kt) + mrows)
        m_blk = s[0]
        for t in s[1:]:
            m_blk = jnp.maximum(m_blk, t)
        m_old = m_ref[...]
        m_new = jnp.maximum(m_old, jnp.max(m_blk, axis=1, keepdims=True))
        alpha = jnp.exp2(m_old - m_new)
        l_sum = jnp.zeros((nrow, LANES), F32)
        acc = alpha * acc_ref[...]
        for t, vt in zip(s, v_pages):
            e = jnp.exp2(t - m_new)
            l_sum = l_sum + e
            acc = acc + _nt(e.astype(BF16), vt)
        l_ref[...] = alpha * l_ref[...] + jnp.sum(l_sum, axis=1, keepdims=True)
        acc_ref[...] = acc
        m_ref[...] = m_new

    update([r[0, 0].astype(BF16) for r in k_refs], [r[0, 0].astype(BF16) for r in v_refs],
           [mask_ref[j] for j in range(pp)])

    @pl.when(c == pl.num_programs(1) - 1)
    def _():
        update([knew_ref[0]], [vnew_ref[0]], [masknew_ref[...]])
        o = acc_ref[...] / l_ref[...]
        grp = (_iota((nrow, 1), 0) % A_HEADS) // (A_HEADS // A_KV_HEADS)
        out = jnp.zeros((nrow, A_HEAD_DIM), F32)
        for g in range(A_KV_HEADS):
            out = out + jnp.where(grp == g, o[:, g * A_HEAD_DIM:(g + 1) * A_HEAD_DIM], 0.0)
        o_ref[0] = out


def _attn_sample(page_table, q_rows, mask, k_new_t, v_new_t, cache_k_t, cache_v_t, layer, pp, t_new):
    nseq, n_pages = page_table.shape
    kvw = A_KV_HEADS * A_HEAD_DIM
    nrow = t_new * A_HEADS
    mask4 = mask.reshape(n_pages + 1, nseq, t_new, LANES)

    def page_spec(j):
        return pl.BlockSpec((1, 1, kvw, PAGE_SIZE), lambda b, c, pt: (layer, pt[b, c * pp + j], 0, 0))

    grid_spec = pltpu.PrefetchScalarGridSpec(
        num_scalar_prefetch=1,
        grid=(nseq, n_pages // pp),
        in_specs=[pl.BlockSpec((1, nrow, kvw), lambda b, c, pt: (b, 0, 0)),
                  pl.BlockSpec((pp, None, t_new, LANES), lambda b, c, pt: (c, b, 0, 0)),
                  pl.BlockSpec((None, None, t_new, LANES), lambda b, c, pt: (n_pages, b, 0, 0)),
                  pl.BlockSpec((1, kvw, PAGE_SIZE), lambda b, c, pt: (b, 0, 0)),
                  pl.BlockSpec((1, kvw, PAGE_SIZE), lambda b, c, pt: (b, 0, 0))]
                 + [page_spec(j) for j in range(pp)] * 2,
        out_specs=pl.BlockSpec((1, nrow, A_HEAD_DIM), lambda b, c, pt: (b, 0, 0)),
        scratch_shapes=[pltpu.VMEM((nrow, 1), F32), pltpu.VMEM((nrow, 1), F32), pltpu.VMEM((nrow, kvw), F32)],
    )
    return pl.pallas_call(
        functools.partial(_attn_sample_kernel, pp=pp, t_new=t_new),
        grid_spec=grid_spec,
        out_shape=jax.ShapeDtypeStruct((nseq, nrow, A_HEAD_DIM), F32),
        compiler_params=_cparams("parallel", "arbitrary"),
        name="attn_sample",
    )(page_table, q_rows, mask4, mask4, k_new_t, v_new_t, *([cache_k_t] * pp), *([cache_v_t] * pp))


def _sprep_kernel(aq_ref, ak_ref, iq_ref, ikw_ref, qg_ref, kg_ref, qn_ref, kn_ref, iqs_ref, iw_ref):
    for c, t in enumerate(_head_rmsnorm(aq_ref[...], qg_ref[...])):
        qn_ref[:, c * LANES:(c + 1) * LANES] = (t * (A_HEAD_DIM ** -0.5 * LOG2_E)).astype(BF16)
    for c, t in enumerate(_head_rmsnorm(ak_ref[...], kg_ref[...])):
        kn_ref[:, c * LANES:(c + 1) * LANES] = t
    iqs_ref[...] = (iq_ref[...] * (IDX_DIM ** -0.5)).astype(BF16)
    iw_ref[...] = ikw_ref[...] * (IDX_HEADS ** -0.5)


def _sprep(p, q_gain, k_gain):
    n = p.shape[0]
    kvw = A_KV_HEADS * A_HEAD_DIM
    return pl.pallas_call(
        _sprep_kernel,
        grid=(1,),
        in_specs=[pl.BlockSpec((n, W), lambda i: (0, C_AQ // W)),
                  pl.BlockSpec((n, kvw), lambda i: (0, C_AK // kvw)),
                  pl.BlockSpec((n, W), lambda i: (0, C_IQ // W)),
                  pl.BlockSpec((n, LANES), lambda i: (0, C_IKW // LANES)),
                  pl.BlockSpec((1, LANES), lambda i: (0, 0)),
                  pl.BlockSpec((1, LANES), lambda i: (0, 0))],
        out_specs=[pl.BlockSpec((n, W), lambda i: (0, 0)), pl.BlockSpec((n, kvw), lambda i: (0, 0)),
                   pl.BlockSpec((n, W), lambda i: (0, 0)), pl.BlockSpec((n, LANES), lambda i: (0, 0))],
        out_shape=[jax.ShapeDtypeStruct((n, W), BF16), jax.ShapeDtypeStruct((n, kvw), F32),
                   jax.ShapeDtypeStruct((n, W), BF16), jax.ShapeDtypeStruct((n, LANES), F32)],
        compiler_params=_cparams("arbitrary"),
        name="sprep",
    )(p, p, p, p, q_gain, k_gain)


def _merge_kernel(oa_ref, ob_ref, oc_ref, od_ref, gates_ref, x_ref, wb_ref, wo_ref, n2_ref, x1_ref, h2_ref):
    merged = None
    for b, o_ref in enumerate((oa_ref, ob_ref, oc_ref, od_ref)):
        up = _dot(o_ref[...], wb_ref[b])
        t = _sigmoid(gates_ref[:, b * D_MODEL:(b + 1) * D_MODEL].astype(F32)) * up
        merged = t if merged is None else merged + t
    x1 = x_ref[...] + _dot(merged.astype(BF16), wo_ref[...])
    x1_ref[...] = x1
    ms = jnp.mean(x1 * x1, axis=-1, keepdims=True)
    h2_ref[...] = (x1 * lax.rsqrt(ms + RMS_EPS) * n2_ref[...]).astype(BF16)


def _merge(oa, ob, oc, od, p, x, w_branch, w_out, norm2, layer, tm):
    n = x.shape[0]
    row = lambda wdt: pl.BlockSpec((tm, wdt), lambda i: (i, 0))
    return pl.pallas_call(
        _merge_kernel,
        grid=(n // tm,),
        in_specs=[row(W), row(W), row(W), row(W),
                  pl.BlockSpec((tm, N_BRANCH * D_MODEL), lambda i: (i, C_GATES // (N_BRANCH * D_MODEL))),
                  row(D_MODEL),
                  pl.BlockSpec((None, N_BRANCH, W, D_MODEL), lambda i: (layer, 0, 0, 0)),
                  pl.BlockSpec((None, D_MODEL, D_MODEL), lambda i: (layer, 0, 0)),
                  pl.BlockSpec((1, D_MODEL), lambda i: (0, 0))],
        out_specs=[row(D_MODEL), row(D_MODEL)],
        out_shape=[jax.ShapeDtypeStruct((n, D_MODEL), F32), jax.ShapeDtypeStruct((n, D_MODEL), BF16)],
        compiler_params=_cparams("parallel"),
        name="merge",
    )(oa, ob, oc, od, p, x, w_branch, w_out, norm2)


def _ffn_kernel(h_ref, x_ref, wg_ref, wu_ref, wd_ref, o_ref, acc_ref):
    j = pl.program_id(1)

    @pl.when(j == 0)
    def _():
        acc_ref[...] = x_ref[...]

    h = h_ref[...]
    act = _silu(_dot(h, wg_ref[...])) * _dot(h, wu_ref[...])
    acc_ref[...] += _dot(act.astype(BF16), wd_ref[...])

    @pl.when(j == pl.num_programs(1) - 1)
    def _():
        o_ref[...] = acc_ref[...]


def _ffn(h2, x1, w_gu, w_down, layer, tm, fc):
    n = x1.shape[0]
    nf = D_FF // fc
    return pl.pallas_call(
        _ffn_kernel,
        grid=(n // tm, nf),
        in_specs=[pl.BlockSpec((tm, D_MODEL), lambda i, j: (i, 0)),
                  pl.BlockSpec((tm, D_MODEL), lambda i, j: (i, 0)),
                  pl.BlockSpec((None, D_MODEL, fc), lambda i, j: (layer, 0, j)),
                  pl.BlockSpec((None, D_MODEL, fc), lambda i, j: (layer, 0, nf + j)),
                  pl.BlockSpec((None, fc, D_MODEL), lambda i, j: (layer, j, 0))],
        out_specs=pl.BlockSpec((tm, D_MODEL), lambda i, j: (i, 0)),
        out_shape=jax.ShapeDtypeStruct((n, D_MODEL), F32),
        scratch_shapes=[pltpu.VMEM((tm, D_MODEL), F32)],
        compiler_params=_cparams("parallel", "arbitrary"),
        name="ffn",
    )(h2, x1, w_gu, w_gu, w_down)


def _pack_w_in(w_in):
    (a_q, a_k, a_v, a_iq, a_ik, a_iw, r_q, r_k, r_v, r_g, c_q, c_f, c_i, c_g,
     m_z, m_xbc, m_dt, gates) = jnp.split(w_in, SPLIT_POINTS, axis=-1)

    def pad_to_lanes(a):
        return jnp.pad(a, ((0, 0), (0, 0), (0, LANES - a.shape[-1])))

    ikw = pad_to_lanes(jnp.concatenate([a_ik, a_iw], axis=-1))
    packed = jnp.concatenate([gates, m_xbc, a_q, a_iq, r_q, r_k, r_v, r_g, c_q, c_f, c_i, c_g, m_z,
                              a_k, a_v, ikw, pad_to_lanes(m_dt)], axis=-1)
    return packed.astype(BF16)


def _rotary_tables(pos):
    half = R_DK // 2
    inv_freq = ROPE_BASE ** (-jnp.arange(half, dtype=F32) / half)
    ang = pos.astype(F32)[:, None] * inv_freq[None, :]
    c, s = jnp.cos(ang), jnp.sin(ang)
    return jnp.concatenate([c, c], axis=-1), jnp.concatenate([-s, s], axis=-1)


def _lower_bounds(lb_param):
    pr = jax.nn.softmax(lb_param.astype(F32), axis=0)
    return jnp.cumsum(pr, axis=0) - pr[0]


def _pad_lanes_row(v):
    return jnp.pad(v, (0, LANES - v.shape[0]))[None, :]


def kernel(x_prompt, x_sample, cache_k, cache_v, cache_kidx, state_ret, state_hgrn, state_ssm, state_conv,
           page_table, norm1_g, w_in, q_norm_g, k_norm_g, lb_param, hgrn_norm_g, A_log, dt_bias, D_skip,
           conv_w, conv_b, ssm_norm_g, w_branch, w_out, norm2_g, w_gu, w_down):
    bp, tp, _ = x_prompt.shape
    bs, ts, _ = x_sample.shape
    depth = w_in.shape[0]
    n_pool = cache_k.shape[1]
    n_pages = page_table.shape[1]
    kvw = A_KV_HEADS * A_HEAD_DIM
    tsp = SUBLANES
    npair = M_HEADS // 2
    gs = math.gcd(bs, 8)

    w_in_p = _pack_w_in(w_in)
    w_branch_b = w_branch.astype(BF16)
    w_out_b = w_out.astype(BF16)
    w_gu_b = w_gu.astype(BF16)
    w_down_b = w_down.astype(BF16)
    lbs = _lower_bounds(lb_param)
    cos_p, sin_p = _rotary_tables(jnp.arange(tp, dtype=jnp.int32))
    pos_s = n_pages * PAGE_SIZE + jnp.arange(tsp, dtype=jnp.int32)
    cos_s, sin_s = _rotary_tables(pos_s)
    ck_t = jnp.transpose(cache_k, (0, 1, 3, 4, 2)).reshape(depth, n_pool, kvw, PAGE_SIZE)
    cv_t = jnp.transpose(cache_v, (0, 1, 3, 4, 2)).reshape(depth, n_pool, kvw, PAGE_SIZE)
    ckidx_t = jnp.transpose(cache_kidx, (0, 1, 3, 2))
    head_group = (jnp.arange(A_HEADS) // (A_HEADS // A_KV_HEADS))[:, None] == jnp.arange(A_KV_HEADS)[None, :]
    ssm0 = state_ssm.reshape(depth, bs, npair, 2 * M_HEAD_DIM, M_STATE)
    buf0 = jnp.pad(state_conv, ((0, 0), (0, 0), (SUBLANES - (M_CONV - 1), 0), (0, 0)))

    zero_ret = jnp.zeros((bp, R_HEADS, R_DK, R_DV), F32)
    zero_hgrn = jnp.zeros((bp, H_HEADS, H_DK, H_DV), F32)
    zero_ssm = jnp.zeros((bp, npair, 2 * M_HEAD_DIM, M_STATE), F32)
    zero_buf = jnp.zeros((bp, SUBLANES, M_CONV_DIM), F32)

    xp = x_prompt.reshape(bp * tp, D_MODEL)
    xs = x_sample.reshape(bs * ts, D_MODEL)
    outs_p = [[] for _ in range(7)]
    outs_s = [[] for _ in range(7)]
    for l in range(depth):
        n1 = norm1_g[l][None, :]
        n2 = norm2_g[l][None, :]
        qg = jnp.tile(q_norm_g[l], 2)[None, :]
        kg = jnp.tile(k_norm_g[l], 2)[None, :]
        lb = lbs[l][None, :]
        hng = hgrn_norm_g[l][None, :]
        alog = _pad_lanes_row(A_log[l])
        dtb = _pad_lanes_row(dt_bias[l])
        dsk = jnp.repeat(D_skip[l], M_HEAD_DIM)[None, :]
        sng = ssm_norm_g[l][None, :]
        cw = conv_w[l]
        cb = conv_b[l][None, :]

        p = _inproj(xp, n1, w_in_p, l, 1024, 1280, BF16)
        kn, vv, ik, knb, vt, ik2 = _kprep(p, kg, 512)
        oa = _attn_prompt(p, knb, vt, ik2, qg, bp, tp, 256, 256)
        ob, ret_new = _retention(p, cos_p, sin_p, zero_ret, bp, tp, 512, R_CHUNK, R_CHUNK)
        oc, hgrn_new = _hgrn2(p, lb, hng, zero_hgrn, bp, tp, 512, H_CHUNK, H_CHUNK, 16)
        od, ssm_new, buf_new = _ssd(p, cw, cb, alog, dtb, dsk, sng, zero_ssm, zero_buf, bp, tp, 512, M_CHUNK, M_CHUNK)
        x1, h2 = _merge(oa, ob, oc, od, p, xp, w_branch_b, w_out_b, n2, l, 512)
        xp = _ffn(h2, x1, w_gu_b, w_down_b, l, 512, D_FF // 2)
        for i, a in enumerate((kn.reshape(bp, tp, A_KV_HEADS, A_HEAD_DIM), vv.reshape(bp, tp, A_KV_HEADS, A_HEAD_DIM),
                               ik.reshape(bp, tp, IDX_DIM), ret_new, hgrn_new,
                               ssm_new.reshape(bp, M_HEADS, M_HEAD_DIM, M_STATE),
                               buf_new[:, SUBLANES - (M_CONV - 1):, :])):
            outs_p[i].append(a)

        ps = _inproj(xs, n1, w_in_p, l, bs * ts, 1280, F32)
        qn_s, kn_s, iq_s, iw_s = _sprep(ps, qg, kg)
        v_s = ps[:, C_AV:C_AV + kvw]
        ik_s = ps[:, C_IKW:C_IKW + IDX_DIM]
        pad_q = lambda a: jnp.pad(a, ((0, 0), (0, tsp - ts)) + ((0, 0),) * (a.ndim - 2))
        new_keys_t = lambda a: jnp.swapaxes(jnp.pad(a.reshape(bs, ts, -1), ((0, 0), (0, PAGE_SIZE - ts), (0, 0))), 1, 2)
        spg = tsp // ts
        head_major = lambda a: jnp.swapaxes(a.reshape(bs // spg, tsp, IDX_HEADS, -1), 1, 2).reshape(bs // spg, IDX_HEADS * tsp, -1)
        iq_rows = head_major(iq_s)
        iw_col = head_major(iw_s[:, IDX_DIM:IDX_DIM + IDX_HEADS])
        mask = _idx_sample(page_table, iq_rows, iw_col, new_keys_t(ik_s).astype(BF16), ckidx_t, l, min(32, n_pages), ts)
        q4 = qn_s.reshape(bs, ts, A_HEADS, 1, A_HEAD_DIM)
        q_rows = jnp.where(head_group[None, None, :, :, None], q4, jnp.zeros_like(q4)).reshape(bs, ts * A_HEADS, kvw)
        oa_s = _attn_sample(page_table, q_rows, mask, new_keys_t(kn_s).astype(BF16), new_keys_t(v_s).astype(BF16),
                            ck_t, cv_t, l, min(64, n_pages), ts)
        oa_s = oa_s.reshape(bs * ts, W).astype(BF16)
        psp = pad_q(ps.reshape(bs, ts, NP)).reshape(bs * tsp, NP)
        ob_s, ret_s = _retention(psp, cos_s, sin_s, state_ret, bs, tsp, tsp, tsp, ts, gs, l)
        oc_s, hgrn_s = _hgrn2(psp, lb, hng, state_hgrn, bs, tsp, tsp, tsp, ts, tsp, gs, l)
        od_s, ssm_s, buf_s = _ssd(psp, cw, cb, alog, dtb, dsk, sng, ssm0, buf0, bs, tsp, tsp, tsp, ts, gs, l)
        unpad = lambda a: a.reshape(bs, tsp, W)[:, :ts].reshape(bs * ts, W)
        x1s, h2s = _merge(oa_s, unpad(ob_s), unpad(oc_s), unpad(od_s), ps, xs, w_branch_b, w_out_b, n2, l, bs * ts)
        xs = _ffn(h2s, x1s, w_gu_b, w_down_b, l, bs * ts, D_FF // 2)
        for i, a in enumerate((kn_s.reshape(bs, ts, A_KV_HEADS, A_HEAD_DIM), v_s.reshape(bs, ts, A_KV_HEADS, A_HEAD_DIM),
                               ik_s.reshape(bs, ts, IDX_DIM), ret_s, hgrn_s,
                               ssm_s.reshape(bs, M_HEADS, M_HEAD_DIM, M_STATE),
                               buf_s[:, SUBLANES - (M_CONV - 1):, :])):
            outs_s[i].append(a)

    dtypes = (cache_k.dtype, cache_v.dtype, cache_kidx.dtype, state_ret.dtype,
              state_hgrn.dtype, state_ssm.dtype, state_conv.dtype)
    res_p = [jnp.stack(a).astype(d) for a, d in zip(outs_p, dtypes)]
    res_s = [jnp.stack(a).astype(d) for a, d in zip(outs_s, dtypes)]
    return (xp.reshape(bp, tp, D_MODEL), xs.reshape(bs, ts, D_MODEL), *res_p, *res_s)
```

```python
import functools
import math
import struct

import jax
import jax.numpy as jnp
from jax import lax
from jax.experimental import pallas as pl
from jax.experimental.pallas import tpu as pltpu

F32 = jnp.float32
BF16 = jnp.bfloat16
I32 = jnp.int32

D_MODEL = 1024
PAGE_SIZE = 128
BRANCH_WIDTH = D_MODEL // 2
N_BRANCH = 4
A_HEAD_DIM = 64
A_HEADS = BRANCH_WIDTH // A_HEAD_DIM
A_KV_HEADS = A_HEADS // 2
IDX_HEADS = 8
IDX_DIM = 64
TOPK_MAX = 256
Q_BLOCK = 128
R_HEADS = 4
R_DK = BRANCH_WIDTH // R_HEADS
R_DV = BRANCH_WIDTH // R_HEADS
R_CHUNK = 128
ROPE_BASE = 10000.0
H_HEADS = 4
H_DK = 128
H_DV = BRANCH_WIDTH // H_HEADS
H_CHUNK = 64
MIN_FORGET = 1e-30
SAFE_EXP = 60.0
M_HEAD_DIM = 64
M_HEADS = BRANCH_WIDTH // M_HEAD_DIM
M_INNER = M_HEADS * M_HEAD_DIM
M_GROUPS = 2
M_STATE = 128
M_CONV = 4
M_CHUNK = 128
M_CONV_DIM = M_INNER + 2 * M_GROUPS * M_STATE
D_FF = -(-8 * D_MODEL // (3 * 256)) * 256
RMS_EPS = 1e-6
GN_EPS = 1e-6
MASK_VALUE = -1e30

SPLIT_SIZES = (
    A_HEADS * A_HEAD_DIM, A_KV_HEADS * A_HEAD_DIM, A_KV_HEADS * A_HEAD_DIM,
    IDX_HEADS * IDX_DIM, IDX_DIM, IDX_HEADS,
    R_HEADS * R_DK, R_HEADS * R_DK, R_HEADS * R_DV, R_HEADS * R_DV,
    H_HEADS * H_DK, H_HEADS * H_DK, H_HEADS * H_DV, H_HEADS * H_DV,
    M_INNER, M_CONV_DIM, M_HEADS,
    N_BRANCH * D_MODEL,
)
SPLIT_POINTS = tuple(sum(SPLIT_SIZES[:i + 1]) for i in range(len(SPLIT_SIZES) - 1))

LANES = 128
SUBLANES = 8
VMEM_LIMIT = 56 * 1024 * 1024

C_GATES = 0
C_XBC = 4096
C_AQ = 5120
C_IQ = 5632
C_RQ = 6144
C_RK = 6656
C_RV = 7168
C_RG = 7680
C_CQ = 8192
C_CF = 8704
C_CI = 9216
C_CG = 9728
C_MZ = 10240
C_AK = 10752
C_AV = 11008
C_IKW = 11264
C_DT = 11392
NP = 11520
W = BRANCH_WIDTH


def _float_key(x):
    b = struct.unpack("<i", struct.pack("<f", x))[0]
    return b if b >= 0 else b ^ 0x7FFFFFFF


VALID_KEY = _float_key(0.5 * MASK_VALUE)
INT_MIN = -2 ** 31
LOG2_E = math.log2(math.e)


def _cparams(*sem):
    return pltpu.CompilerParams(dimension_semantics=sem, vmem_limit_bytes=VMEM_LIMIT)


def _nt(a, b):
    return lax.dot_general(a, b, (((1,), (1,)), ((), ())), preferred_element_type=F32)


def _tn(a, b):
    return lax.dot_general(a, b, (((0,), (0,)), ((), ())), preferred_element_type=F32)


def _dot(a, b):
    return jnp.dot(a, b, preferred_element_type=F32)


def _dot_exact_lhs(a01, x):
    a = a01.astype(BF16)
    x0 = x.astype(BF16)
    r1 = x - x0.astype(F32)
    x1 = r1.astype(BF16)
    x2 = (r1 - x1.astype(F32)).astype(BF16)
    return _dot(a, x0) + _dot(a, x1) + _dot(a, x2)


def _dot_exact_rhs(x, b01):
    b = b01.astype(BF16)
    x0 = x.astype(BF16)
    r1 = x - x0.astype(F32)
    x1 = r1.astype(BF16)
    x2 = (r1 - x1.astype(F32)).astype(BF16)
    return _dot(x0, b) + _dot(x1, b) + _dot(x2, b)


def _sigmoid(x):
    return 1.0 / (1.0 + jnp.exp(-x))


def _silu(x):
    return x * _sigmoid(x)


def _iota(shape, dim):
    return lax.broadcasted_iota(I32, shape, dim)


def _tri_incl(n):
    return (_iota((n, n), 0) >= _iota((n, n), 1)).astype(F32)


def _inproj_kernel(x_ref, g_ref, w_ref, o_ref, h_ref):
    @pl.when(pl.program_id(1) == 0)
    def _():
        x = x_ref[...]
        ms = jnp.mean(x * x, axis=-1, keepdims=True)
        h_ref[...] = (x * lax.rsqrt(ms + RMS_EPS) * g_ref[...]).astype(BF16)

    o_ref[...] = _dot(h_ref[...], w_ref[...]).astype(o_ref.dtype)


def _inproj(x, g, w, layer, tm, tn, out_dtype):
    n = x.shape[0]
    return pl.pallas_call(
        _inproj_kernel,
        grid=(n // tm, NP // tn),
        in_specs=[pl.BlockSpec((tm, D_MODEL), lambda i, j: (i, 0)),
                  pl.BlockSpec((1, D_MODEL), lambda i, j: (0, 0)),
                  pl.BlockSpec((None, D_MODEL, tn), lambda i, j: (layer, 0, j))],
        out_specs=pl.BlockSpec((tm, tn), lambda i, j: (i, j)),
        out_shape=jax.ShapeDtypeStruct((n, NP), out_dtype),
        scratch_shapes=[pltpu.VMEM((tm, D_MODEL), BF16)],
        compiler_params=_cparams("parallel", "arbitrary"),
        name="inproj",
    )(x, g, w)


def _head_rmsnorm(x, gain_row):
    pair = (_iota((LANES, LANES), 0) // A_HEAD_DIM == _iota((LANES, LANES), 1) // A_HEAD_DIM).astype(F32)
    outs = []
    for c in range(x.shape[1] // LANES):
        xc = x[:, c * LANES:(c + 1) * LANES]
        ms = _dot_exact_rhs(xc * xc, pair) * (1.0 / A_HEAD_DIM)
        outs.append(xc * lax.rsqrt(ms + RMS_EPS) * gain_row)
    return outs


def _kth_largest_key(count_ge, k, shape, n_keys):
    c0 = count_ge(jnp.zeros(shape, I32))
    nonneg = c0 >= k
    prefix = jnp.where(nonneg, 0, INT_MIN).astype(I32)
    c_prefix = jnp.where(nonneg, c0, n_keys).astype(I32)

    def body(t, carry):
        prefix, c_prefix = carry
        cand = prefix + jnp.left_shift(jnp.int32(1), 30 - t)
        c = count_ge(cand)
        take = c >= k
        return jnp.where(take, cand, prefix), jnp.where(take, c, c_prefix)

    return lax.fori_loop(0, 31, body, (prefix, c_prefix))


def _to_key(s):
    b = lax.bitcast_convert_type(s, I32)
    return jnp.where(b >= 0, b, b ^ 0x7FFFFFFF)


def _kprep_kernel(ak_ref, av_ref, ikw_ref, kg_ref, kn_ref, v_ref, ik_ref, knb_ref, vt_ref, ik2_ref):
    kn = _head_rmsnorm(ak_ref[...].astype(F32), kg_ref[...])
    for c, t in enumerate(kn):
        kn_ref[:, c * LANES:(c + 1) * LANES] = t
        knb_ref[:, c * LANES:(c + 1) * LANES] = t.astype(BF16)
    v = av_ref[...].astype(F32)
    v_ref[...] = v
    vt_ref[...] = v.T.astype(BF16)
    ikw = ikw_ref[...].astype(F32)
    ik_ref[...] = ikw[:, :IDX_DIM]
    lane = _iota(ikw.shape, 1)
    ik2_ref[...] = jnp.where(lane < IDX_DIM, ikw, pltpu.roll(ikw, IDX_DIM, 1)).astype(BF16)


def _kprep(p, k_gain, tm):
    n = p.shape[0]
    kvw = A_KV_HEADS * A_HEAD_DIM
    row = lambda wdt: pl.BlockSpec((tm, wdt), lambda i: (i, 0))
    return pl.pallas_call(
        _kprep_kernel,
        grid=(n // tm,),
        in_specs=[pl.BlockSpec((tm, kvw), lambda i: (i, C_AK // kvw)),
                  pl.BlockSpec((tm, kvw), lambda i: (i, C_AV // kvw)),
                  pl.BlockSpec((tm, LANES), lambda i: (i, C_IKW // LANES)),
                  pl.BlockSpec((1, LANES), lambda i: (0, 0))],
        out_specs=[row(kvw), row(kvw), row(IDX_DIM), row(kvw), pl.BlockSpec((kvw, tm), lambda i: (0, i)), row(LANES)],
        out_shape=[jax.ShapeDtypeStruct((n, kvw), F32),
                   jax.ShapeDtypeStruct((n, kvw), F32),
                   jax.ShapeDtypeStruct((n, IDX_DIM), F32),
                   jax.ShapeDtypeStruct((n, kvw), BF16),
                   jax.ShapeDtypeStruct((kvw, n), BF16),
                   jax.ShapeDtypeStruct((n, LANES), BF16)],
        compiler_params=_cparams("parallel"),
        name="kprep",
    )(p, p, p, k_gain)


def _attn_prompt_kernel(aq_ref, iq_ref, ikw_ref, qg_ref, knb_ref, vt_ref, ik2_ref, o_ref,
                        key_ref, msk_ref, s_ref, qt_ref, iqt_ref, acc_ref, cut_ref, *, topk, kb, qb):
    i = pl.program_id(1)
    nblk = (i * qb + qb + kb - 1) // kb
    row = _iota((kb, qb), 0)
    lane = _iota((kb, qb), 1)
    lane_q = _iota((qb, LANES), 1)
    own_half = [lane_q < A_HEAD_DIM, lane_q >= A_HEAD_DIM]
    n_pairs = A_KV_HEADS // 2
    hpp = A_HEADS // n_pairs
    nsub = kb // SUBLANES

    qn = _head_rmsnorm(aq_ref[...].astype(F32), qg_ref[...])
    for h in range(A_HEADS):
        src, dst = h % 2, (h // 2) % 2
        t = qn[h // 2] * (A_HEAD_DIM ** -0.5 * LOG2_E)
        if src != dst:
            t = pltpu.roll(t, A_HEAD_DIM, 1)
        t = jnp.where(own_half[dst], t, 0.0)
        qt_ref[h // hpp, :, (h % hpp) * qb:(h % hpp + 1) * qb] = t.T.astype(BF16)
    for h in range(IDX_HEADS):
        t = iq_ref[:, (h // 2) * LANES:(h // 2 + 1) * LANES].astype(F32) * (IDX_DIM ** -0.5)
        iqt_ref[:, h * qb:(h + 1) * qb] = jnp.where(own_half[h % 2], t, 0.0).T.astype(BF16)
    iw_t = ikw_ref[...].astype(F32).T[IDX_DIM:IDX_DIM + IDX_HEADS, :] * (IDX_HEADS ** -0.5)

    def score_blk(j, carry):
        ks = pl.multiple_of(j * kb, kb)
        lg = _dot(ik2_ref[pl.ds(ks, kb), :], iqt_ref[...])
        acc = jnp.zeros((kb, qb), F32)
        for h in range(IDX_HEADS):
            acc = acc + jnp.maximum(lg[:, h * qb:(h + 1) * qb], 0.0) * iw_t[h:h + 1, :]
        adm = (ks + row) <= (i * qb + lane)
        key_ref[j] = _to_key(jnp.where(adm, acc, MASK_VALUE))
        return carry

    lax.fori_loop(0, nblk, score_blk, 0)

    def count(pred):
        def body(j, c):
            return c + jnp.sum(jnp.where(pred(key_ref[j], j), 1, 0).reshape(nsub, SUBLANES, qb), axis=0)
        c = lax.fori_loop(0, nblk, body, jnp.zeros((SUBLANES, qb), I32))
        return jnp.sum(c, axis=0, keepdims=True)

    cut_ref[0:1, :] = jnp.full((1, qb), INT_MIN, I32)
    cut_ref[1:2, :] = jnp.full((1, qb), 2 ** 30, I32)

    @pl.when(i * qb + qb > topk)
    def _():
        thr, n_ge = _kth_largest_key(lambda cand: count(lambda blk, j: blk >= cand), topk, (1, qb), nblk * kb)
        cut_ref[0:1, :] = thr

        @pl.when(jnp.max(n_ge) > topk)
        def _():
            need = topk - count(lambda blk, j: blk > thr)

            def body(t, ans):
                cand = ans + jnp.left_shift(jnp.int32(1), 15 - t)
                upto = count(lambda blk, j: (blk == thr) & ((j * kb + row) <= cand - 1))
                return jnp.where(upto < need, cand, ans)

            cut_ref[1:2, :] = lax.fori_loop(0, 16, body, jnp.zeros((1, qb), I32))

    thr = cut_ref[0:1, :]
    last = cut_ref[1:2, :]
    lo = jnp.maximum(thr, VALID_KEY)
    tie_ok = thr > VALID_KEY

    def mask_blk(j, carry):
        blk = key_ref[j]
        sel = (blk > lo) | ((blk == thr) & tie_ok & ((j * kb + row) <= last))
        msk_ref[j] = jnp.where(sel, 0.0, MASK_VALUE)
        return carry

    lax.fori_loop(0, nblk, mask_blk, 0)

    def fold(t, op):
        return op(t.reshape(nsub, SUBLANES, qb), axis=0)

    def logits_blk(j, m8):
        ks = pl.multiple_of(j * kb, kb)
        mk = msk_ref[j]
        out = []
        for pr in range(n_pairs):
            s4 = _dot(knb_ref[pl.ds(ks, kb), pr * LANES:(pr + 1) * LANES], qt_ref[pr])
            for hh in range(hpp):
                h = pr * hpp + hh
                s = s4[:, hh * qb:(hh + 1) * qb] + mk
                s_ref[j, h] = s
                out.append(jnp.maximum(m8[h], fold(s, jnp.max)))
        return tuple(out)

    m8 = lax.fori_loop(0, nblk, logits_blk, tuple(jnp.full((SUBLANES, qb), MASK_VALUE, F32) for _ in range(A_HEADS)))
    m_row = [jnp.max(t, axis=0, keepdims=True) for t in m8]

    acc_ref[...] = jnp.zeros(acc_ref.shape, F32)

    def values_blk(j, l8):
        ks = pl.multiple_of(j * kb, kb)
        out = []
        for h in range(A_HEADS):
            g = h // (A_HEADS // A_KV_HEADS)
            e = jnp.exp2(s_ref[j, h] - m_row[h])
            out.append(l8[h] + fold(e, jnp.sum))
            vt = vt_ref[g * A_HEAD_DIM:(g + 1) * A_HEAD_DIM, pl.ds(ks, kb)]
            acc_ref[h] = acc_ref[h] + _dot(vt, e.astype(BF16))
        return tuple(out)

    l8 = lax.fori_loop(0, nblk, values_blk, tuple(jnp.zeros((SUBLANES, qb), F32) for _ in range(A_HEADS)))

    for h in range(A_HEADS):
        l_row = jnp.sum(l8[h], axis=0, keepdims=True)
        o_ref[:, h * A_HEAD_DIM:(h + 1) * A_HEAD_DIM] = (acc_ref[h] / l_row).T.astype(o_ref.dtype)


def _attn_prompt(p, knb, vt, ik2, q_gain, batch, seq, kb, qb):
    nqb = seq // qb
    nkb = seq // kb
    kvw = A_KV_HEADS * A_HEAD_DIM
    topk = min(TOPK_MAX, seq // 4)
    n_pairs = A_KV_HEADS // 2
    hpp = A_HEADS // n_pairs
    return pl.pallas_call(
        functools.partial(_attn_prompt_kernel, topk=topk, kb=kb, qb=qb),
        grid=(batch, nqb),
        in_specs=[pl.BlockSpec((qb, W), lambda b, i: (b * nqb + i, C_AQ // W)),
                  pl.BlockSpec((qb, W), lambda b, i: (b * nqb + i, C_IQ // W)),
                  pl.BlockSpec((qb, LANES), lambda b, i: (b * nqb + i, C_IKW // LANES)),
                  pl.BlockSpec((1, LANES), lambda b, i: (0, 0)),
                  pl.BlockSpec((seq, kvw), lambda b, i: (b, 0)),
                  pl.BlockSpec((kvw, seq), lambda b, i: (0, b)),
                  pl.BlockSpec((seq, LANES), lambda b, i: (b, 0))],
        out_specs=pl.BlockSpec((qb, W), lambda b, i: (b * nqb + i, 0)),
        out_shape=jax.ShapeDtypeStruct((batch * seq, W), BF16),
        scratch_shapes=[pltpu.VMEM((nkb, kb, qb), I32),
                        pltpu.VMEM((nkb, kb, qb), F32),
                        pltpu.VMEM((nkb, A_HEADS, kb, qb), F32),
                        pltpu.VMEM((n_pairs, LANES, hpp * qb), BF16),
                        pltpu.VMEM((LANES, IDX_HEADS * qb), BF16),
                        pltpu.VMEM((A_HEADS, A_HEAD_DIM, qb), F32),
                        pltpu.VMEM((SUBLANES, qb), I32)],
        compiler_params=_cparams("parallel", "arbitrary"),
        name="attn_prompt",
    )(p, p, p, q_gain, knb, vt, ik2)


def _ret_body(q_ref, k_ref, v_ref, g_ref, cos_ref, sin_ref, s0_ref, o_ref, s_out_ref, st_ref, cp, c_true):
    tb = pl.program_id(1)

    @pl.when(tb == 0)
    def _():
        st_ref[...] = s0_ref[...]

    row = _iota((cp, 1), 0).astype(F32)
    diff = (_iota((cp, cp), 0) - _iota((cp, cp), 1)).astype(F32)

    def chunk(c, carry):
        r0 = pl.multiple_of(c * cp, cp)
        rows = pl.ds(r0, cp)
        cos = cos_ref[rows, :]
        sin = sin_ref[rows, :]
        for h in range(R_HEADS):
            lg = math.log1p(-2.0 ** (-5.0 - h))
            cols = slice(h * R_DK, (h + 1) * R_DK)
            q = q_ref[rows, cols].astype(F32)
            k = k_ref[rows, cols].astype(F32)
            v = v_ref[rows, cols].astype(F32)
            q = q * cos + pltpu.roll(q, R_DK // 2, 1) * sin
            k = (k * cos + pltpu.roll(k, R_DK // 2, 1) * sin) * (R_DK ** -0.5)
            decay = jnp.where(diff >= 0, jnp.exp(lg * jnp.maximum(diff, 0.0)), 0.0)
            scores = _nt(q.astype(BF16), k.astype(BF16)) * decay
            s = st_ref[h]
            o = _dot(scores.astype(BF16), v.astype(BF16))
            o = o + _dot((q * jnp.exp(lg * (row + 1.0))).astype(BF16), s.astype(BF16))
            kd = jnp.where(row < c_true, jnp.exp(lg * jnp.maximum(c_true - 1.0 - row, 0.0)), 0.0)
            st_ref[h] = math.exp(lg * c_true) * s + _tn((k * kd).astype(BF16), v.astype(BF16))
            mu = jnp.mean(o, axis=-1, keepdims=True)
            d = o - mu
            var = jnp.mean(d * d, axis=-1, keepdims=True)
            o_ref[rows, cols] = (_silu(g_ref[rows, cols].astype(F32)) * (d * lax.rsqrt(var + GN_EPS))).astype(o_ref.dtype)
        return carry

    n_chunks = q_ref.shape[0] // cp
    lax.fori_loop(0, n_chunks, chunk, 0, unroll=2 if n_chunks % 2 == 0 else 1)

    @pl.when(tb == pl.num_programs(1) - 1)
    def _():
        s_out_ref[...] = st_ref[...]


def _seq_views(gseq, row_refs, seq_refs):
    rows = row_refs[0].shape[0] // gseq
    for g in range(gseq):
        yield [r.at[pl.ds(g * rows, rows)] for r in row_refs], [r.at[g] for r in seq_refs]


def _ret_kernel(q_ref, k_ref, v_ref, g_ref, cos_ref, sin_ref, s0_ref, o_ref, s_out_ref, st_ref,
                *, cp, c_true, gseq):
    for (q, k, v, g, o), (s0, s_out, st) in _seq_views(gseq, (q_ref, k_ref, v_ref, g_ref, o_ref), (s0_ref, s_out_ref, st_ref)):
        _ret_body(q, k, v, g, cos_ref, sin_ref, s0, o, s_out, st, cp, c_true)


def _hgrn_kernel(q_ref, f_ref, i_ref, g_ref, lb_ref, ng_ref, s0_ref, o_ref, s_out_ref,
                 st_ref, qs_ref, ks_ref, gs_ref, *, cp, c_true, sb, gseq):
    for (q, f, i, g, o, qs, ks, gs), (s0, s_out, st) in _seq_views(
            gseq, (q_ref, f_ref, i_ref, g_ref, o_ref, qs_ref, ks_ref, gs_ref), (s0_ref, s_out_ref, st_ref)):
        _hgrn_body(q, f, i, g, lb_ref, ng_ref, s0, o, s_out, st, qs, ks, gs, cp, c_true, sb)


def _hgrn_body(q_ref, f_ref, i_ref, g_ref, lb_ref, ng_ref, s0_ref, o_ref, s_out_ref,
               st_ref, qs_ref, ks_ref, gs_ref, cp, c_true, sb):
    tb = pl.program_id(1)

    @pl.when(tb == 0)
    def _():
        st_ref[...] = s0_ref[...]

    tri = _tri_incl(cp)
    rowi = _iota((cp, 1), 0)
    real = rowi < c_true
    sub_n = _iota((sb, 1), 0)
    ng = ng_ref[...]
    n_chunks = q_ref.shape[0] // cp
    unroll = 4 if n_chunks % 4 == 0 else (2 if n_chunks % 2 == 0 else 1)
    n_sub = cp // sb

    def prepare(c, spread):
        rows = pl.ds(pl.multiple_of(c * cp, cp), cp)
        for h in range(H_HEADS):
            cols = slice(h * H_DK, (h + 1) * H_DK)
            lb = lb_ref[:, cols]
            f = f_ref[rows, cols].astype(F32)
            forget = lb + (1.0 - lb) * _sigmoid(f)
            logf = jnp.where(real, jnp.log(jnp.maximum(forget, MIN_FORGET)), 0.0)
            gc = _dot_exact_lhs(tri, logf)
            qs_ref[rows, cols] = _silu(q_ref[rows, cols].astype(F32)) * (H_DK ** -0.5)
            ks_ref[rows, cols] = jnp.where(real, (1.0 - lb) * _sigmoid(-f), 0.0)
            gs_ref[rows, cols] = gc
            for b in range(n_sub):
                spread = jnp.maximum(spread, gc[b * sb:b * sb + 1, :] - gc[b * sb + sb - 1:b * sb + sb, :])
        return spread

    spread = lax.fori_loop(0, n_chunks, prepare, jnp.zeros((1, H_DK), F32), unroll=unroll)
    safe = jnp.max(spread) < SAFE_EXP

    blk_of_row = rowi // sb
    causal = _iota((cp, cp), 0) >= _iota((cp, cp), 1)

    def intra_factored(q, k, gc):
        gref_rows = jnp.concatenate([jnp.broadcast_to(gc[b * sb:b * sb + 1, :], (sb, H_DK)) for b in range(n_sub)], axis=0)
        qd = q * jnp.exp(gc - gref_rows)
        lhs, rhs = [], []
        for b in range(n_sub):
            lhs.append(jnp.where(blk_of_row == b, qd, 0.0).astype(BF16))
            kd = k * jnp.exp(jnp.minimum(gc[b * sb:b * sb + 1, :] - gc, SAFE_EXP))
            rhs.append(jnp.where(rowi < (b + 1) * sb, kd, 0.0).astype(BF16))
        cat = (lambda t: t[0]) if n_sub == 1 else (lambda t: jnp.concatenate(t, axis=1))
        return jnp.where(causal, _nt(cat(lhs), cat(rhs)), 0.0)

    def intra_by_column(q, k, gc):
        rows_out = []
        for b in range(n_sub):
            b0 = b * sb
            gi = gc[b0:b0 + sb, :]
            qi = q[b0:b0 + sb, :]
            ki = k[b0:b0 + sb, :]
            lane_a = _iota((sb, cp), 1)
            if b0 > 0:
                gref = gc[b0:b0 + 1, :]
                kd = jnp.where(rowi < b0, k * jnp.exp(jnp.minimum(gref - gc, 0.0)), 0.0)
                a = _nt((qi * jnp.exp(gi - gref)).astype(BF16), kd.astype(BF16))
            else:
                a = jnp.zeros((sb, cp), F32)
            for m in range(sb):
                e = jnp.exp(jnp.where(sub_n >= m, gi - gi[m:m + 1, :], MASK_VALUE))
                col = jnp.sum(qi * e * ki[m:m + 1, :], axis=1, keepdims=True)
                a = jnp.where(lane_a == b0 + m, col, a)
            rows_out.append(a)
        return rows_out[0] if n_sub == 1 else jnp.concatenate(rows_out, axis=0)

    def make_chunk(intra):
        def chunk(c, carry):
            rows = pl.ds(pl.multiple_of(c * cp, cp), cp)
            for h in range(H_HEADS):
                cols = slice(h * H_DK, (h + 1) * H_DK)
                q = qs_ref[rows, cols]
                k = ks_ref[rows, cols]
                gc = gs_ref[rows, cols]
                vb = jnp.where(real, i_ref[rows, cols].astype(F32), 0.0).astype(BF16)
                sbf = st_ref[h].astype(BF16)
                qe = (q * jnp.exp(gc)).astype(BF16)
                am = intra(q, k, gc).astype(BF16)
                if cp % LANES == 0 or cp * 2 == LANES:
                    o = _dot(jnp.concatenate([qe, am], axis=1), jnp.concatenate([sbf, vb], axis=0))
                else:
                    o = _dot(qe, sbf) + _dot(am, vb)
                g_last = gc[cp - 1:cp, :]
                kd = k * jnp.exp(g_last - gc)
                eg_col = jnp.broadcast_to(jnp.exp(g_last), (SUBLANES, H_DK)).T[:, 0:1]
                st_ref[h] = eg_col * st_ref[h] + _tn(kd.astype(BF16), vb)
                ms = jnp.mean(o * o, axis=-1, keepdims=True)
                o = o * lax.rsqrt(ms + RMS_EPS) * ng
                o_ref[rows, cols] = (_silu(g_ref[rows, cols].astype(F32)) * o).astype(o_ref.dtype)
            return carry
        return chunk

    @pl.when(safe)
    def _():
        lax.fori_loop(0, n_chunks, make_chunk(intra_factored), 0, unroll=unroll)

    @pl.when(jnp.logical_not(safe))
    def _():
        lax.fori_loop(0, n_chunks, make_chunk(intra_by_column), 0)

    @pl.when(tb == pl.num_programs(1) - 1)
    def _():
        s_out_ref[...] = st_ref[...]


def _softplus(x):
    return jnp.maximum(x, 0.0) + jnp.log1p(jnp.exp(-jnp.abs(x)))


def _ssd_kernel(z_ref, xbc_ref, dt_ref, cw_ref, cb_ref, alog_ref, dtb_ref, dskip_ref, ng_ref,
                h0_ref, buf0_ref, o_ref, h_out_ref, buf_out_ref, st_ref, tail_ref, ext_ref,
                *, cp, c_true, gseq):
    for (z, xbc, dt, o), (h0, buf0, h_out, buf_out, st, tail, ext) in _seq_views(
            gseq, (z_ref, xbc_ref, dt_ref, o_ref), (h0_ref, buf0_ref, h_out_ref, buf_out_ref, st_ref, tail_ref, ext_ref)):
        _ssd_body(z, xbc, dt, cw_ref, cb_ref, alog_ref, dtb_ref, dskip_ref, ng_ref, h0, buf0, o, h_out, buf_out,
                  st, tail, ext, cp, c_true)


def _ssd_body(z_ref, xbc_ref, dt_ref, cw_ref, cb_ref, alog_ref, dtb_ref, dskip_ref, ng_ref,
              h0_ref, buf0_ref, o_ref, h_out_ref, buf_out_ref, st_ref, tail_ref, ext_ref, cp, c_true):
    tb = pl.program_id(1)

    @pl.when(tb == 0)
    def _():
        st_ref[...] = h0_ref[...]
        tail_ref[...] = buf0_ref[...]

    tri = _tri_incl(cp)
    rowi = _iota((cp, 1), 0)
    real = rowi < c_true
    causal = _iota((cp, cp), 0) >= _iota((cp, cp), 1)
    lane = _iota((1, LANES), 1)
    left = lane < M_HEAD_DIM
    top = _iota((LANES, 1), 0) < M_HEAD_DIM
    cw = cw_ref[...]
    cb = cb_ref[...]
    neg_a = -jnp.exp(alog_ref[...])
    gw = M_INNER // M_GROUPS

    def chunk(c, carry):
        r0 = pl.multiple_of(c * cp, cp)
        rows = pl.ds(r0, cp)
        u = xbc_ref[rows, :].astype(F32)
        ext_ref[0:SUBLANES, :] = tail_ref[...]
        ext_ref[SUBLANES:SUBLANES + cp, :] = u
        y = cb + cw[M_CONV - 1:M_CONV, :] * u
        for j in range(M_CONV - 1):
            y = y + cw[j:j + 1, :] * ext_ref[pl.ds(SUBLANES - (M_CONV - 1) + j, cp), :]
        tail_ref[...] = ext_ref[pl.ds(c_true, SUBLANES), :]
        xbc = _silu(y)
        dt = jnp.where(real, _softplus(dt_ref[rows, :].astype(F32) + dtb_ref[...]), 0.0)
        a = neg_a * dt
        cum = _dot_exact_lhs(tri, a)
        cum_t = cum.T
        dt_t = dt.T
        e_cum = jnp.exp(cum)
        cum_last = cum[cp - 1:cp, :]
        w_all = jnp.exp(cum_last - cum) * dt
        e_last = jnp.exp(cum_last)
        z = z_ref[rows, :].astype(F32)
        ys = []
        for pr in range(M_HEADS // 2):
            g = (2 * pr) // (M_HEADS // M_GROUPS)
            bm = xbc[:, M_INNER + g * M_STATE:M_INNER + (g + 1) * M_STATE].astype(BF16)
            cm = xbc[:, M_INNER + (M_GROUPS + g) * M_STATE:M_INNER + (M_GROUPS + g + 1) * M_STATE].astype(BF16)
            xp = xbc[:, pr * LANES:(pr + 1) * LANES]
            xpb = xp.astype(BF16)
            cbm = _nt(cm, bm)
            hp = st_ref[pr]
            cross = _nt(cm, hp.astype(BF16))
            intra = []
            for hh in range(2):
                h = 2 * pr + hh
                seg = jnp.exp(jnp.where(causal, cum[:, h:h + 1] - cum_t[h:h + 1, :], MASK_VALUE))
                mat = cbm * seg * dt_t[h:h + 1, :]
                intra.append(_dot(mat.astype(BF16), xpb))
            h0, h1 = 2 * pr, 2 * pr + 1
            y_pair = (jnp.where(left, intra[0], intra[1])
                      + cross * jnp.where(left, e_cum[:, h0:h0 + 1], e_cum[:, h1:h1 + 1]))
            wx = xp * jnp.where(left, w_all[:, h0:h0 + 1], w_all[:, h1:h1 + 1])
            decay = jnp.where(top, e_last[:, h0:h0 + 1], e_last[:, h1:h1 + 1])
            st_ref[pr] = decay * hp + _tn(wx.astype(BF16), bm)
            y_pair = y_pair + dskip_ref[:, pr * LANES:(pr + 1) * LANES] * xp
            ys.append(y_pair * _silu(z[:, pr * LANES:(pr + 1) * LANES]))
        for g in range(M_GROUPS):
            yg = jnp.concatenate(ys[g * 2:(g + 1) * 2], axis=1)
            ms = jnp.mean(yg * yg, axis=-1, keepdims=True)
            o_ref[rows, g * gw:(g + 1) * gw] = (yg * lax.rsqrt(ms + RMS_EPS)
                                                * ng_ref[:, g * gw:(g + 1) * gw]).astype(o_ref.dtype)
        return carry

    lax.fori_loop(0, z_ref.shape[0] // cp, chunk, 0)

    @pl.when(tb == pl.num_programs(1) - 1)
    def _():
        h_out_ref[...] = st_ref[...]
        buf_out_ref[...] = tail_ref[...]


def _row_blocks(batch, rows_per_seq, tb, gseq):
    n_tb = rows_per_seq // tb
    assert batch % gseq == 0 and (gseq == 1 or n_tb == 1), (batch, rows_per_seq, tb, gseq)
    return n_tb, (lambda col: (lambda b, t: (b * n_tb + t, col)))


def _state_specs(shape, gseq, layer):
    zeros = (0,) * len(shape)
    out = pl.BlockSpec((gseq, *shape), lambda b, t: (b, *zeros))
    if layer is None:
        return out, out
    return pl.BlockSpec((None, gseq, *shape), lambda b, t: (layer, b, *zeros)), out


def _retention(p, cos2, sin2, s0, batch, rows_per_seq, tb, cp, c_true, gseq=1, layer=None):
    n_tb, at = _row_blocks(batch, rows_per_seq, tb, gseq)
    st_in, st_spec = _state_specs((R_HEADS, R_DK, R_DV), gseq, layer)
    rb = gseq * tb
    return pl.pallas_call(
        functools.partial(_ret_kernel, cp=cp, c_true=c_true, gseq=gseq),
        grid=(batch // gseq, n_tb),
        in_specs=[pl.BlockSpec((rb, W), at(C_RQ // W)), pl.BlockSpec((rb, W), at(C_RK // W)),
                  pl.BlockSpec((rb, W), at(C_RV // W)), pl.BlockSpec((rb, W), at(C_RG // W)),
                  pl.BlockSpec((tb, R_DK), lambda b, t: (t, 0)), pl.BlockSpec((tb, R_DK), lambda b, t: (t, 0)),
                  st_in],
        out_specs=[pl.BlockSpec((rb, W), at(0)), st_spec],
        out_shape=[jax.ShapeDtypeStruct((batch * rows_per_seq, W), BF16),
                   jax.ShapeDtypeStruct((batch, R_HEADS, R_DK, R_DV), F32)],
        scratch_shapes=[pltpu.VMEM((gseq, R_HEADS, R_DK, R_DV), F32)],
        compiler_params=_cparams("parallel", "arbitrary"),
        name="retention",
    )(p, p, p, p, cos2, sin2, s0)


def _hgrn2(p, lb, norm_g, s0, batch, rows_per_seq, tb, cp, c_true, sb, gseq=1, layer=None):
    n_tb, at = _row_blocks(batch, rows_per_seq, tb, gseq)
    st_in, st_spec = _state_specs((H_HEADS, H_DK, H_DV), gseq, layer)
    rb = gseq * tb
    return pl.pallas_call(
        functools.partial(_hgrn_kernel, cp=cp, c_true=c_true, sb=sb, gseq=gseq),
        grid=(batch // gseq, n_tb),
        in_specs=[pl.BlockSpec((rb, W), at(C_CQ // W)), pl.BlockSpec((rb, W), at(C_CF // W)),
                  pl.BlockSpec((rb, W), at(C_CI // W)), pl.BlockSpec((rb, W), at(C_CG // W)),
                  pl.BlockSpec((1, W), lambda b, t: (0, 0)), pl.BlockSpec((1, H_DV), lambda b, t: (0, 0)),
                  st_in],
        out_specs=[pl.BlockSpec((rb, W), at(0)), st_spec],
        out_shape=[jax.ShapeDtypeStruct((batch * rows_per_seq, W), BF16),
                   jax.ShapeDtypeStruct((batch, H_HEADS, H_DK, H_DV), F32)],
        scratch_shapes=[pltpu.VMEM((gseq, H_HEADS, H_DK, H_DV), F32)] + [pltpu.VMEM((rb, W), F32)] * 3,
        compiler_params=_cparams("parallel", "arbitrary"),
        name="hgrn2",
    )(p, p, p, p, lb, norm_g, s0)


def _ssd(p, conv_w, conv_b, a_log, dt_bias, d_skip, norm_g, h0, buf0, batch, rows_per_seq, tb, cp, c_true, gseq=1,
         layer=None):
    n_tb, at = _row_blocks(batch, rows_per_seq, tb, gseq)
    rb = gseq * tb
    npair = M_HEADS // 2
    st_in, st_spec = _state_specs((npair, 2 * M_HEAD_DIM, M_STATE), gseq, layer)
    buf_in, buf_spec = _state_specs((SUBLANES, M_CONV_DIM), gseq, layer)
    row = lambda wdt: pl.BlockSpec((1, wdt), lambda b, t: (0, 0))
    return pl.pallas_call(
        functools.partial(_ssd_kernel, cp=cp, c_true=c_true, gseq=gseq),
        grid=(batch // gseq, n_tb),
        in_specs=[pl.BlockSpec((rb, W), at(C_MZ // W)), pl.BlockSpec((rb, M_CONV_DIM), at(C_XBC // M_CONV_DIM)),
                  pl.BlockSpec((rb, LANES), at(C_DT // LANES)),
                  pl.BlockSpec((M_CONV, M_CONV_DIM), lambda b, t: (0, 0)), row(M_CONV_DIM),
                  row(LANES), row(LANES), row(W), row(W), st_in, buf_in],
        out_specs=[pl.BlockSpec((rb, W), at(0)), st_spec, buf_spec],
        out_shape=[jax.ShapeDtypeStruct((batch * rows_per_seq, W), BF16),
                   jax.ShapeDtypeStruct((batch, npair, 2 * M_HEAD_DIM, M_STATE), F32),
                   jax.ShapeDtypeStruct((batch, SUBLANES, M_CONV_DIM), F32)],
        scratch_shapes=[pltpu.VMEM((gseq, npair, 2 * M_HEAD_DIM, M_STATE), F32),
                        pltpu.VMEM((gseq, SUBLANES, M_CONV_DIM), F32),
                        pltpu.VMEM((gseq, cp + SUBLANES, M_CONV_DIM), F32)],
        compiler_params=_cparams("parallel", "arbitrary"),
        name="ssd",
    )(p, p, p, conv_w, conv_b, a_log, dt_bias, d_skip, norm_g, h0, buf0)


def _idx_sample_kernel(pt_ref, iq_ref, iw_ref, iknew_ref, *rest, pp, n_pages, topk, t_new):
    spg = SUBLANES // t_new
    page_refs = rest[:spg * pp]
    mask_ref = rest[spg * pp]
    key_ref = rest[spg * pp + 1]
    cut_ref = rest[spg * pp + 2]
    del pt_ref
    b = pl.program_id(0)
    c = pl.program_id(1)
    qs = SUBLANES
    nrow = mask_ref.shape[1]
    iq = iq_ref[0]
    wcol = iw_ref[0]
    r0 = pl.multiple_of(b * qs, qs)
    row = _iota((qs, LANES), 0)
    lane_q = _iota((qs, LANES), 1)

    def scores(keys_t):
        lg = _dot(iq, keys_t)
        sc = (jnp.maximum(lg, 0.0) * wcol).reshape(IDX_HEADS, qs, LANES).sum(axis=0)
        return sc + 0.0

    def own_rows(per_seq):
        out = per_seq[0]
        for g in range(1, spg):
            out = jnp.where(row >= g * t_new, per_seq[g], out)
        return out

    for j in range(pp):
        sc = own_rows([scores(page_refs[g * pp + j][0, 0].astype(BF16)) for g in range(spg)])
        key_ref[c * pp + j, pl.ds(r0, qs), :] = _to_key(sc)

    @pl.when(c == pl.num_programs(1) - 1)
    def _():
        sc = own_rows([scores(iknew_ref[g]) for g in range(spg)])
        sm = jnp.where(lane_q <= row % t_new, sc, MASK_VALUE)
        key_ref[n_pages, pl.ds(r0, qs), :] = jnp.where(lane_q < t_new, _to_key(sm), INT_MIN)

    @pl.when((c == pl.num_programs(1) - 1) & (b == pl.num_programs(0) - 1))
    def _():
        n_all = n_pages + 1
        lane = _iota((nrow, LANES), 1)

        def count(pred):
            def body(pg, acc):
                return acc + jnp.where(pred(key_ref[pg], pg), 1, 0)
            acc = lax.fori_loop(0, n_all, body, jnp.zeros((nrow, LANES), I32))
            return jnp.sum(acc, axis=1, keepdims=True)

        def count_ge(cand):
            cb = jnp.broadcast_to(cand, (nrow, LANES))
            return count(lambda blk, pg: blk >= cb)

        thr, n_ge = _kth_largest_key(count_ge, topk, (nrow, 1), n_all * LANES)
        thr = jnp.broadcast_to(thr, (nrow, LANES))
        cut_ref[...] = jnp.full((nrow, LANES), 2 ** 30, I32)

        @pl.when(jnp.max(n_ge) > topk)
        def _():
            need = topk - count(lambda blk, pg: blk > thr)

            def body(t, ans):
                cand = ans + jnp.left_shift(jnp.int32(1), 15 - t)
                cb = jnp.broadcast_to(cand - 1, (nrow, LANES))
                upto = count(lambda blk, pg: (blk == thr) & ((pg * LANES + lane) <= cb))
                return jnp.where(upto < need, cand, ans)
            last = lax.fori_loop(0, 16, body, jnp.zeros((nrow, 1), I32))
            cut_ref[...] = jnp.broadcast_to(last, (nrow, LANES))

        last = cut_ref[...]
        lo = jnp.maximum(thr, VALID_KEY)
        tie_ok = thr > VALID_KEY

        def write(pg, carry):
            blk = key_ref[pg]
            sel = (blk > lo) | ((blk == thr) & tie_ok & ((pg * LANES + lane) <= last))
            mask_ref[pg] = jnp.where(sel, 0.0, MASK_VALUE)
            return carry

        lax.fori_loop(0, n_all, write, 0)


def _idx_sample(page_table, iq_rows, iw_col, ik_new_t, cache_kidx_t, layer, pp, t_new):
    nseq, n_pages = page_table.shape
    topk = min(TOPK_MAX, (n_pages * PAGE_SIZE + t_new) // 4)
    qs = SUBLANES
    spg = qs // t_new
    assert qs % t_new == 0 and nseq % spg == 0, (nseq, t_new)
    nrow = nseq * t_new

    def page_spec(g, j):
        return pl.BlockSpec((1, 1, IDX_DIM, PAGE_SIZE), lambda b, c, pt: (layer, pt[b * spg + g, c * pp + j], 0, 0))

    grid_spec = pltpu.PrefetchScalarGridSpec(
        num_scalar_prefetch=1,
        grid=(nseq // spg, n_pages // pp),
        in_specs=[pl.BlockSpec((1, qs * IDX_HEADS, IDX_DIM), lambda b, c, pt: (b, 0, 0)),
                  pl.BlockSpec((1, qs * IDX_HEADS, 1), lambda b, c, pt: (b, 0, 0)),
                  pl.BlockSpec((spg, IDX_DIM, PAGE_SIZE), lambda b, c, pt: (b, 0, 0))]
                 + [page_spec(g, j) for g in range(spg) for j in range(pp)],
        out_specs=pl.BlockSpec((n_pages + 1, nrow, LANES), lambda b, c, pt: (0, 0, 0)),
        scratch_shapes=[pltpu.VMEM((n_pages + 1, nrow, LANES), I32),
                        pltpu.VMEM((nrow, LANES), I32)],
    )
    return pl.pallas_call(
        functools.partial(_idx_sample_kernel, pp=pp, n_pages=n_pages, topk=topk, t_new=t_new),
        grid_spec=grid_spec,
        out_shape=jax.ShapeDtypeStruct((n_pages + 1, nrow, LANES), F32),
        compiler_params=_cparams("arbitrary", "arbitrary"),
        name="idx_sample",
    )(page_table, iq_rows, iw_col, ik_new_t, *([cache_kidx_t] * (spg * pp)))


def _attn_sample_kernel(pt_ref, q_ref, mask_ref, masknew_ref, knew_ref, vnew_ref, *rest, pp, t_new):
    k_refs = rest[:pp]
    v_refs = rest[pp:2 * pp]
    o_ref = rest[2 * pp]
    m_ref, l_ref, acc_ref = rest[2 * pp + 1:]
    del pt_ref
    c = pl.program_id(1)
    nrow = t_new * A_HEADS
    q = q_ref[0]

    @pl.when(c == 0)
    def _():
        m_ref[...] = jnp.full(m_ref.shape, MASK_VALUE, F32)
        l_ref[...] = jnp.zeros(l_ref.shape, F32)
        acc_ref[...] = jnp.zeros(acc_ref.shape, F32)

    def update(k_pages, v_pages, masks):
        s = []
        for kt, mk in zip(k_pages, masks):
            mrows = jnp.concatenate([jnp.broadcast_to(mk[t:t + 1, :], (A_HEADS, LANES)) for t in range(t_new)], axis=0)
            s.append(_dot(q, kt) + mrows)
        m_blk = s[0]
        for t in s[1:]:
            m_blk = jnp.maximum(m_blk, t)
        m_old = m_ref[...]
        m_new = jnp.maximum(m_old, jnp.max(m_blk, axis=1, keepdims=True))
        alpha = jnp.exp2(m_old - m_new)
        l_sum = jnp.zeros((nrow, LANES), F32)
        acc = alpha * acc_ref[...]
        for t, vt in zip(s, v_pages):
            e = jnp.exp2(t - m_new)
            l_sum = l_sum + e
            acc = acc + _nt(e.astype(BF16), vt)
        l_ref[...] = alpha * l_ref[...] + jnp.sum(l_sum, axis=1, keepdims=True)
        acc_ref[...] = acc
        m_ref[...] = m_new

    update([r[0, 0].astype(BF16) for r in k_refs], [r[0, 0].astype(BF16) for r in v_refs],
           [mask_ref[j] for j in range(pp)])

    @pl.when(c == pl.num_programs(1) - 1)
    def _():
        update([knew_ref[0]], [vnew_ref[0]], [masknew_ref[...]])
        o = acc_ref[...] / l_ref[...]
        grp = (_iota((nrow, 1), 0) % A_HEADS) // (A_HEADS // A_KV_HEADS)
        out = jnp.zeros((nrow, A_HEAD_DIM), F32)
        for g in range(A_KV_HEADS):
            out = out + jnp.where(grp == g, o[:, g * A_HEAD_DIM:(g + 1) * A_HEAD_DIM], 0.0)
        o_ref[0] = out


def _attn_sample(page_table, q_rows, mask, k_new_t, v_new_t, cache_k_t, cache_v_t, layer, pp, t_new):
    nseq, n_pages = page_table.shape
    kvw = A_KV_HEADS * A_HEAD_DIM
    nrow = t_new * A_HEADS
    mask4 = mask.reshape(n_pages + 1, nseq, t_new, LANES)

    def page_spec(j):
        return pl.BlockSpec((1, 1, kvw, PAGE_SIZE), lambda b, c, pt: (layer, pt[b, c * pp + j], 0, 0))

    grid_spec = pltpu.PrefetchScalarGridSpec(
        num_scalar_prefetch=1,
        grid=(nseq, n_pages // pp),
        in_specs=[pl.BlockSpec((1, nrow, kvw), lambda b, c, pt: (b, 0, 0)),
                  pl.BlockSpec((pp, None, t_new, LANES), lambda b, c, pt: (c, b, 0, 0)),
                  pl.BlockSpec((None, None, t_new, LANES), lambda b, c, pt: (n_pages, b, 0, 0)),
                  pl.BlockSpec((1, kvw, PAGE_SIZE), lambda b, c, pt: (b, 0, 0)),
                  pl.BlockSpec((1, kvw, PAGE_SIZE), lambda b, c, pt: (b, 0, 0))]
                 + [page_spec(j) for j in range(pp)] * 2,
        out_specs=pl.BlockSpec((1, nrow, A_HEAD_DIM), lambda b, c, pt: (b, 0, 0)),
        scratch_shapes=[pltpu.VMEM((nrow, 1), F32), pltpu.VMEM((nrow, 1), F32), pltpu.VMEM((nrow, kvw), F32)],
    )
    return pl.pallas_call(
        functools.partial(_attn_sample_kernel, pp=pp, t_new=t_new),
        grid_spec=grid_spec,
        out_shape=jax.ShapeDtypeStruct((nseq, nrow, A_HEAD_DIM), F32),
        compiler_params=_cparams("parallel", "arbitrary"),
        name="attn_sample",
    )(page_table, q_rows, mask4, mask4, k_new_t, v_new_t, *([cache_k_t] * pp), *([cache_v_t] * pp))


def _sprep_kernel(aq_ref, ak_ref, iq_ref, ikw_ref, qg_ref, kg_ref, qn_ref, kn_ref, iqs_ref, iw_ref):
    for c, t in enumerate(_head_rmsnorm(aq_ref[...], qg_ref[...])):
        qn_ref[:, c * LANES:(c + 1) * LANES] = (t * (A_HEAD_DIM ** -0.5 * LOG2_E)).astype(BF16)
    for c, t in enumerate(_head_rmsnorm(ak_ref[...], kg_ref[...])):
        kn_ref[:, c * LANES:(c + 1) * LANES] = t
    iqs_ref[...] = (iq_ref[...] * (IDX_DIM ** -0.5)).astype(BF16)
    iw_ref[...] = ikw_ref[...] * (IDX_HEADS ** -0.5)


def _sprep(p, q_gain, k_gain):
    n = p.shape[0]
    kvw = A_KV_HEADS * A_HEAD_DIM
    return pl.pallas_call(
        _sprep_kernel,
        grid=(1,),
        in_specs=[pl.BlockSpec((n, W), lambda i: (0, C_AQ // W)),
                  pl.BlockSpec((n, kvw), lambda i: (0, C_AK // kvw)),
                  pl.BlockSpec((n, W), lambda i: (0, C_IQ // W)),
                  pl.BlockSpec((n, LANES), lambda i: (0, C_IKW // LANES)),
                  pl.BlockSpec((1, LANES), lambda i: (0, 0)),
                  pl.BlockSpec((1, LANES), lambda i: (0, 0))],
        out_specs=[pl.BlockSpec((n, W), lambda i: (0, 0)), pl.BlockSpec((n, kvw), lambda i: (0, 0)),
                   pl.BlockSpec((n, W), lambda i: (0, 0)), pl.BlockSpec((n, LANES), lambda i: (0, 0))],
        out_shape=[jax.ShapeDtypeStruct((n, W), BF16), jax.ShapeDtypeStruct((n, kvw), F32),
                   jax.ShapeDtypeStruct((n, W), BF16), jax.ShapeDtypeStruct((n, LANES), F32)],
        compiler_params=_cparams("arbitrary"),
        name="sprep",
    )(p, p, p, p, q_gain, k_gain)


def _merge_kernel(oa_ref, ob_ref, oc_ref, od_ref, gates_ref, x_ref, wb_ref, wo_ref, n2_ref, x1_ref, h2_ref):
    merged = None
    for b, o_ref in enumerate((oa_ref, ob_ref, oc_ref, od_ref)):
        up = _dot(o_ref[...], wb_ref[b])
        t = _sigmoid(gates_ref[:, b * D_MODEL:(b + 1) * D_MODEL].astype(F32)) * up
        merged = t if merged is None else merged + t
    x1 = x_ref[...] + _dot(merged.astype(BF16), wo_ref[...])
    x1_ref[...] = x1
    ms = jnp.mean(x1 * x1, axis=-1, keepdims=True)
    h2_ref[...] = (x1 * lax.rsqrt(ms + RMS_EPS) * n2_ref[...]).astype(BF16)


def _merge(oa, ob, oc, od, p, x, w_branch, w_out, norm2, layer, tm):
    n = x.shape[0]
    row = lambda wdt: pl.BlockSpec((tm, wdt), lambda i: (i, 0))
    return pl.pallas_call(
        _merge_kernel,
        grid=(n // tm,),
        in_specs=[row(W), row(W), row(W), row(W),
                  pl.BlockSpec((tm, N_BRANCH * D_MODEL), lambda i: (i, C_GATES // (N_BRANCH * D_MODEL))),
                  row(D_MODEL),
                  pl.BlockSpec((None, N_BRANCH, W, D_MODEL), lambda i: (layer, 0, 0, 0)),
                  pl.BlockSpec((None, D_MODEL, D_MODEL), lambda i: (layer, 0, 0)),
                  pl.BlockSpec((1, D_MODEL), lambda i: (0, 0))],
        out_specs=[row(D_MODEL), row(D_MODEL)],
        out_shape=[jax.ShapeDtypeStruct((n, D_MODEL), F32), jax.ShapeDtypeStruct((n, D_MODEL), BF16)],
        compiler_params=_cparams("parallel"),
        name="merge",
    )(oa, ob, oc, od, p, x, w_branch, w_out, norm2)


def _ffn_kernel(h_ref, x_ref, wg_ref, wu_ref, wd_ref, o_ref, acc_ref):
    j = pl.program_id(1)

    @pl.when(j == 0)
    def _():
        acc_ref[...] = x_ref[...]

    h = h_ref[...]
    act = _silu(_dot(h, wg_ref[...])) * _dot(h, wu_ref[...])
    acc_ref[...] += _dot(act.astype(BF16), wd_ref[...])

    @pl.when(j == pl.num_programs(1) - 1)
    def _():
        o_ref[...] = acc_ref[...]


def _ffn(h2, x1, w_gu, w_down, layer, tm, fc):
    n = x1.shape[0]
    nf = D_FF // fc
    return pl.pallas_call(
        _ffn_kernel,
        grid=(n // tm, nf),
        in_specs=[pl.BlockSpec((tm, D_MODEL), lambda i, j: (i, 0)),
                  pl.BlockSpec((tm, D_MODEL), lambda i, j: (i, 0)),
                  pl.BlockSpec((None, D_MODEL, fc), lambda i, j: (layer, 0, j)),
                  pl.BlockSpec((None, D_MODEL, fc), lambda i, j: (layer, 0, nf + j)),
                  pl.BlockSpec((None, fc, D_MODEL), lambda i, j: (layer, j, 0))],
        out_specs=pl.BlockSpec((tm, D_MODEL), lambda i, j: (i, 0)),
        out_shape=jax.ShapeDtypeStruct((n, D_MODEL), F32),
        scratch_shapes=[pltpu.VMEM((tm, D_MODEL), F32)],
        compiler_params=_cparams("parallel", "arbitrary"),
        name="ffn",
    )(h2, x1, w_gu, w_gu, w_down)


def _pack_w_in(w_in):
    (a_q, a_k, a_v, a_iq, a_ik, a_iw, r_q, r_k, r_v, r_g, c_q, c_f, c_i, c_g,
     m_z, m_xbc, m_dt, gates) = jnp.split(w_in, SPLIT_POINTS, axis=-1)

    def pad_to_lanes(a):
        return jnp.pad(a, ((0, 0), (0, 0), (0, LANES - a.shape[-1])))

    ikw = pad_to_lanes(jnp.concatenate([a_ik, a_iw], axis=-1))
    packed = jnp.concatenate([gates, m_xbc, a_q, a_iq, r_q, r_k, r_v, r_g, c_q, c_f, c_i, c_g, m_z,
                              a_k, a_v, ikw, pad_to_lanes(m_dt)], axis=-1)
    return packed.astype(BF16)


def _rotary_tables(pos):
    half = R_DK // 2
    inv_freq = ROPE_BASE ** (-jnp.arange(half, dtype=F32) / half)
    ang = pos.astype(F32)[:, None] * inv_freq[None, :]
    c, s = jnp.cos(ang), jnp.sin(ang)
    return jnp.concatenate([c, c], axis=-1), jnp.concatenate([-s, s], axis=-1)


def _lower_bounds(lb_param):
    pr = jax.nn.softmax(lb_param.astype(F32), axis=0)
    return jnp.cumsum(pr, axis=0) - pr[0]


def _pad_lanes_row(v):
    return jnp.pad(v, (0, LANES - v.shape[0]))[None, :]


def kernel(x_prompt, x_sample, cache_k, cache_v, cache_kidx, state_ret, state_hgrn, state_ssm, state_conv,
           page_table, norm1_g, w_in, q_norm_g, k_norm_g, lb_param, hgrn_norm_g, A_log, dt_bias, D_skip,
           conv_w, conv_b, ssm_norm_g, w_branch, w_out, norm2_g, w_gu, w_down):
    bp, tp, _ = x_prompt.shape
    bs, ts, _ = x_sample.shape
    depth = w_in.shape[0]
    n_pool = cache_k.shape[1]
    n_pages = page_table.shape[1]
    kvw = A_KV_HEADS * A_HEAD_DIM
    tsp = SUBLANES
    npair = M_HEADS // 2
    gs = math.gcd(bs, 8)

    w_in_p = _pack_w_in(w_in)
    w_branch_b = w_branch.astype(BF16)
    w_out_b = w_out.astype(BF16)
    w_gu_b = w_gu.astype(BF16)
    w_down_b = w_down.astype(BF16)
    lbs = _lower_bounds(lb_param)
    cos_p, sin_p = _rotary_tables(jnp.arange(tp, dtype=jnp.int32))
    pos_s = n_pages * PAGE_SIZE + jnp.arange(tsp, dtype=jnp.int32)
    cos_s, sin_s = _rotary_tables(pos_s)
    ck_t = jnp.transpose(cache_k, (0, 1, 3, 4, 2)).reshape(depth, n_pool, kvw, PAGE_SIZE)
    cv_t = jnp.transpose(cache_v, (0, 1, 3, 4, 2)).reshape(depth, n_pool, kvw, PAGE_SIZE)
    ckidx_t = jnp.transpose(cache_kidx, (0, 1, 3, 2))
    head_group = (jnp.arange(A_HEADS) // (A_HEADS // A_KV_HEADS))[:, None] == jnp.arange(A_KV_HEADS)[None, :]
    ssm0 = state_ssm.reshape(depth, bs, npair, 2 * M_HEAD_DIM, M_STATE)
    buf0 = jnp.pad(state_conv, ((0, 0), (0, 0), (SUBLANES - (M_CONV - 1), 0), (0, 0)))

    zero_ret = jnp.zeros((bp, R_HEADS, R_DK, R_DV), F32)
    zero_hgrn = jnp.zeros((bp, H_HEADS, H_DK, H_DV), F32)
    zero_ssm = jnp.zeros((bp, npair, 2 * M_HEAD_DIM, M_STATE), F32)
    zero_buf = jnp.zeros((bp, SUBLANES, M_CONV_DIM), F32)

    xp = x_prompt.reshape(bp * tp, D_MODEL)
    xs = x_sample.reshape(bs * ts, D_MODEL)
    outs_p = [[] for _ in range(7)]
    outs_s = [[] for _ in range(7)]
    for l in range(depth):
        n1 = norm1_g[l][None, :]
        n2 = norm2_g[l][None, :]
        qg = jnp.tile(q_norm_g[l], 2)[None, :]
        kg = jnp.tile(k_norm_g[l], 2)[None, :]
        lb = lbs[l][None, :]
        hng = hgrn_norm_g[l][None, :]
        alog = _pad_lanes_row(A_log[l])
        dtb = _pad_lanes_row(dt_bias[l])
        dsk = jnp.repeat(D_skip[l], M_HEAD_DIM)[None, :]
        sng = ssm_norm_g[l][None, :]
        cw = conv_w[l]
        cb = conv_b[l][None, :]

        p = _inproj(xp, n1, w_in_p, l, 1024, 1280, BF16)
        kn, vv, ik, knb, vt, ik2 = _kprep(p, kg, 512)
        oa = _attn_prompt(p, knb, vt, ik2, qg, bp, tp, 256, 256)
        ob, ret_new = _retention(p, cos_p, sin_p, zero_ret, bp, tp, 512, R_CHUNK, R_CHUNK)
        oc, hgrn_new = _hgrn2(p, lb, hng, zero_hgrn, bp, tp, 512, H_CHUNK, H_CHUNK, 16)
        od, ssm_new, buf_new = _ssd(p, cw, cb, alog, dtb, dsk, sng, zero_ssm, zero_buf, bp, tp, 512, M_CHUNK, M_CHUNK)
        x1, h2 = _merge(oa, ob, oc, od, p, xp, w_branch_b, w_out_b, n2, l, 512)
        xp = _ffn(h2, x1, w_gu_b, w_down_b, l, 512, D_FF // 2)
        for i, a in enumerate((kn.reshape(bp, tp, A_KV_HEADS, A_HEAD_DIM), vv.reshape(bp, tp, A_KV_HEADS, A_HEAD_DIM),
                               ik.reshape(bp, tp, IDX_DIM), ret_new, hgrn_new,
                               ssm_new.reshape(bp, M_HEADS, M_HEAD_DIM, M_STATE),
                               buf_new[:, SUBLANES - (M_CONV - 1):, :])):
            outs_p[i].append(a)

        ps = _inproj(xs, n1, w_in_p, l, bs * ts, 1280, F32)
        qn_s, kn_s, iq_s, iw_s = _sprep(ps, qg, kg)
        v_s = ps[:, C_AV:C_AV + kvw]
        ik_s = ps[:, C_IKW:C_IKW + IDX_DIM]
        pad_q = lambda a: jnp.pad(a, ((0, 0), (0, tsp - ts)) + ((0, 0),) * (a.ndim - 2))
        new_keys_t = lambda a: jnp.swapaxes(jnp.pad(a.reshape(bs, ts, -1), ((0, 0), (0, PAGE_SIZE - ts), (0, 0))), 1, 2)
        spg = tsp // ts
        head_major = lambda a: jnp.swapaxes(a.reshape(bs // spg, tsp, IDX_HEADS, -1), 1, 2).reshape(bs // spg, IDX_HEADS * tsp, -1)
        iq_rows = head_major(iq_s)
        iw_col = head_major(iw_s[:, IDX_DIM:IDX_DIM + IDX_HEADS])
        mask = _idx_sample(page_table, iq_rows, iw_col, new_keys_t(ik_s).astype(BF16), ckidx_t, l, min(32, n_pages), ts)
        q4 = qn_s.reshape(bs, ts, A_HEADS, 1, A_HEAD_DIM)
        q_rows = jnp.where(head_group[None, None, :, :, None], q4, jnp.zeros_like(q4)).reshape(bs, ts * A_HEADS, kvw)
        oa_s = _attn_sample(page_table, q_rows, mask, new_keys_t(kn_s).astype(BF16), new_keys_t(v_s).astype(BF16),
                            ck_t, cv_t, l, min(64, n_pages), ts)
        oa_s = oa_s.reshape(bs * ts, W).astype(BF16)
        psp = pad_q(ps.reshape(bs, ts, NP)).reshape(bs * tsp, NP)
        ob_s, ret_s = _retention(psp, cos_s, sin_s, state_ret, bs, tsp, tsp, tsp, ts, gs, l)
        oc_s, hgrn_s = _hgrn2(psp, lb, hng, state_hgrn, bs, tsp, tsp, tsp, ts, tsp, gs, l)
        od_s, ssm_s, buf_s = _ssd(psp, cw, cb, alog, dtb, dsk, sng, ssm0, buf0, bs, tsp, tsp, tsp, ts, gs, l)
        unpad = lambda a: a.reshape(bs, tsp, W)[:, :ts].reshape(bs * ts, W)
        x1s, h2s = _merge(oa_s, unpad(ob_s), unpad(oc_s), unpad(od_s), ps, xs, w_branch_b, w_out_b, n2, l, bs * ts)
        xs = _ffn(h2s, x1s, w_gu_b, w_down_b, l, bs * ts, D_FF // 2)
        for i, a in enumerate((kn_s.reshape(bs, ts, A_KV_HEADS, A_HEAD_DIM), v_s.reshape(bs, ts, A_KV_HEADS, A_HEAD_DIM),
                               ik_s.reshape(bs, ts, IDX_DIM), ret_s, hgrn_s,
                               ssm_s.reshape(bs, M_HEADS, M_HEAD_DIM, M_STATE),
                               buf_s[:, SUBLANES - (M_CONV - 1):, :])):
            outs_s[i].append(a)

    dtypes = (cache_k.dtype, cache_v.dtype, cache_kidx.dtype, state_ret.dtype,
              state_hgrn.dtype, state_ssm.dtype, state_conv.dtype)
    res_p = [jnp.stack(a).astype(d) for a, d in zip(outs_p, dtypes)]
    res_s = [jnp.stack(a).astype(d) for a, d in zip(outs_s, dtypes)]
    return (xp.reshape(bp, tp, D_MODEL), xs.reshape(bs, ts, D_MODEL), *res_p, *res_s)
```
